```python
import jax, jax.numpy as jnp
from jax import lax
import numpy as np

D_MODEL = 1024
BATCH = 2
SEQ = 8192
DEPTH = 4
DEC_BATCH = 128
DEC_SEQ = 1
PAST_LEN = 8192
PAGE_SIZE = 128

HEAD_DIM = 64
D_MIX = D_MODEL
RW_WIDTH = D_MIX // 2
SW_WIDTH = D_MIX - RW_WIDTH
RW_HEADS = RW_WIDTH // HEAD_DIM
SW_HEADS = SW_WIDTH // HEAD_DIM
SW_KV_HEADS = 2
SW_GROUP = SW_HEADS // SW_KV_HEADS
KV_WIDTH = SW_KV_HEADS * HEAD_DIM
WINDOW = 128
BLOCK = WINDOW
D_W_LORA = 64
D_A_LORA = 64
D_G_LORA = 128
RW_COLS = 3 * RW_WIDTH + D_W_LORA + D_A_LORA + D_G_LORA
IN_COLS = RW_COLS + SW_WIDTH + 2 * KV_WIDTH
RW_SPLIT = (RW_WIDTH, 2 * RW_WIDTH, 3 * RW_WIDTH, 3 * RW_WIDTH + D_W_LORA, 3 * RW_WIDTH + D_W_LORA + D_A_LORA)
IN_SPLIT = (RW_COLS, RW_COLS + SW_WIDTH, RW_COLS + SW_WIDTH + KV_WIDTH)
MEM_TOKENS = 256
MEM_HEADS = 4
MEM_HEAD_DIM = D_MODEL // MEM_HEADS
D_FF = 2816
ALPHA = (2.0 * DEPTH) ** 0.25
BETA = (8.0 * DEPTH) ** -0.25
LN_EPS = 1e-5
GN_EPS = 64e-5
NORM_EPS = 1e-12

kernel_name = 'hymba_rwkv7_swa_alibi_deepnorm_decode_step'


def alibi_slopes(n):
    return jnp.asarray(2.0 ** (-8.0 * np.arange(1, n + 1) / n), jnp.float32)


def layer_norm(x, g, b):
    xf = x.astype(jnp.float32)
    mu = jnp.mean(xf, axis=-1, keepdims=True)
    var = jnp.mean(jnp.square(xf - mu), axis=-1, keepdims=True)
    return (xf - mu) * lax.rsqrt(var + LN_EPS) * g + b


def post_norm(x, f, g, b):
    return layer_norm(ALPHA * x + f, g, b).astype(x.dtype)


def ffn_half(x, w1, w3, w2):
    return 0.5 * ((jax.nn.silu(x @ w1) * (x @ w3)) @ w2)


def sink_softmax(logits, sink):
    m = jnp.maximum(jnp.max(logits, axis=-1, keepdims=True), sink)
    e = jnp.exp(logits - m)
    return e / (jnp.sum(e, axis=-1, keepdims=True) + jnp.exp(sink - m))


def wkv_step(s, inp):
    r, w, k, v, kk, kka = inp
    s = (s * w[:, :, None, :]
         - jnp.einsum('bhvk,bhk->bhv', s, kk)[..., None] * kka[:, :, None, :]
         + v[..., None] * k[:, :, None, :])
    return s, jnp.einsum('bhvk,bhk->bhv', s, r)


def rwkv_group(p, p_prev, s0, mu, w0, w_up, a0, a_up, g_up, k_k, k_a, r_k, gn_g, gn_b):
    f32 = jnp.float32
    b, t, _ = p.shape
    xm = p + (p_prev - p) * mu
    r, k, v, wl, al, gl = jnp.split(xm, RW_SPLIT, axis=-1)
    w_log = -jax.nn.softplus(-(w0 + jnp.tanh(wl) @ w_up).astype(f32)) - 0.5
    decay = jnp.exp(-jnp.exp(w_log))
    a = jax.nn.sigmoid((a0 + al @ a_up).astype(f32))
    g = jax.nn.sigmoid(gl) @ g_up
    heads = lambda z: z.astype(f32).reshape(b, t, RW_HEADS, HEAD_DIM)
    kk = heads(k * k_k)
    kk = kk / jnp.maximum(jnp.linalg.norm(kk, axis=-1, keepdims=True), NORM_EPS)
    k_mod = k.astype(f32) * (1.0 + (a - 1.0) * k_a)
    r_h, k_h, v_h, w_h, a_h = heads(r), heads(k_mod), heads(v), heads(decay), heads(a)
    xs = tuple(jnp.moveaxis(z, 1, 0) for z in (r_h, w_h, k_h, v_h, kk, kk * a_h))
    s_fin, y = lax.scan(wkv_step, s0.astype(f32), xs)
    y = jnp.moveaxis(y, 0, 1)
    mean = jnp.mean(y, axis=-1, keepdims=True)
    var = jnp.mean(jnp.square(y - mean), axis=-1, keepdims=True)
    y = ((y - mean) * lax.rsqrt(var + GN_EPS)).reshape(b, t, RW_WIDTH) * gn_g + gn_b
    bonus = jnp.sum(r_h * k_h * r_k, axis=-1, keepdims=True) * v_h
    y = (y + bonus.reshape(b, t, RW_WIDTH)) * g
    return y.astype(p.dtype), s_fin


def swa_prompt(q, k, v, sinks, slopes):
    f32 = jnp.float32
    b, t, _ = q.shape
    nb = t // BLOCK
    qb = q.reshape(b, nb, BLOCK, SW_KV_HEADS, SW_GROUP, HEAD_DIM).astype(f32)
    kb = k.reshape(b, nb, BLOCK, SW_KV_HEADS, HEAD_DIM).astype(f32)
    vb = v.reshape(b, nb, BLOCK, SW_KV_HEADS, HEAD_DIM).astype(f32)
    pad = jnp.zeros_like(kb[:, :1])
    kw = jnp.concatenate([jnp.concatenate([pad, kb[:, :-1]], axis=1), kb], axis=2)
    vw = jnp.concatenate([jnp.concatenate([pad, vb[:, :-1]], axis=1), vb], axis=2)
    s = jnp.einsum('bnicgd,bnjcd->bncgij', qb, kw) * HEAD_DIM ** -0.5
    i = jnp.arange(BLOCK)[:, None]
    j = jnp.arange(2 * BLOCK)[None, :]
    dist = BLOCK + i - j
    blk = jnp.arange(nb)[:, None, None]
    valid = (dist >= 0) & (dist < WINDOW) & ((blk - 1) * BLOCK + j >= 0)
    sl = slopes.reshape(SW_KV_HEADS, SW_GROUP)[:, :, None, None]
    logits = jnp.where(valid[None, :, None, None], s - sl * dist.astype(f32), -jnp.inf)
    p = sink_softmax(logits, sinks.astype(f32).reshape(SW_KV_HEADS, SW_GROUP)[:, :, None, None])
    o = jnp.einsum('bncgij,bnjcd->bnicgd', p, vw)
    return o.reshape(b, t, SW_WIDTH).astype(q.dtype)


def swa_sample(q, k, v, buf_k, buf_v, sinks, slopes):
    f32 = jnp.float32
    b, t, _ = q.shape
    wb = buf_k.shape[1]
    qh = q.reshape(b, t, SW_KV_HEADS, SW_GROUP, HEAD_DIM).astype(f32)
    kw = jnp.concatenate([buf_k, k.reshape(b, t, SW_KV_HEADS, HEAD_DIM)], axis=1)
    vw = jnp.concatenate([buf_v, v.reshape(b, t, SW_KV_HEADS, HEAD_DIM)], axis=1)
    s = jnp.einsum('bicgd,bjcd->bcgij', qh, kw.astype(f32)) * HEAD_DIM ** -0.5
    dist = wb + jnp.arange(t)[:, None] - jnp.arange(wb + t)[None, :]
    valid = (dist >= 0) & (dist < WINDOW)
    sl = slopes.reshape(SW_KV_HEADS, SW_GROUP)[:, :, None, None]
    logits = jnp.where(valid, s - sl * dist.astype(f32), -jnp.inf)
    p = sink_softmax(logits, sinks.astype(f32).reshape(SW_KV_HEADS, SW_GROUP)[:, :, None, None])
    o = jnp.einsum('bcgij,bjcd->bicgd', p, vw.astype(f32))
    return o.reshape(b, t, SW_WIDTH).astype(q.dtype), kw[:, -wb:], vw[:, -wb:]


def mem_attend(x, mk, mv, wq, wo):
    f32 = jnp.float32
    b, t, _ = x.shape
    q = (x @ wq).reshape(b, t, MEM_HEADS, MEM_HEAD_DIM)
    s = jnp.einsum('bthd,bmhd->bhtm', q.astype(f32), mk.astype(f32)) * MEM_HEAD_DIM ** -0.5
    p = jax.nn.softmax(s, axis=-1)
    o = jnp.einsum('bhtm,bmhd->bthd', p, mv.astype(f32))
    return o.reshape(b, t, D_MODEL).astype(x.dtype) @ wo


def setup_inputs(seed: int = 0) -> dict:
    key = jax.random.key(seed)
    ks = iter(jax.random.split(key, 48))
    nrm = lambda shape, scale: scale * jax.random.normal(next(ks), shape, jnp.float32)
    unif = lambda shape, lo, hi: jax.random.uniform(next(ks), shape, jnp.float32, lo, hi)
    L = DEPTH
    win_buf = min(WINDOW, PAST_LEN)
    return {
        'x_prompt': nrm((BATCH, SEQ, D_MODEL), 1.0),
        'x_sample': nrm((DEC_BATCH, DEC_SEQ, D_MODEL), 1.0),
        'mem_prompt': nrm((BATCH, MEM_TOKENS, D_MODEL), 1.0),
        'state_wkv': nrm((L, DEC_BATCH, RW_HEADS, HEAD_DIM, HEAD_DIM), 0.3),
        'state_shift': nrm((L, DEC_BATCH, D_MODEL), 1.0),
        'cache_win_k': nrm((L, DEC_BATCH, win_buf, SW_KV_HEADS, HEAD_DIM), 1.0),
        'cache_win_v': nrm((L, DEC_BATCH, win_buf, SW_KV_HEADS, HEAD_DIM), 1.0),
        'cache_mem_k': nrm((L, DEC_BATCH, MEM_TOKENS, MEM_HEADS, MEM_HEAD_DIM), 1.0),
        'cache_mem_v': nrm((L, DEC_BATCH, MEM_TOKENS, MEM_HEADS, MEM_HEAD_DIM), 1.0),
        'ln_g': 1.0 + nrm((L, 4, D_MODEL), 0.02),
        'ln_b': nrm((L, 4, D_MODEL), 0.02),
        'ffn_w1': nrm((L, 2, D_MODEL, D_FF), D_MODEL ** -0.5),
        'ffn_w3': nrm((L, 2, D_MODEL, D_FF), D_MODEL ** -0.5),
        'ffn_w2': nrm((L, 2, D_FF, D_MODEL), BETA * D_FF ** -0.5),
        'w_in': nrm((L, D_MODEL, IN_COLS), D_MODEL ** -0.5),
        'rw_mu': unif((L, RW_COLS), 0.0, 1.0),
        'rw_w0': unif((L, RW_WIDTH), -5.0, 1.0),
        'rw_w_up': nrm((L, D_W_LORA, RW_WIDTH), 0.5 * D_W_LORA ** -0.5),
        'rw_a0': nrm((L, RW_WIDTH), 0.5),
        'rw_a_up': nrm((L, D_A_LORA, RW_WIDTH), 0.5 * D_A_LORA ** -0.5),
        'rw_g_up': nrm((L, D_G_LORA, RW_WIDTH), D_G_LORA ** -0.5),
        'rw_k_k': 0.85 + nrm((L, RW_WIDTH), 0.02),
        'rw_k_a': 1.0 + nrm((L, RW_WIDTH), 0.02),
        'rw_r_k': nrm((L, RW_HEADS, HEAD_DIM), 0.1),
        'rw_gn_g': 1.0 + nrm((L, RW_WIDTH), 0.02),
        'rw_gn_b': nrm((L, RW_WIDTH), 0.02),
        'sw_sinks': nrm((L, SW_HEADS), 0.5),
        'w_out': nrm((L, D_MIX, D_MODEL), BETA * D_MIX ** -0.5),
        'mem_wq': nrm((L, D_MODEL, D_MODEL), D_MODEL ** -0.5),
        'mem_wk': nrm((L, D_MODEL, D_MODEL), D_MODEL ** -0.5),
        'mem_wv': nrm((L, D_MODEL, D_MODEL), D_MODEL ** -0.5),
        'mem_wo': nrm((L, D_MODEL, D_MODEL), BETA * D_MODEL ** -0.5),
    }


def reference(x_prompt, x_sample, mem_prompt, state_wkv, state_shift, cache_win_k, cache_win_v,
              cache_mem_k, cache_mem_v, ln_g, ln_b, ffn_w1, ffn_w3, ffn_w2, w_in, rw_mu, rw_w0,
              rw_w_up, rw_a0, rw_a_up, rw_g_up, rw_k_k, rw_k_a, rw_r_k, rw_gn_g, rw_gn_b, sw_sinks,
              w_out, mem_wq, mem_wk, mem_wv, mem_wo):
    slopes = alibi_slopes(SW_HEADS)
    wb = cache_win_k.shape[2]

    xp = x_prompt
    bp, tp, _ = xp.shape
    p_wkv, p_shift, p_wk, p_wv, p_mk, p_mv = [], [], [], [], [], []
    for l in range(DEPTH):
        rw = (rw_mu[l], rw_w0[l], rw_w_up[l], rw_a0[l], rw_a_up[l], rw_g_up[l], rw_k_k[l],
              rw_k_a[l], rw_r_k[l], rw_gn_g[l], rw_gn_b[l])
        xp = post_norm(xp, ffn_half(xp, ffn_w1[l, 0], ffn_w3[l, 0], ffn_w2[l, 0]), ln_g[l, 0], ln_b[l, 0])
        p_rw, q, k, v = jnp.split(xp @ w_in[l], IN_SPLIT, axis=-1)
        prev = jnp.concatenate([jnp.zeros_like(p_rw[:, :1]), p_rw[:, :-1]], axis=1)
        s0 = jnp.zeros((bp, RW_HEADS, HEAD_DIM, HEAD_DIM), jnp.float32)
        y_rw, s_fin = rwkv_group(p_rw, prev, s0, *rw)
        y_sw = swa_prompt(q, k, v, sw_sinks[l], slopes)
        p_wkv.append(s_fin)
        p_shift.append(xp[:, -1])
        p_wk.append(k.reshape(bp, tp, SW_KV_HEADS, HEAD_DIM)[:, -wb:])
        p_wv.append(v.reshape(bp, tp, SW_KV_HEADS, HEAD_DIM)[:, -wb:])
        xp = post_norm(xp, jnp.concatenate([y_rw, y_sw], axis=-1) @ w_out[l], ln_g[l, 1], ln_b[l, 1])
        mk = (mem_prompt @ mem_wk[l]).reshape(bp, MEM_TOKENS, MEM_HEADS, MEM_HEAD_DIM)
        mv = (mem_prompt @ mem_wv[l]).reshape(bp, MEM_TOKENS, MEM_HEADS, MEM_HEAD_DIM)
        p_mk.append(mk)
        p_mv.append(mv)
        xp = post_norm(xp, mem_attend(xp, mk, mv, mem_wq[l], mem_wo[l]), ln_g[l, 2], ln_b[l, 2])
        xp = post_norm(xp, ffn_half(xp, ffn_w1[l, 1], ffn_w3[l, 1], ffn_w2[l, 1]), ln_g[l, 3], ln_b[l, 3])

    xs = x_sample
    s_wkv, s_shift, s_wk, s_wv = [], [], [], []
    for l in range(DEPTH):
        rw = (rw_mu[l], rw_w0[l], rw_w_up[l], rw_a0[l], rw_a_up[l], rw_g_up[l], rw_k_k[l],
              rw_k_a[l], rw_r_k[l], rw_gn_g[l], rw_gn_b[l])
        xs = post_norm(xs, ffn_half(xs, ffn_w1[l, 0], ffn_w3[l, 0], ffn_w2[l, 0]), ln_g[l, 0], ln_b[l, 0])
        p_rw, q, k, v = jnp.split(xs @ w_in[l], IN_SPLIT, axis=-1)
        prev_row = (state_shift[l] @ w_in[l, :, :RW_COLS])[:, None]
        prev = jnp.concatenate([prev_row, p_rw[:, :-1]], axis=1)
        y_rw, s_fin = rwkv_group(p_rw, prev, state_wkv[l], *rw)
        y_sw, nk, nv = swa_sample(q, k, v, cache_win_k[l], cache_win_v[l], sw_sinks[l], slopes)
        s_wkv.append(s_fin)
        s_shift.append(xs[:, -1])
        s_wk.append(nk)
        s_wv.append(nv)
        xs = post_norm(xs, jnp.concatenate([y_rw, y_sw], axis=-1) @ w_out[l], ln_g[l, 1], ln_b[l, 1])
        xs = post_norm(xs, mem_attend(xs, cache_mem_k[l], cache_mem_v[l], mem_wq[l], mem_wo[l]), ln_g[l, 2], ln_b[l, 2])
        xs = post_norm(xs, ffn_half(xs, ffn_w1[l, 1], ffn_w3[l, 1], ffn_w2[l, 1]), ln_g[l, 3], ln_b[l, 3])

    return (xp, xs,
            jnp.stack(p_wkv), jnp.stack(p_shift), jnp.stack(p_wk), jnp.stack(p_wv),
            jnp.stack(p_mk), jnp.stack(p_mv),
            jnp.stack(s_wkv), jnp.stack(s_shift), jnp.stack(s_wk), jnp.stack(s_wv))
```

```python
import functools

import jax
import jax.numpy as jnp
from jax import lax
from jax.experimental import pallas as pl
from jax.experimental.pallas import tpu as pltpu

F32 = jnp.float32
BF16 = jnp.bfloat16

D_MODEL = 1024
HEAD_DIM = 64
RW_WIDTH = 512
RW_HEADS = 8
RW_PAIRS = RW_HEADS // 2
SW_WIDTH = 512
SW_HEADS = 8
SW_KV_HEADS = 2
SW_GROUP = SW_HEADS // SW_KV_HEADS
KV_WIDTH = SW_KV_HEADS * HEAD_DIM
WINDOW = 128
D_W_LORA = 64
D_A_LORA = 64
D_G_LORA = 128
RW_COLS = 3 * RW_WIDTH + D_W_LORA + D_A_LORA + D_G_LORA
LORA_WA_START = 3 * RW_WIDTH
LORA_G_START = LORA_WA_START + D_W_LORA + D_A_LORA
MEM_TOKENS = 256
MEM_HEADS = 4
MEM_HEAD_DIM = D_MODEL // MEM_HEADS
D_FF = 2816
DEPTH = 4
ALPHA = (2.0 * DEPTH) ** 0.25
LN_EPS = 1e-5
GN_EPS = 64e-5
NORM_EPS = 1e-12

LANES = 128
VMEM_LIMIT = 56 * 1024 * 1024


def _params(*semantics):
    return pltpu.CompilerParams(dimension_semantics=semantics, vmem_limit_bytes=VMEM_LIMIT)


def _row_tile(n, want):
    return want if n % want == 0 else n


def _layer_norm(z, g, b):
    mu = jnp.mean(z, axis=-1, keepdims=True)
    d = z - mu
    var = jnp.mean(d * d, axis=-1, keepdims=True)
    return d * lax.rsqrt(var + LN_EPS) * g + b


def _sigmoid(x):
    return 1.0 / (1.0 + jnp.exp(-x))


def _split3(x):
    hi = x.astype(BF16)
    r1 = x - hi.astype(F32)
    mid = r1.astype(BF16)
    lo = (r1 - mid.astype(F32)).astype(BF16)
    return jnp.concatenate([hi, mid, lo], axis=1)


def _split2(x):
    hi = x.astype(BF16)
    lo = (x - hi.astype(F32)).astype(BF16)
    return jnp.concatenate([hi, lo], axis=1)


def _mm_kernel(x_ref, *refs, n_out):
    xb = x_ref[...].astype(BF16)
    for w_ref, o_ref in zip(refs[:n_out], refs[n_out:]):
        o_ref[...] = jnp.dot(xb, w_ref[...], preferred_element_type=F32)


def matmul_multi(x, ws, tm=512):
    n, k = x.shape
    tm = _row_tile(n, tm)
    n_out = len(ws)
    return pl.pallas_call(
        functools.partial(_mm_kernel, n_out=n_out),
        grid=(n // tm,),
        in_specs=[pl.BlockSpec((tm, k), lambda i: (i, 0))]
        + [pl.BlockSpec(w.shape, lambda i: (0, 0)) for w in ws],
        out_specs=[pl.BlockSpec((tm, w.shape[1]), lambda i: (i, 0)) for w in ws],
        out_shape=[jax.ShapeDtypeStruct((n, w.shape[1]), F32) for w in ws],
        compiler_params=_params("parallel"),
        name="matmul_multi",
    )(x, *ws)


def _ffn_kernel(x_ref, w1_ref, w3_ref, w2_ref, g_ref, b_ref, o_ref, acc_ref, *, n_ff):
    j = pl.program_id(1)
    xb = x_ref[...].astype(BF16)
    h1 = jnp.dot(xb, w1_ref[...], preferred_element_type=F32)
    h3 = jnp.dot(xb, w3_ref[...], preferred_element_type=F32)
    h = (h1 * _sigmoid(h1)) * h3
    part = jnp.dot(h.astype(BF16), w2_ref[...], preferred_element_type=F32)

    @pl.when(j == 0)
    def _():
        acc_ref[...] = part

    @pl.when(j > 0)
    def _():
        acc_ref[...] += part

    @pl.when(j == n_ff - 1)
    def _():
        z = ALPHA * x_ref[...] + 0.5 * acc_ref[...]
        o_ref[...] = _layer_norm(z, g_ref[...], b_ref[...])


def ffn_ln(x, w1, w3, w2, g, b, tm=512, tf=1408):
    n, d = x.shape
    tm = _row_tile(n, tm)
    n_ff = D_FF // tf
    return pl.pallas_call(
        functools.partial(_ffn_kernel, n_ff=n_ff),
        grid=(n // tm, n_ff),
        in_specs=[
            pl.BlockSpec((tm, d), lambda i, j: (i, 0)),
            pl.BlockSpec((d, tf), lambda i, j: (0, j)),
            pl.BlockSpec((d, tf), lambda i, j: (0, j)),
            pl.BlockSpec((tf, d), lambda i, j: (j, 0)),
            pl.BlockSpec((1, d), lambda i, j: (0, 0)),
            pl.BlockSpec((1, d), lambda i, j: (0, 0)),
        ],
        out_specs=pl.BlockSpec((tm, d), lambda i, j: (i, 0)),
        out_shape=jax.ShapeDtypeStruct((n, d), F32),
        scratch_shapes=[pltpu.VMEM((tm, d), F32)],
        compiler_params=_params("parallel", "arbitrary"),
        name="ffn_ln",
    )(x, w1, w3, w2, g, b)


def _proj_ln_kernel(x_ref, *refs, n_in):
    a_refs = refs[:n_in]
    w_refs = refs[n_in:2 * n_in]
    g_ref, b_ref, o_ref = refs[2 * n_in:]
    f = None
    for a_ref, w_ref in zip(a_refs, w_refs):
        t = jnp.dot(a_ref[...].astype(BF16), w_ref[...], preferred_element_type=F32)
        f = t if f is None else f + t
    o_ref[...] = _layer_norm(ALPHA * x_ref[...] + f, g_ref[...], b_ref[...])


def proj_ln(x, acts, ws, g, b, tm=512):
    n, d = x.shape
    tm = _row_tile(n, tm)
    n_in = len(acts)
    return pl.pallas_call(
        functools.partial(_proj_ln_kernel, n_in=n_in),
        grid=(n // tm,),
        in_specs=[pl.BlockSpec((tm, d), lambda i: (i, 0))]
        + [pl.BlockSpec((tm, a.shape[1]), lambda i: (i, 0)) for a in acts]
        + [pl.BlockSpec(w.shape, lambda i: (0, 0)) for w in ws]
        + [pl.BlockSpec((1, d), lambda i: (0, 0))] * 2,
        out_specs=pl.BlockSpec((tm, d), lambda i: (i, 0)),
        out_shape=jax.ShapeDtypeStruct((n, d), F32),
        compiler_params=_params("parallel"),
        name="proj_ln",
    )(x, *acts, *ws, g, b)


def _rwkv_prep_kernel(p_ref, prev_ref, mu_ref, w0_ref, wup_ref, a0_ref, aup_ref, gup_ref,
                      kk_ref, ka_ref, rk_ref, seg_ref,
                      r_out, w_out, k_out, v_out, kk_out, kka_out, g_out, bonus_out):
    p = p_ref[...]
    xm = p + (prev_ref[...] - p) * mu_ref[...]
    r = xm[:, 0:RW_WIDTH]
    k = xm[:, RW_WIDTH:2 * RW_WIDTH]
    v = xm[:, 2 * RW_WIDTH:3 * RW_WIDTH]
    wa = xm[:, LORA_WA_START:LORA_G_START]
    gl = xm[:, LORA_G_START:RW_COLS]

    def seg_sum(x):
        return jnp.dot(_split3(x), seg_ref[...], preferred_element_type=F32)

    lw = jnp.dot(jnp.tanh(wa).astype(BF16), wup_ref[...], preferred_element_type=F32)
    la = jnp.dot(wa.astype(BF16), aup_ref[...], preferred_element_type=F32)
    z = -(w0_ref[...] + lw)
    softplus = jnp.maximum(z, 0.0) + jnp.log(1.0 + jnp.exp(-jnp.abs(z)))
    w_log = -softplus - 0.5
    decay = jnp.exp(-jnp.exp(w_log))
    a = _sigmoid(a0_ref[...] + la)
    g = jnp.dot(_sigmoid(gl).astype(BF16), gup_ref[...], preferred_element_type=F32)
    kk = k * kk_ref[...]
    nrm = jnp.sqrt(seg_sum(kk * kk))
    kk = kk / jnp.maximum(nrm, NORM_EPS)
    k_mod = k * (1.0 + (a - 1.0) * ka_ref[...])
    for o_ref, val in ((r_out, r), (w_out, decay), (k_out, k_mod), (v_out, v), (kk_out, kk),
                       (kka_out, kk * a)):
        for pr in range(RW_PAIRS):
            o_ref[pr] = val[:, pr * LANES:(pr + 1) * LANES]
    g_out[...] = g
    bonus_out[...] = seg_sum(r * k_mod * rk_ref[...]) * v


def rwkv_prep(p, prev, mu, w0, wup, a0, aup, gup, k_k, k_a, r_k, seg, tm=256):
    n = p.shape[0]
    tm = _row_tile(n, tm)
    row = lambda c: pl.BlockSpec((tm, c), lambda i: (i, 0))
    full = lambda a: pl.BlockSpec(a.shape, lambda i: (0, 0))
    consts = (mu, w0, wup, a0, aup, gup, k_k, k_a, r_k, seg)
    return pl.pallas_call(
        _rwkv_prep_kernel,
        grid=(n // tm,),
        in_specs=[row(RW_COLS), row(RW_COLS)] + [full(c) for c in consts],
        out_specs=[pl.BlockSpec((RW_PAIRS, tm, LANES), lambda i: (0, i, 0))] * 6 + [row(RW_WIDTH)] * 2,
        out_shape=[jax.ShapeDtypeStruct((RW_PAIRS, n, LANES), F32)] * 6
        + [jax.ShapeDtypeStruct((n, RW_WIDTH), F32)] * 2,
        compiler_params=_params("parallel"),
        name="rwkv_prep",
    )(p, prev, *consts)


def _wkv_kernel(r_ref, w_ref, k_ref, v_ref, kk_ref, kka_ref, s0_ref, y_ref, s_ref, *, bb, tc):
    @pl.when(pl.program_id(1) == 0)
    def _():
        s_ref[...] = s0_ref[...]

    sub = lax.broadcasted_iota(jnp.int32, (HEAD_DIM, LANES), 0)
    lane = lax.broadcasted_iota(jnp.int32, (HEAD_DIM, LANES), 1)
    diag = (lane % HEAD_DIM) == sub
    rj = lax.broadcasted_iota(jnp.int32, (2 * LANES, LANES), 0)
    cl = lax.broadcasted_iota(jnp.int32, (2 * LANES, LANES), 1)
    ones2 = jnp.where(((rj % LANES) // HEAD_DIM) == (cl // HEAD_DIM), 1.0, 0.0).astype(BF16)

    def head_sum(x):
        return jnp.dot(_split2(x), ones2, preferred_element_type=F32)

    def step(t, carry):
        for b in range(bb):
            for p in range(RW_PAIRS):
                row = lambda ref: ref[p, b, pl.ds(t, 1), :]
                tiles = lambda x: x.reshape(HEAD_DIM // 8, 8, LANES)
                flat = lambda x: x.reshape(HEAD_DIM, LANES)
                s = tiles(s_ref[b, p])
                sa = tiles(head_sum(flat(s * row(kk_ref))))
                vb = tiles(head_sum(flat(jnp.where(tiles(diag), row(v_ref), 0.0))))
                s = s * row(w_ref) - sa * row(kka_ref) + vb * row(k_ref)
                yb = head_sum(flat(s * row(r_ref)))
                s_ref[b, p] = flat(s)
                y_ref[p, b, pl.ds(t, 1), :] = jnp.sum(jnp.where(diag, yb, 0.0), axis=0, keepdims=True)
        return carry

    lax.fori_loop(0, tc, step, 0)


def wkv_scan(r, w, k, v, kk, kka, s0, bb, tc):
    _, nb, t, _ = r.shape
    seq = pl.BlockSpec((RW_PAIRS, bb, tc, LANES), lambda i, j: (0, i, j, 0))
    st = pl.BlockSpec((bb, RW_PAIRS, HEAD_DIM, LANES), lambda i, j: (i, 0, 0, 0))
    return pl.pallas_call(
        functools.partial(_wkv_kernel, bb=bb, tc=tc),
        grid=(nb // bb, t // tc),
        in_specs=[seq] * 6 + [st],
        out_specs=[seq, st],
        out_shape=[jax.ShapeDtypeStruct((RW_PAIRS, nb, t, LANES), F32),
                   jax.ShapeDtypeStruct(s0.shape, F32)],
        compiler_params=_params("parallel", "arbitrary"),
        name="wkv_scan",
    )(r, w, k, v, kk, kka, s0)


def _rwkv_post_kernel(y_ref, bonus_ref, g_ref, gng_ref, gnb_ref, seg_ref, o_ref):
    def seg_mean(x):
        return jnp.dot(_split3(x), seg_ref[...], preferred_element_type=F32) * (1.0 / HEAD_DIM)

    y = jnp.concatenate([y_ref[pr] for pr in range(RW_PAIRS)], axis=1)
    d = y - seg_mean(y)
    var = seg_mean(d * d)
    yn = d * lax.rsqrt(var + GN_EPS) * gng_ref[...] + gnb_ref[...]
    o_ref[...] = (yn + bonus_ref[...]) * g_ref[...]


def rwkv_post(y, bonus, g, gn_g, gn_b, seg, tm=512):
    n = y.shape[1]
    tm = _row_tile(n, tm)
    row = pl.BlockSpec((tm, RW_WIDTH), lambda i: (i, 0))
    full = lambda a: pl.BlockSpec(a.shape, lambda i: (0, 0))
    return pl.pallas_call(
        _rwkv_post_kernel,
        grid=(n // tm,),
        in_specs=[pl.BlockSpec((RW_PAIRS, tm, LANES), lambda i: (0, i, 0)), row, row,
                  full(gn_g), full(gn_b), full(seg)],
        out_specs=row,
        out_shape=jax.ShapeDtypeStruct((n, RW_WIDTH), F32),
        compiler_params=_params("parallel"),
        name="rwkv_post",
    )(y, bonus, g, gn_g, gn_b, seg)


def _alibi_slope(h):
    return 2.0 ** (-8.0 * (h + 1) / SW_HEADS)


def _swa_prompt_kernel(sink_ref, q_ref, kp_ref, kc_ref, vp_ref, vc_ref, o_ref):
    n = pl.program_id(1)
    q = q_ref[0]
    kw = jnp.concatenate([kp_ref[0], kc_ref[0]], axis=0).astype(BF16)
    vw = jnp.concatenate([vp_ref[0], vc_ref[0]], axis=0).astype(BF16)
    i = lax.broadcasted_iota(jnp.int32, (WINDOW, 2 * WINDOW), 0)
    j = lax.broadcasted_iota(jnp.int32, (WINDOW, 2 * WINDOW), 1)
    dist = WINDOW + i - j
    valid = (dist >= 0) & (dist < WINDOW) & ((n - 1) * WINDOW + j >= 0)
    distf = dist.astype(F32)
    for c in range(SW_KV_HEADS):
        kc = kw[:, c * HEAD_DIM:(c + 1) * HEAD_DIM]
        vc = vw[:, c * HEAD_DIM:(c + 1) * HEAD_DIM]
        for gi in range(SW_GROUP):
            h = c * SW_GROUP + gi
            qh = q[:, h * HEAD_DIM:(h + 1) * HEAD_DIM].astype(BF16)
            s = lax.dot_general(qh, kc, (((1,), (1,)), ((), ())),
                                preferred_element_type=F32) * HEAD_DIM ** -0.5
            logits = jnp.where(valid, s - _alibi_slope(h) * distf, -jnp.inf)
            sink = sink_ref[h]
            m = jnp.maximum(jnp.max(logits, axis=-1, keepdims=True), sink)
            e = jnp.exp(logits - m)
            prob = e / (jnp.sum(e, axis=-1, keepdims=True) + jnp.exp(sink - m))
            o_ref[0, :, h * HEAD_DIM:(h + 1) * HEAD_DIM] = jnp.dot(
                prob.astype(BF16), vc, preferred_element_type=F32)


def swa_prompt(q, k, v, sinks):
    nb, t, _ = q.shape
    cur = lambda w: pl.BlockSpec((1, WINDOW, w), lambda b, n: (b, n, 0))
    prv = lambda w: pl.BlockSpec((1, WINDOW, w), lambda b, n: (b, jnp.maximum(n - 1, 0), 0))
    return pl.pallas_call(
        _swa_prompt_kernel,
        grid=(nb, t // WINDOW),
        in_specs=[pl.BlockSpec(memory_space=pltpu.SMEM), cur(SW_WIDTH),
                  prv(KV_WIDTH), cur(KV_WIDTH), prv(KV_WIDTH), cur(KV_WIDTH)],
        out_specs=cur(SW_WIDTH),
        out_shape=jax.ShapeDtypeStruct((nb, t, SW_WIDTH), F32),
        compiler_params=_params("parallel", "parallel"),
        name="swa_prompt",
    )(sinks, q, k, k, v, v)


def _swa_sample_kernel(sink_ref, slope_ref, q_ref, kn_ref, vn_ref, ck_ref, cv_ref,
                       o_ref, nk_ref, nv_ref):
    last = lax.broadcasted_iota(jnp.int32, (WINDOW, KV_WIDTH), 0) == WINDOW - 1
    kw = jnp.where(last, kn_ref[0], pltpu.roll(ck_ref[0], WINDOW - 1, 0))
    vw = jnp.where(last, vn_ref[0], pltpu.roll(cv_ref[0], WINDOW - 1, 0))
    nk_ref[0] = kw
    nv_ref[0] = vw
    kwb = kw.astype(BF16)
    vwb = vw.astype(BF16)
    j = lax.broadcasted_iota(jnp.int32, (SW_GROUP, WINDOW), 1)
    distf = (WINDOW - 1 - j).astype(F32)
    q = q_ref[0]
    for c in range(SW_KV_HEADS):
        rows = slice(c * SW_GROUP, (c + 1) * SW_GROUP)
        qc = q[rows].astype(BF16)
        s = lax.dot_general(qc, kwb[:, c * HEAD_DIM:(c + 1) * HEAD_DIM], (((1,), (1,)), ((), ())),
                            preferred_element_type=F32) * HEAD_DIM ** -0.5
        logits = s - slope_ref[rows] * distf
        sink = sink_ref[rows]
        m = jnp.maximum(jnp.max(logits, axis=-1, keepdims=True), sink)
        e = jnp.exp(logits - m)
        prob = e / (jnp.sum(e, axis=-1, keepdims=True) + jnp.exp(sink - m))
        o_ref[0, rows, :] = jnp.dot(prob.astype(BF16), vwb[:, c * HEAD_DIM:(c + 1) * HEAD_DIM],
                                    preferred_element_type=F32)


def swa_sample(q, kn, vn, ck, cv, sinks, slopes):
    nb = q.shape[0]
    per_b = lambda s: pl.BlockSpec((1,) + s, lambda b: (b, 0, 0))
    full = lambda a: pl.BlockSpec(a.shape, lambda b: (0, 0))
    return pl.pallas_call(
        _swa_sample_kernel,
        grid=(nb,),
        in_specs=[full(sinks), full(slopes), per_b((SW_HEADS, HEAD_DIM)), per_b((1, KV_WIDTH)),
                  per_b((1, KV_WIDTH)), per_b((WINDOW, KV_WIDTH)), per_b((WINDOW, KV_WIDTH))],
        out_specs=[per_b((SW_HEADS, HEAD_DIM)), per_b((WINDOW, KV_WIDTH)), per_b((WINDOW, KV_WIDTH))],
        out_shape=[jax.ShapeDtypeStruct((nb, SW_HEADS, HEAD_DIM), F32),
                   jax.ShapeDtypeStruct((nb, WINDOW, KV_WIDTH), F32),
                   jax.ShapeDtypeStruct((nb, WINDOW, KV_WIDTH), F32)],
        compiler_params=_params("parallel"),
        name="swa_sample",
    )(sinks, slopes, q, kn, vn, ck, cv)


def _mem_attn_kernel(q_ref, mk_ref, mv_ref, o_ref, *, rows):
    q = q_ref[0]
    if rows != q.shape[0]:
        q = jnp.broadcast_to(q, (rows, q.shape[1]))
    for h in range(MEM_HEADS):
        cols = slice(h * MEM_HEAD_DIM, (h + 1) * MEM_HEAD_DIM)
        s = lax.dot_general(q[:, cols].astype(BF16), mk_ref[0, :, cols].astype(BF16),
                            (((1,), (1,)), ((), ())),
                            preferred_element_type=F32) * MEM_HEAD_DIM ** -0.5
        m = jnp.max(s, axis=-1, keepdims=True)
        e = jnp.exp(s - m)
        prob = e / jnp.sum(e, axis=-1, keepdims=True)
        o = jnp.dot(prob.astype(BF16), mv_ref[0, :, cols].astype(BF16), preferred_element_type=F32)
        o_ref[0, :, cols] = o[:o_ref.shape[1]]


def mem_attn(q, mk, mv, tm=512):
    ng, t, d = q.shape
    tm = _row_tile(t, tm)
    rows = max(tm, 8)
    return pl.pallas_call(
        functools.partial(_mem_attn_kernel, rows=rows),
        grid=(ng, t // tm),
        in_specs=[pl.BlockSpec((1, tm, d), lambda g, i: (g, i, 0)),
                  pl.BlockSpec((1, MEM_TOKENS, d), lambda g, i: (g, 0, 0)),
                  pl.BlockSpec((1, MEM_TOKENS, d), lambda g, i: (g, 0, 0))],
        out_specs=pl.BlockSpec((1, tm, d), lambda g, i: (g, i, 0)),
        out_shape=jax.ShapeDtypeStruct((ng, t, d), F32),
        compiler_params=_params("parallel", "parallel"),
        name="mem_attn",
    )(q, mk, mv)


def _pair_state(s):
    nb = s.shape[0]
    s = s.reshape(nb, RW_PAIRS, 2, HEAD_DIM, HEAD_DIM)
    return jnp.swapaxes(s, 2, 3).reshape(nb, RW_PAIRS, HEAD_DIM, LANES)


def _unpair_state(s):
    nb = s.shape[0]
    s = s.reshape(nb, RW_PAIRS, HEAD_DIM, 2, HEAD_DIM)
    return jnp.swapaxes(s, 2, 3).reshape(nb, RW_HEADS, HEAD_DIM, HEAD_DIM)


def kernel(x_prompt, x_sample, mem_prompt, state_wkv, state_shift, cache_win_k, cache_win_v,
           cache_mem_k, cache_mem_v, ln_g, ln_b, ffn_w1, ffn_w3, ffn_w2, w_in, rw_mu, rw_w0,
           rw_w_up, rw_a0, rw_a_up, rw_g_up, rw_k_k, rw_k_a, rw_r_k, rw_gn_g, rw_gn_b, sw_sinks,
           w_out, mem_wq, mem_wk, mem_wv, mem_wo):
    depth = ln_g.shape[0]
    bp, tp, d = x_prompt.shape
    bs, ts, _ = x_sample.shape
    assert ts == 1 and cache_win_k.shape[2] == WINDOW and tp % WINDOW == 0

    w1b, w3b, w2b = (w.astype(BF16) for w in (ffn_w1, ffn_w3, ffn_w2))
    w_in_b = w_in.astype(BF16)
    w_out_b = w_out.astype(BF16)
    wqb, wkb, wvb, wob = (w.astype(BF16) for w in (mem_wq, mem_wk, mem_wv, mem_wo))
    zpad = jnp.zeros((depth, D_W_LORA, RW_WIDTH), BF16)
    wup_b = jnp.concatenate([rw_w_up.astype(BF16), zpad], axis=1)
    aup_b = jnp.concatenate([zpad, rw_a_up.astype(BF16)], axis=1)
    gup_b = rw_g_up.astype(BF16)
    hid = jnp.arange(3 * RW_WIDTH) % RW_WIDTH // HEAD_DIM
    seg3 = (hid[:, None] == (jnp.arange(RW_WIDTH) // HEAD_DIM)[None, :]).astype(BF16)
    slopes = jnp.asarray([[_alibi_slope(h)] for h in range(SW_HEADS)], F32)
    q0, k0, v0 = RW_COLS, RW_COLS + SW_WIDTH, RW_COLS + SW_WIDTH + KV_WIDTH
    row = lambda a: a.reshape(1, -1)

    def layer(l, x, nb, t, prev_fn, s0, swa_fn, mem_fn):
        x = ffn_ln(x, w1b[l, 0], w3b[l, 0], w2b[l, 0], row(ln_g[l, 0]), row(ln_b[l, 0]))
        p_rw, q, k, v = matmul_multi(
            x, [w_in_b[l, :, :q0], w_in_b[l, :, q0:k0], w_in_b[l, :, k0:v0], w_in_b[l, :, v0:]])
        r, dec, k_mod, vv, kk, kka, g, bonus = rwkv_prep(
            p_rw, prev_fn(p_rw), row(rw_mu[l]), row(rw_w0[l]), wup_b[l], row(rw_a0[l]), aup_b[l],
            gup_b[l], row(rw_k_k[l]), row(rw_k_a[l]), row(rw_r_k[l]), seg3)
        seq = lambda a: a.reshape(RW_PAIRS, nb, t, LANES)
        y, s_fin = wkv_scan(seq(r), seq(dec), seq(k_mod), seq(vv), seq(kk), seq(kka), s0,
                            bb=2 if t > 1 else 8, tc=min(t, 256))
        y_rw = rwkv_post(y.reshape(RW_PAIRS, nb * t, LANES), bonus, g, row(rw_gn_g[l]), row(rw_gn_b[l]),
                         seg3)
        y_sw, win_k, win_v = swa_fn(q, k, v)
        shift = x.reshape(nb, t, d)[:, -1]
        x = proj_ln(x, [y_rw, y_sw], [w_out_b[l, :RW_WIDTH], w_out_b[l, RW_WIDTH:]],
                    row(ln_g[l, 1]), row(ln_b[l, 1]))
        (qm,) = matmul_multi(x, [wqb[l]])
        o = mem_fn(qm)
        x = proj_ln(x, [o], [wob[l]], row(ln_g[l, 2]), row(ln_b[l, 2]))
        x = ffn_ln(x, w1b[l, 1], w3b[l, 1], w2b[l, 1], row(ln_g[l, 3]), row(ln_b[l, 3]))
        return x, _unpair_state(s_fin), shift, win_k, win_v

    xp = x_prompt.reshape(bp * tp, d)
    p_wkv, p_shift, p_wk, p_wv, p_mk, p_mv = [], [], [], [], [], []
    for l in range(depth):
        def prev_prompt(p_rw):
            p3 = p_rw.reshape(bp, tp, RW_COLS)
            return jnp.concatenate([jnp.zeros_like(p3[:, :1]), p3[:, :-1]], axis=1).reshape(bp * tp, RW_COLS)

        def swa_p(q, k, v, l=l):
            k3 = k.reshape(bp, tp, KV_WIDTH)
            v3 = v.reshape(bp, tp, KV_WIDTH)
            y = swa_prompt(q.reshape(bp, tp, SW_WIDTH), k3, v3, sw_sinks[l])
            tail = lambda a: a[:, -WINDOW:].reshape(bp, WINDOW, SW_KV_HEADS, HEAD_DIM)
            return y.reshape(bp * tp, SW_WIDTH), tail(k3), tail(v3)

        mk, mv = matmul_multi(mem_prompt.reshape(bp * MEM_TOKENS, d), [wkb[l], wvb[l]])
        mk = mk.reshape(bp, MEM_TOKENS, d)
        mv = mv.reshape(bp, MEM_TOKENS, d)

        def mem_p(qm, mk=mk, mv=mv):
            return mem_attn(qm.reshape(bp, tp, d), mk, mv).reshape(bp * tp, d)

        s0 = jnp.zeros((bp, RW_PAIRS, HEAD_DIM, LANES), F32)
        xp, s_fin, shift, wk_, wv_ = layer(l, xp, bp, tp, prev_prompt, s0, swa_p, mem_p)
        p_wkv.append(s_fin)
        p_shift.append(shift)
        p_wk.append(wk_)
        p_wv.append(wv_)
        p_mk.append(mk.reshape(bp, MEM_TOKENS, MEM_HEADS, MEM_HEAD_DIM))
        p_mv.append(mv.reshape(bp, MEM_TOKENS, MEM_HEADS, MEM_HEAD_DIM))

    xs = x_sample.reshape(bs, d)
    s_wkv, s_shift, s_wk, s_wv = [], [], [], []
    for l in range(depth):
        def prev_sample(p_rw, l=l):
            (prev,) = matmul_multi(state_shift[l], [w_in_b[l, :, :q0]])
            return prev

        def swa_s(q, k, v, l=l):
            o, nk, nv = swa_sample(
                q.reshape(bs, SW_HEADS, HEAD_DIM), k.reshape(bs, 1, KV_WIDTH), v.reshape(bs, 1, KV_WIDTH),
                cache_win_k[l].reshape(bs, WINDOW, KV_WIDTH), cache_win_v[l].reshape(bs, WINDOW, KV_WIDTH),
                sw_sinks[l].reshape(SW_HEADS, 1), slopes)
            unflat = lambda a: a.reshape(bs, WINDOW, SW_KV_HEADS, HEAD_DIM)
            return o.reshape(bs, SW_WIDTH), unflat(nk), unflat(nv)

        def mem_s(qm, l=l):
            return mem_attn(qm.reshape(bs, 1, d), cache_mem_k[l].reshape(bs, MEM_TOKENS, d),
                            cache_mem_v[l].reshape(bs, MEM_TOKENS, d)).reshape(bs, d)

        xs, s_fin, shift, wk_, wv_ = layer(l, xs, bs, 1, prev_sample, _pair_state(state_wkv[l]),
                                           swa_s, mem_s)
        s_wkv.append(s_fin)
        s_shift.append(shift)
        s_wk.append(wk_)
        s_wv.append(wv_)

    return (xp.reshape(bp, tp, d), xs.reshape(bs, 1, d),
            jnp.stack(p_wkv), jnp.stack(p_shift), jnp.stack(p_wk), jnp.stack(p_wv),
            jnp.stack(p_mk), jnp.stack(p_mv),
            jnp.stack(s_wkv), jnp.stack(s_shift), jnp.stack(s_wk), jnp.stack(s_wv))
```

```python
import functools

import jax
import jax.numpy as jnp
from jax import lax
from jax.experimental import pallas as pl
from jax.experimental.pallas import tpu as pltpu

F32 = jnp.float32
BF16 = jnp.bfloat16

D_MODEL = 1024
HEAD_DIM = 64
RW_WIDTH = 512
RW_HEADS = 8
RW_PAIRS = RW_HEADS // 2
SW_WIDTH = 512
SW_HEADS = 8
SW_KV_HEADS = 2
SW_GROUP = SW_HEADS // SW_KV_HEADS
KV_WIDTH = SW_KV_HEADS * HEAD_DIM
WINDOW = 128
D_W_LORA = 64
D_A_LORA = 64
D_G_LORA = 128
RW_COLS = 3 * RW_WIDTH + D_W_LORA + D_A_LORA + D_G_LORA
LORA_WA_START = 3 * RW_WIDTH
LORA_G_START = LORA_WA_START + D_W_LORA + D_A_LORA
MEM_TOKENS = 256
MEM_HEADS = 4
MEM_HEAD_DIM = D_MODEL // MEM_HEADS
D_FF = 2816
DEPTH = 4
ALPHA = (2.0 * DEPTH) ** 0.25
LN_EPS = 1e-5
GN_EPS = 64e-5
NORM_EPS = 1e-12

CHUNK = 64
LANES = 128
VMEM_LIMIT = 56 * 1024 * 1024


def _params(*semantics):
    return pltpu.CompilerParams(dimension_semantics=semantics, vmem_limit_bytes=VMEM_LIMIT)


def _row_tile(n, want):
    return want if n % want == 0 else n


def _layer_norm(z, g, b):
    mu = jnp.mean(z, axis=-1, keepdims=True)
    d = z - mu
    var = jnp.mean(d * d, axis=-1, keepdims=True)
    return d * lax.rsqrt(var + LN_EPS) * g + b


def _sigmoid(x):
    return 1.0 / (1.0 + jnp.exp(-x))


def _split3(x):
    hi = x.astype(BF16)
    r1 = x - hi.astype(F32)
    mid = r1.astype(BF16)
    lo = (r1 - mid.astype(F32)).astype(BF16)
    return jnp.concatenate([hi, mid, lo], axis=1)


def _split2(x):
    hi = x.astype(BF16)
    lo = (x - hi.astype(F32)).astype(BF16)
    return jnp.concatenate([hi, lo], axis=1)


def _mm_kernel(x_ref, *refs, n_out):
    xb = x_ref[...].astype(BF16)
    for w_ref, o_ref in zip(refs[:n_out], refs[n_out:]):
        o_ref[...] = jnp.dot(xb, w_ref[...], preferred_element_type=F32)


def matmul_multi(x, ws, tm=512):
    n, k = x.shape
    tm = _row_tile(n, tm)
    n_out = len(ws)
    return pl.pallas_call(
        functools.partial(_mm_kernel, n_out=n_out),
        grid=(n // tm,),
        in_specs=[pl.BlockSpec((tm, k), lambda i: (i, 0))]
        + [pl.BlockSpec(w.shape, lambda i: (0, 0)) for w in ws],
        out_specs=[pl.BlockSpec((tm, w.shape[1]), lambda i: (i, 0)) for w in ws],
        out_shape=[jax.ShapeDtypeStruct((n, w.shape[1]), F32) for w in ws],
        compiler_params=_params("parallel"),
        name="matmul_multi",
    )(x, *ws)


def _ffn_kernel(x_ref, w1_ref, w3_ref, w2_ref, g_ref, b_ref, o_ref, acc_ref, *, n_ff):
    j = pl.program_id(1)
    xb = x_ref[...].astype(BF16)
    h1 = jnp.dot(xb, w1_ref[...], preferred_element_type=F32)
    h3 = jnp.dot(xb, w3_ref[...], preferred_element_type=F32)
    h = (h1 * _sigmoid(h1)) * h3
    part = jnp.dot(h.astype(BF16), w2_ref[...], preferred_element_type=F32)

    @pl.when(j == 0)
    def _():
        acc_ref[...] = part

    @pl.when(j > 0)
    def _():
        acc_ref[...] += part

    @pl.when(j == n_ff - 1)
    def _():
        z = ALPHA * x_ref[...] + 0.5 * acc_ref[...]
        o_ref[...] = _layer_norm(z, g_ref[...], b_ref[...])


def ffn_ln(x, w1, w3, w2, g, b, tm=512, tf=1408):
    n, d = x.shape
    tm = _row_tile(n, tm)
    n_ff = D_FF // tf
    return pl.pallas_call(
        functools.partial(_ffn_kernel, n_ff=n_ff),
        grid=(n // tm, n_ff),
        in_specs=[
            pl.BlockSpec((tm, d), lambda i, j: (i, 0)),
            pl.BlockSpec((d, tf), lambda i, j: (0, j)),
            pl.BlockSpec((d, tf), lambda i, j: (0, j)),
            pl.BlockSpec((tf, d), lambda i, j: (j, 0)),
            pl.BlockSpec((1, d), lambda i, j: (0, 0)),
            pl.BlockSpec((1, d), lambda i, j: (0, 0)),
        ],
        out_specs=pl.BlockSpec((tm, d), lambda i, j: (i, 0)),
        out_shape=jax.ShapeDtypeStruct((n, d), F32),
        scratch_shapes=[pltpu.VMEM((tm, d), F32)],
        compiler_params=_params("parallel", "arbitrary"),
        name="ffn_ln",
    )(x, w1, w3, w2, g, b)


def _proj_ln_kernel(x_ref, *refs, n_in):
    a_refs = refs[:n_in]
    w_refs = refs[n_in:2 * n_in]
    g_ref, b_ref, o_ref = refs[2 * n_in:]
    f = None
    for a_ref, w_ref in zip(a_refs, w_refs):
        t = jnp.dot(a_ref[...].astype(BF16), w_ref[...], preferred_element_type=F32)
        f = t if f is None else f + t
    o_ref[...] = _layer_norm(ALPHA * x_ref[...] + f, g_ref[...], b_ref[...])


def proj_ln(x, acts, ws, g, b, tm=512):
    n, d = x.shape
    tm = _row_tile(n, tm)
    n_in = len(acts)
    return pl.pallas_call(
        functools.partial(_proj_ln_kernel, n_in=n_in),
        grid=(n // tm,),
        in_specs=[pl.BlockSpec((tm, d), lambda i: (i, 0))]
        + [pl.BlockSpec((tm, a.shape[1]), lambda i: (i, 0)) for a in acts]
        + [pl.BlockSpec(w.shape, lambda i: (0, 0)) for w in ws]
        + [pl.BlockSpec((1, d), lambda i: (0, 0))] * 2,
        out_specs=pl.BlockSpec((tm, d), lambda i: (i, 0)),
        out_shape=jax.ShapeDtypeStruct((n, d), F32),
        compiler_params=_params("parallel"),
        name="proj_ln",
    )(x, *acts, *ws, g, b)


def _rwkv_prep_kernel(p_ref, prev_ref, mu_ref, w0_ref, wup_ref, a0_ref, aup_ref, gup_ref,
                      kk_ref, ka_ref, rk_ref, seg_ref, *refs, chunked, tiles_per_seq):
    if chunked:
        tril_ref, ones_ref = refs[:2]
        refs = refs[2:]
    g_out, bonus_out = refs[-2:]
    p = p_ref[...]
    if chunked:
        first = (pl.program_id(0) % tiles_per_seq) == 0
        above = jnp.where(first, 0.0, prev_ref[7:8, :])
        top = lax.broadcasted_iota(jnp.int32, p.shape, 0) == 0
        prev = jnp.where(top, above, pltpu.roll(p, 1, 0))
    else:
        prev = prev_ref[...]
    xm = p + (prev - p) * mu_ref[...]
    r = xm[:, 0:RW_WIDTH]
    k = xm[:, RW_WIDTH:2 * RW_WIDTH]
    v = xm[:, 2 * RW_WIDTH:3 * RW_WIDTH]
    wa = xm[:, LORA_WA_START:LORA_G_START]
    gl = xm[:, LORA_G_START:RW_COLS]

    def seg_sum(x):
        return jnp.dot(_split3(x), seg_ref[...], preferred_element_type=F32)

    lw = jnp.dot(jnp.tanh(wa).astype(BF16), wup_ref[...], preferred_element_type=F32)
    la = jnp.dot(wa.astype(BF16), aup_ref[...], preferred_element_type=F32)
    z = -(w0_ref[...] + lw)
    softplus = jnp.maximum(z, 0.0) + jnp.log(1.0 + jnp.exp(-jnp.abs(z)))
    w_log = -softplus - 0.5
    log_decay = -jnp.exp(w_log)
    a = _sigmoid(a0_ref[...] + la)
    g = jnp.dot(_sigmoid(gl).astype(BF16), gup_ref[...], preferred_element_type=F32)
    kk = k * kk_ref[...]
    nrm = jnp.sqrt(seg_sum(kk * kk))
    kk = kk / jnp.maximum(nrm, NORM_EPS)
    k_mod = k * (1.0 + (a - 1.0) * ka_ref[...])
    kka = kk * a
    if chunked:
        parts = _split3(log_decay)

        def time_sum(m_ref):
            s3 = jnp.dot(m_ref[...], parts, preferred_element_type=F32)
            return s3[:, :RW_WIDTH] + s3[:, RW_WIDTH:2 * RW_WIDTH] + s3[:, 2 * RW_WIDTH:]

        cum = time_sum(tril_ref)
        tot = time_sum(ones_ref)
        grow = jnp.exp(-cum)
        rest = jnp.exp(tot - cum)
        outs = (kk * jnp.exp(cum - log_decay), r * jnp.exp(cum), -kka * grow, k_mod * grow,
                -kka * rest, k_mod * rest, v, jnp.exp(tot))
    else:
        outs = (r, jnp.exp(log_decay), k_mod, v, kk, kka)
    for o_ref, val in zip(refs, outs):
        for pr in range(RW_PAIRS):
            o_ref[pr] = val[:, pr * LANES:(pr + 1) * LANES].astype(o_ref.dtype)
    g_out[...] = g
    bonus_out[...] = seg_sum(r * k_mod * rk_ref[...]) * v


def rwkv_prep(p, prev, mu, w0, wup, a0, aup, gup, k_k, k_a, r_k, seg, seq_len, tm=256):
    n = p.shape[0]
    chunked = seq_len > 1
    tm = _row_tile(n, tm)
    row = lambda c: pl.BlockSpec((tm, c), lambda i: (i, 0))
    prev_spec = row(RW_COLS)
    if chunked:
        assert seq_len % tm == 0
        prev_spec = pl.BlockSpec((8, RW_COLS), lambda i: (jnp.maximum(i * (tm // 8) - 1, 0), 0))
    full = lambda a: pl.BlockSpec(a.shape, lambda i: (0, 0))
    consts = (mu, w0, wup, a0, aup, gup, k_k, k_a, r_k, seg)
    dtypes = [F32] * 6
    if chunked:
        assert tm % CHUNK == 0
        t_idx = jnp.arange(tm)
        same = (t_idx[:, None] // CHUNK) == (t_idx[None, :] // CHUNK)
        consts += ((same & (t_idx[None, :] <= t_idx[:, None])).astype(BF16), same.astype(BF16))
        dtypes = [BF16] * 7 + [F32]
    pair = pl.BlockSpec((RW_PAIRS, tm, LANES), lambda i: (0, i, 0))
    return pl.pallas_call(
        functools.partial(_rwkv_prep_kernel, chunked=chunked, tiles_per_seq=max(seq_len // tm, 1)),
        grid=(n // tm,),
        in_specs=[row(RW_COLS), prev_spec] + [full(c) for c in consts],
        out_specs=[pair] * len(dtypes) + [row(RW_WIDTH)] * 2,
        out_shape=[jax.ShapeDtypeStruct((RW_PAIRS, n, LANES), dt) for dt in dtypes]
        + [jax.ShapeDtypeStruct((n, RW_WIDTH), F32)] * 2,
        compiler_params=_params("parallel"),
        name="rwkv_prep",
    )(p, prev, *consts)


def _wkv_kernel(r_ref, w_ref, k_ref, v_ref, kk_ref, kka_ref, s0_ref, y_ref, s_ref, *, bb, tc):
    @pl.when(pl.program_id(1) == 0)
    def _():
        s_ref[...] = s0_ref[...]

    sub = lax.broadcasted_iota(jnp.int32, (HEAD_DIM, LANES), 0)
    lane = lax.broadcasted_iota(jnp.int32, (HEAD_DIM, LANES), 1)
    diag = (lane % HEAD_DIM) == sub
    rj = lax.broadcasted_iota(jnp.int32, (2 * LANES, LANES), 0)
    cl = lax.broadcasted_iota(jnp.int32, (2 * LANES, LANES), 1)
    ones2 = jnp.where(((rj % LANES) // HEAD_DIM) == (cl // HEAD_DIM), 1.0, 0.0).astype(BF16)

    def head_sum(x):
        return jnp.dot(_split2(x), ones2, preferred_element_type=F32)

    def step(t, carry):
        for b in range(bb):
            for p in range(RW_PAIRS):
                row = lambda ref: ref[p, b, pl.ds(t, 1), :]
                tiles = lambda x: x.reshape(HEAD_DIM // 8, 8, LANES)
                flat = lambda x: x.reshape(HEAD_DIM, LANES)
                s = tiles(s_ref[b, p])
                sa = tiles(head_sum(flat(s * row(kk_ref))))
                vb = tiles(head_sum(flat(jnp.where(tiles(diag), row(v_ref), 0.0))))
                s = s * row(w_ref) - sa * row(kka_ref) + vb * row(k_ref)
                yb = head_sum(flat(s * row(r_ref)))
                s_ref[b, p] = flat(s)
                y_ref[p, b, pl.ds(t, 1), :] = jnp.sum(jnp.where(diag, yb, 0.0), axis=0, keepdims=True)
        return carry

    lax.fori_loop(0, tc, step, 0)


def wkv_scan(r, w, k, v, kk, kka, s0, bb, tc):
    _, nb, t, _ = r.shape
    seq = pl.BlockSpec((RW_PAIRS, bb, tc, LANES), lambda i, j: (0, i, j, 0))
    st = pl.BlockSpec((bb, RW_PAIRS, HEAD_DIM, LANES), lambda i, j: (i, 0, 0, 0))
    return pl.pallas_call(
        functools.partial(_wkv_kernel, bb=bb, tc=tc),
        grid=(nb // bb, t // tc),
        in_specs=[seq] * 6 + [st],
        out_specs=[seq, st],
        out_shape=[jax.ShapeDtypeStruct((RW_PAIRS, nb, t, LANES), F32),
                   jax.ShapeDtypeStruct(s0.shape, F32)],
        compiler_params=_params("parallel", "arbitrary"),
        name="wkv_scan",
    )(r, w, k, v, kk, kka, s0)


def _dot_nt(a, b):
    return lax.dot_general(a, b, (((1,), (1,)), ((), ())), preferred_element_type=F32)


def _dot_tn(a, b):
    return lax.dot_general(a, b, (((0,), (0,)), ((), ())), preferred_element_type=F32)


def _dot(a, b):
    return jnp.dot(a, b, preferred_element_type=F32)


def _wkv_chunk_kernel(qt_ref, rt_ref, bh_ref, kh_ref, bb_ref, kb_ref, v_ref, gc_ref, s0_ref,
                      y_ref, s_ref, *, nb, tc):
    @pl.when(pl.program_id(1) == 0)
    def _():
        s_ref[...] = s0_ref[...]

    c = CHUNK
    assert c == HEAD_DIM and 2 * c == LANES
    row = lax.broadcasted_iota(jnp.int32, (c, LANES), 0)
    lane = lax.broadcasted_iota(jnp.int32, (c, LANES), 1)
    head_a = lane < HEAD_DIM
    head_a2 = lax.broadcasted_iota(jnp.int32, (2 * c, LANES), 1) < HEAD_DIM
    m_mask = (lane >= c) & (lane - c < row)
    incl = (lane % c) <= row
    r64 = lax.broadcasted_iota(jnp.int32, (c, c), 0)
    c64 = lax.broadcasted_iota(jnp.int32, (c, c), 1)
    strict = c64 < r64
    eye = jnp.where(c64 == r64, 1.0, 0.0)
    pairs = [(b, p) for b in range(nb) for p in range(RW_PAIRS)]
    bf = lambda x: x.astype(BF16)
    stack = lambda x, y: jnp.concatenate([x, y], axis=0)

    def chunk(ci, carry):
        t0 = pl.multiple_of(ci * c, c)
        ld = lambda ref, b, p: ref[p, b, pl.ds(t0, c), :]
        qr = [stack(ld(qt_ref, b, p), ld(rt_ref, b, p)) for b, p in pairs]
        bk = [stack(ld(bh_ref, b, p), ld(kh_ref, b, p)) for b, p in pairs]
        zero = jnp.zeros((), BF16)
        e = [_dot_nt(q, stack(jnp.where(head_a2, x, zero), jnp.where(head_a2, zero, x)))
             for q, x in zip(qr, bk)]

        pw = [jnp.where(strict, x[:c, h * LANES:h * LANES + c], 0.0) for x in e for h in range(2)]
        inv = [eye + n for n in pw]
        for _ in range(5):
            pb = [bf(x) for x in pw]
            pw = [_dot(x, x) for x in pb]
            inv = [t + _dot(bf(x), bf(t)) for x, t in zip(pw, inv)]

        s = [s_ref[b, p] for b, p in pairs]
        ff = [_dot_nt(q, bf(stack(jnp.where(head_a, x, 0.0), jnp.where(head_a, 0.0, x))))
              for q, x in zip(qr, s)]
        v = [ld(v_ref, b, p) for b, p in pairs]
        vv = [stack(x, x) for x in v]
        heads = lambda fn: jnp.where(head_a, fn(0), fn(1))
        g = [f[:c] + heads(lambda h: _dot(bf(jnp.where(m_mask, x[:c, h * LANES:(h + 1) * LANES], 0.0)), w))
             for f, x, w in zip(ff, e, vv)]
        u = [heads(lambda h: _dot(bf(inv[2 * i + h]), bf(x))) for i, x in enumerate(g)]
        uv = [stack(bf(x), w) for x, w in zip(u, v)]
        y = [f[c:] + heads(lambda h: _dot(bf(jnp.where(incl, x[c:, h * LANES:(h + 1) * LANES], 0.0)), w))
             for f, x, w in zip(ff, e, uv)]
        add = [_dot_tn(w, stack(ld(bb_ref, b, p), ld(kb_ref, b, p))) for w, (b, p) in zip(uv, pairs)]
        for i, (b, p) in enumerate(pairs):
            y_ref[p, b, pl.ds(t0, c), :] = y[i]
            gc = gc_ref[p, b, pl.ds(t0, 8), :]
            kept = (s[i].reshape(c // 8, 8, LANES) * gc).reshape(c, LANES)
            s_ref[b, p] = kept + jnp.where(head_a, add[i][:c], add[i][c:])
        return carry

    lax.fori_loop(0, tc // c, chunk, 0)


def wkv_chunked(qt, rt, bh, kh, bb, kb, v, gc, s0, tc=256):
    _, nb, t, _ = qt.shape
    seq = pl.BlockSpec((RW_PAIRS, nb, tc, LANES), lambda i, j: (0, 0, j, 0))
    st = pl.BlockSpec((nb, RW_PAIRS, HEAD_DIM, LANES), lambda i, j: (0, 0, 0, 0))
    return pl.pallas_call(
        functools.partial(_wkv_chunk_kernel, nb=nb, tc=tc),
        grid=(1, t // tc),
        in_specs=[seq] * 8 + [st],
        out_specs=[seq, st],
        out_shape=[jax.ShapeDtypeStruct((RW_PAIRS, nb, t, LANES), F32),
                   jax.ShapeDtypeStruct(s0.shape, F32)],
        compiler_params=_params("arbitrary", "arbitrary"),
        name="wkv_chunked",
    )(qt, rt, bh, kh, bb, kb, v, gc, s0)


def _rwkv_post_kernel(y_ref, bonus_ref, g_ref, gng_ref, gnb_ref, seg_ref, o_ref):
    def seg_mean(x):
        return jnp.dot(_split3(x), seg_ref[...], preferred_element_type=F32) * (1.0 / HEAD_DIM)

    y = jnp.concatenate([y_ref[pr] for pr in range(RW_PAIRS)], axis=1)
    d = y - seg_mean(y)
    var = seg_mean(d * d)
    yn = d * lax.rsqrt(var + GN_EPS) * gng_ref[...] + gnb_ref[...]
    o_ref[...] = (yn + bonus_ref[...]) * g_ref[...]


def rwkv_post(y, bonus, g, gn_g, gn_b, seg, tm=512):
    n = y.shape[1]
    tm = _row_tile(n, tm)
    row = pl.BlockSpec((tm, RW_WIDTH), lambda i: (i, 0))
    full = lambda a: pl.BlockSpec(a.shape, lambda i: (0, 0))
    return pl.pallas_call(
        _rwkv_post_kernel,
        grid=(n // tm,),
        in_specs=[pl.BlockSpec((RW_PAIRS, tm, LANES), lambda i: (0, i, 0)), row, row,
                  full(gn_g), full(gn_b), full(seg)],
        out_specs=row,
        out_shape=jax.ShapeDtypeStruct((n, RW_WIDTH), F32),
        compiler_params=_params("parallel"),
        name="rwkv_post",
    )(y, bonus, g, gn_g, gn_b, seg)


def _alibi_slope(h):
    return 2.0 ** (-8.0 * (h + 1) / SW_HEADS)


def _swa_prompt_kernel(sink_ref, q_ref, kp_ref, kc_ref, vp_ref, vc_ref, o_ref):
    n = pl.program_id(1)
    q = q_ref[0]
    kw = jnp.concatenate([kp_ref[0], kc_ref[0]], axis=0).astype(BF16)
    vw = jnp.concatenate([vp_ref[0], vc_ref[0]], axis=0).astype(BF16)
    i = lax.broadcasted_iota(jnp.int32, (WINDOW, 2 * WINDOW), 0)
    j = lax.broadcasted_iota(jnp.int32, (WINDOW, 2 * WINDOW), 1)
    dist = WINDOW + i - j
    valid = (dist >= 0) & (dist < WINDOW) & ((n - 1) * WINDOW + j >= 0)
    distf = dist.astype(F32)
    for c in range(SW_KV_HEADS):
        kc = kw[:, c * HEAD_DIM:(c + 1) * HEAD_DIM]
        vc = vw[:, c * HEAD_DIM:(c + 1) * HEAD_DIM]
        for gi in range(SW_GROUP):
            h = c * SW_GROUP + gi
            qh = q[:, h * HEAD_DIM:(h + 1) * HEAD_DIM].astype(BF16)
            s = lax.dot_general(qh, kc, (((1,), (1,)), ((), ())),
                                preferred_element_type=F32) * HEAD_DIM ** -0.5
            logits = jnp.where(valid, s - _alibi_slope(h) * distf, -jnp.inf)
            sink = sink_ref[h]
            m = jnp.maximum(jnp.max(logits, axis=-1, keepdims=True), sink)
            e = jnp.exp(logits - m)
            prob = e / (jnp.sum(e, axis=-1, keepdims=True) + jnp.exp(sink - m))
            o_ref[0, :, h * HEAD_DIM:(h + 1) * HEAD_DIM] = jnp.dot(
                prob.astype(BF16), vc, preferred_element_type=F32)


def swa_prompt(q, k, v, sinks):
    nb, t, _ = q.shape
    cur = lambda w: pl.BlockSpec((1, WINDOW, w), lambda b, n: (b, n, 0))
    prv = lambda w: pl.BlockSpec((1, WINDOW, w), lambda b, n: (b, jnp.maximum(n - 1, 0), 0))
    return pl.pallas_call(
        _swa_prompt_kernel,
        grid=(nb, t // WINDOW),
        in_specs=[pl.BlockSpec(memory_space=pltpu.SMEM), cur(SW_WIDTH),
                  prv(KV_WIDTH), cur(KV_WIDTH), prv(KV_WIDTH), cur(KV_WIDTH)],
        out_specs=cur(SW_WIDTH),
        out_shape=jax.ShapeDtypeStruct((nb, t, SW_WIDTH), F32),
        compiler_params=_params("parallel", "parallel"),
        name="swa_prompt",
    )(sinks, q, k, k, v, v)


def _swa_sample_kernel(sink_ref, slope_ref, q_ref, kn_ref, vn_ref, ck_ref, cv_ref,
                       o_ref, nk_ref, nv_ref):
    last = lax.broadcasted_iota(jnp.int32, (WINDOW, KV_WIDTH), 0) == WINDOW - 1
    kw = jnp.where(last, kn_ref[0], pltpu.roll(ck_ref[0], WINDOW - 1, 0))
    vw = jnp.where(last, vn_ref[0], pltpu.roll(cv_ref[0], WINDOW - 1, 0))
    nk_ref[0] = kw
    nv_ref[0] = vw
    kwb = kw.astype(BF16)
    vwb = vw.astype(BF16)
    j = lax.broadcasted_iota(jnp.int32, (SW_GROUP, WINDOW), 1)
    distf = (WINDOW - 1 - j).astype(F32)
    q = q_ref[0]
    for c in range(SW_KV_HEADS):
        rows = slice(c * SW_GROUP, (c + 1) * SW_GROUP)
        qc = q[rows].astype(BF16)
        s = lax.dot_general(qc, kwb[:, c * HEAD_DIM:(c + 1) * HEAD_DIM], (((1,), (1,)), ((), ())),
                            preferred_element_type=F32) * HEAD_DIM ** -0.5
        logits = s - slope_ref[rows] * distf
        sink = sink_ref[rows]
        m = jnp.maximum(jnp.max(logits, axis=-1, keepdims=True), sink)
        e = jnp.exp(logits - m)
        prob = e / (jnp.sum(e, axis=-1, keepdims=True) + jnp.exp(sink - m))
        o_ref[0, rows, :] = jnp.dot(prob.astype(BF16), vwb[:, c * HEAD_DIM:(c + 1) * HEAD_DIM],
                                    preferred_element_type=F32)


def swa_sample(q, kn, vn, ck, cv, sinks, slopes):
    nb = q.shape[0]
    per_b = lambda s: pl.BlockSpec((1,) + s, lambda b: (b, 0, 0))
    full = lambda a: pl.BlockSpec(a.shape, lambda b: (0, 0))
    return pl.pallas_call(
        _swa_sample_kernel,
        grid=(nb,),
        in_specs=[full(sinks), full(slopes), per_b((SW_HEADS, HEAD_DIM)), per_b((1, KV_WIDTH)),
                  per_b((1, KV_WIDTH)), per_b((WINDOW, KV_WIDTH)), per_b((WINDOW, KV_WIDTH))],
        out_specs=[per_b((SW_HEADS, HEAD_DIM)), per_b((WINDOW, KV_WIDTH)), per_b((WINDOW, KV_WIDTH))],
        out_shape=[jax.ShapeDtypeStruct((nb, SW_HEADS, HEAD_DIM), F32),
                   jax.ShapeDtypeStruct((nb, WINDOW, KV_WIDTH), F32),
                   jax.ShapeDtypeStruct((nb, WINDOW, KV_WIDTH), F32)],
        compiler_params=_params("parallel"),
        name="swa_sample",
    )(sinks, slopes, q, kn, vn, ck, cv)


def _mem_attn_kernel(q_ref, mk_ref, mv_ref, o_ref, *, rows):
    q = q_ref[0]
    if rows != q.shape[0]:
        q = jnp.broadcast_to(q, (rows, q.shape[1]))
    for h in range(MEM_HEADS):
        cols = slice(h * MEM_HEAD_DIM, (h + 1) * MEM_HEAD_DIM)
        s = lax.dot_general(q[:, cols].astype(BF16), mk_ref[0, :, cols].astype(BF16),
                            (((1,), (1,)), ((), ())),
                            preferred_element_type=F32) * MEM_HEAD_DIM ** -0.5
        m = jnp.max(s, axis=-1, keepdims=True)
        e = jnp.exp(s - m)
        prob = e / jnp.sum(e, axis=-1, keepdims=True)
        o = jnp.dot(prob.astype(BF16), mv_ref[0, :, cols].astype(BF16), preferred_element_type=F32)
        o_ref[0, :, cols] = o[:o_ref.shape[1]]


def mem_attn(q, mk, mv, layer, tm=512):
    ng, t, d = q.shape
    tm = _row_tile(t, tm)
    rows = max(tm, 8)
    mem = pl.BlockSpec((None, 1, MEM_TOKENS, d), lambda g, i: (layer, g, 0, 0))
    return pl.pallas_call(
        functools.partial(_mem_attn_kernel, rows=rows),
        grid=(ng, t // tm),
        in_specs=[pl.BlockSpec((1, tm, d), lambda g, i: (g, i, 0)), mem, mem],
        out_specs=pl.BlockSpec((1, tm, d), lambda g, i: (g, i, 0)),
        out_shape=jax.ShapeDtypeStruct((ng, t, d), F32),
        compiler_params=_params("parallel", "parallel"),
        name="mem_attn",
    )(q, mk, mv)


def _pair_state(s):
    nb = s.shape[0]
    s = s.reshape(nb, RW_PAIRS, 2, HEAD_DIM, HEAD_DIM)
    return jnp.swapaxes(s, 2, 3).reshape(nb, RW_PAIRS, HEAD_DIM, LANES)


def _unpair_state(s):
    nb = s.shape[0]
    s = s.reshape(nb, RW_PAIRS, HEAD_DIM, 2, HEAD_DIM)
    return jnp.swapaxes(s, 2, 3).reshape(nb, RW_HEADS, HEAD_DIM, HEAD_DIM)


def kernel(x_prompt, x_sample, mem_prompt, state_wkv, state_shift, cache_win_k, cache_win_v,
           cache_mem_k, cache_mem_v, ln_g, ln_b, ffn_w1, ffn_w3, ffn_w2, w_in, rw_mu, rw_w0,
           rw_w_up, rw_a0, rw_a_up, rw_g_up, rw_k_k, rw_k_a, rw_r_k, rw_gn_g, rw_gn_b, sw_sinks,
           w_out, mem_wq, mem_wk, mem_wv, mem_wo):
    depth = ln_g.shape[0]
    bp, tp, d = x_prompt.shape
    bs, ts, _ = x_sample.shape
    assert ts == 1 and cache_win_k.shape[2] == WINDOW and tp % WINDOW == 0

    w1b, w3b, w2b = (w.astype(BF16) for w in (ffn_w1, ffn_w3, ffn_w2))
    w_in_b = w_in.astype(BF16)
    w_out_b = w_out.astype(BF16)
    wqb, wkb, wvb, wob = (w.astype(BF16) for w in (mem_wq, mem_wk, mem_wv, mem_wo))
    zpad = jnp.zeros((depth, D_W_LORA, RW_WIDTH), BF16)
    wup_b = jnp.concatenate([rw_w_up.astype(BF16), zpad], axis=1)
    aup_b = jnp.concatenate([zpad, rw_a_up.astype(BF16)], axis=1)
    gup_b = rw_g_up.astype(BF16)
    hid = jnp.arange(3 * RW_WIDTH) % RW_WIDTH // HEAD_DIM
    seg3 = (hid[:, None] == (jnp.arange(RW_WIDTH) // HEAD_DIM)[None, :]).astype(BF16)
    slopes = jnp.asarray([[_alibi_slope(h)] for h in range(SW_HEADS)], F32)
    q0, k0, v0 = RW_COLS, RW_COLS + SW_WIDTH, RW_COLS + SW_WIDTH + KV_WIDTH
    row = lambda a: a.reshape(1, -1)

    def layer(l, x, nb, t, prev_fn, s0, swa_fn, mem_fn):
        x = ffn_ln(x, w1b[l, 0], w3b[l, 0], w2b[l, 0], row(ln_g[l, 0]), row(ln_b[l, 0]))
        p_rw, q, k, v = matmul_multi(
            x, [w_in_b[l, :, :q0], w_in_b[l, :, q0:k0], w_in_b[l, :, k0:v0], w_in_b[l, :, v0:]])
        *ops, g, bonus = rwkv_prep(
            p_rw, prev_fn(p_rw), row(rw_mu[l]), row(rw_w0[l]), wup_b[l], row(rw_a0[l]), aup_b[l],
            gup_b[l], row(rw_k_k[l]), row(rw_k_a[l]), row(rw_r_k[l]), seg3, seq_len=t)
        ops = [a.reshape(RW_PAIRS, nb, t, LANES) for a in ops]
        if t > 1:
            y, s_fin = wkv_chunked(*ops, s0)
        else:
            y, s_fin = wkv_scan(*ops, s0, bb=8, tc=1)
        y_rw = rwkv_post(y.reshape(RW_PAIRS, nb * t, LANES), bonus, g, row(rw_gn_g[l]), row(rw_gn_b[l]),
                         seg3)
        y_sw, win_k, win_v = swa_fn(q, k, v)
        shift = x.reshape(nb, t, d)[:, -1]
        x = proj_ln(x, [y_rw, y_sw], [w_out_b[l, :RW_WIDTH], w_out_b[l, RW_WIDTH:]],
                    row(ln_g[l, 1]), row(ln_b[l, 1]))
        (qm,) = matmul_multi(x, [wqb[l]])
        o = mem_fn(qm)
        x = proj_ln(x, [o], [wob[l]], row(ln_g[l, 2]), row(ln_b[l, 2]))
        x = ffn_ln(x, w1b[l, 1], w3b[l, 1], w2b[l, 1], row(ln_g[l, 3]), row(ln_b[l, 3]))
        return x, _unpair_state(s_fin), shift, win_k, win_v

    xp = x_prompt.reshape(bp * tp, d)
    p_wkv, p_shift, p_wk, p_wv, p_mk, p_mv = [], [], [], [], [], []
    for l in range(depth):
        prev_prompt = lambda p_rw: p_rw

        def swa_p(q, k, v, l=l):
            k3 = k.reshape(bp, tp, KV_WIDTH)
            v3 = v.reshape(bp, tp, KV_WIDTH)
            y = swa_prompt(q.reshape(bp, tp, SW_WIDTH), k3, v3, sw_sinks[l])
            tail = lambda a: a[:, -WINDOW:].reshape(bp, WINDOW, SW_KV_HEADS, HEAD_DIM)
            return y.reshape(bp * tp, SW_WIDTH), tail(k3), tail(v3)

        mk, mv = matmul_multi(mem_prompt.reshape(bp * MEM_TOKENS, d), [wkb[l], wvb[l]])
        mk = mk.reshape(bp, MEM_TOKENS, d)
        mv = mv.reshape(bp, MEM_TOKENS, d)

        def mem_p(qm, mk=mk, mv=mv):
            return mem_attn(qm.reshape(bp, tp, d), mk[None], mv[None], 0).reshape(bp * tp, d)

        s0 = jnp.zeros((bp, RW_PAIRS, HEAD_DIM, LANES), F32)
        xp, s_fin, shift, wk_, wv_ = layer(l, xp, bp, tp, prev_prompt, s0, swa_p, mem_p)
        p_wkv.append(s_fin)
        p_shift.append(shift)
        p_wk.append(wk_)
        p_wv.append(wv_)
        p_mk.append(mk.reshape(bp, MEM_TOKENS, MEM_HEADS, MEM_HEAD_DIM))
        p_mv.append(mv.reshape(bp, MEM_TOKENS, MEM_HEADS, MEM_HEAD_DIM))

    xs = x_sample.reshape(bs, d)
    mem_k_flat = cache_mem_k.reshape(depth, bs, MEM_TOKENS, d)
    mem_v_flat = cache_mem_v.reshape(depth, bs, MEM_TOKENS, d)
    s_wkv, s_shift, s_wk, s_wv = [], [], [], []
    for l in range(depth):
        def prev_sample(p_rw, l=l):
            (prev,) = matmul_multi(state_shift[l], [w_in_b[l, :, :q0]])
            return prev

        def swa_s(q, k, v, l=l):
            o, nk, nv = swa_sample(
                q.reshape(bs, SW_HEADS, HEAD_DIM), k.reshape(bs, 1, KV_WIDTH), v.reshape(bs, 1, KV_WIDTH),
                cache_win_k[l].reshape(bs, WINDOW, KV_WIDTH), cache_win_v[l].reshape(bs, WINDOW, KV_WIDTH),
                sw_sinks[l].reshape(SW_HEADS, 1), slopes)
            unflat = lambda a: a.reshape(bs, WINDOW, SW_KV_HEADS, HEAD_DIM)
            return o.reshape(bs, SW_WIDTH), unflat(nk), unflat(nv)

        def mem_s(qm, l=l):
            return mem_attn(qm.reshape(bs, 1, d), mem_k_flat, mem_v_flat, l).reshape(bs, d)

        xs, s_fin, shift, wk_, wv_ = layer(l, xs, bs, 1, prev_sample, _pair_state(state_wkv[l]),
                                           swa_s, mem_s)
        s_wkv.append(s_fin)
        s_shift.append(shift)
        s_wk.append(wk_)
        s_wv.append(wv_)

    return (xp.reshape(bp, tp, d), xs.reshape(bs, 1, d),
            jnp.stack(p_wkv), jnp.stack(p_shift), jnp.stack(p_wk), jnp.stack(p_wv),
            jnp.stack(p_mk), jnp.stack(p_mv),
            jnp.stack(s_wkv), jnp.stack(s_shift), jnp.stack(s_wk), jnp.stack(s_wv))
```

```python
import functools

import jax
import jax.numpy as jnp
from jax import lax
from jax.experimental import pallas as pl
from jax.experimental.pallas import tpu as pltpu

F32 = jnp.float32
BF16 = jnp.bfloat16

D_MODEL = 1024
HEAD_DIM = 64
RW_WIDTH = 512
RW_HEADS = 8
RW_PAIRS = RW_HEADS // 2
SW_WIDTH = 512
SW_HEADS = 8
SW_KV_HEADS = 2
SW_GROUP = SW_HEADS // SW_KV_HEADS
KV_WIDTH = SW_KV_HEADS * HEAD_DIM
WINDOW = 128
D_W_LORA = 64
D_A_LORA = 64
D_G_LORA = 128
RW_COLS = 3 * RW_WIDTH + D_W_LORA + D_A_LORA + D_G_LORA
LORA_WA_START = 3 * RW_WIDTH
LORA_G_START = LORA_WA_START + D_W_LORA + D_A_LORA
MEM_TOKENS = 256
MEM_HEADS = 4
MEM_HEAD_DIM = D_MODEL // MEM_HEADS
D_FF = 2816
DEPTH = 4
ALPHA = (2.0 * DEPTH) ** 0.25
LN_EPS = 1e-5
GN_EPS = 64e-5
NORM_EPS = 1e-12

CHUNK = 64
FF_TILE = 256
LANES = 128
VMEM_LIMIT = 56 * 1024 * 1024


def _params(*semantics):
    return pltpu.CompilerParams(dimension_semantics=semantics, vmem_limit_bytes=VMEM_LIMIT)


def _row_tile(n, want):
    return want if n % want == 0 else n


def _layer_norm(z, g, b):
    mu = jnp.mean(z, axis=-1, keepdims=True)
    d = z - mu
    var = jnp.mean(d * d, axis=-1, keepdims=True)
    return d * lax.rsqrt(var + LN_EPS) * g + b


def _sigmoid(x):
    return 1.0 / (1.0 + jnp.exp(-x))


def _split3(x):
    hi = x.astype(BF16)
    r1 = x - hi.astype(F32)
    mid = r1.astype(BF16)
    lo = (r1 - mid.astype(F32)).astype(BF16)
    return jnp.concatenate([hi, mid, lo], axis=1)


def _split2(x):
    hi = x.astype(BF16)
    lo = (x - hi.astype(F32)).astype(BF16)
    return jnp.concatenate([hi, lo], axis=1)


def _mm_kernel(x_ref, *refs, n_out):
    xb = x_ref[...].astype(BF16)
    for w_ref, o_ref in zip(refs[:n_out], refs[n_out:]):
        o_ref[...] = jnp.dot(xb, w_ref[...], preferred_element_type=F32)


def matmul_multi(x, ws, tm=512):
    n, k = x.shape
    tm = _row_tile(n, tm)
    n_out = len(ws)
    return pl.pallas_call(
        functools.partial(_mm_kernel, n_out=n_out),
        grid=(n // tm,),
        in_specs=[pl.BlockSpec((tm, k), lambda i: (i, 0))]
        + [pl.BlockSpec(w.shape, lambda i: (0, 0)) for w in ws],
        out_specs=[pl.BlockSpec((tm, w.shape[1]), lambda i: (i, 0)) for w in ws],
        out_shape=[jax.ShapeDtypeStruct((n, w.shape[1]), F32) for w in ws],
        compiler_params=_params("parallel"),
        name="matmul_multi",
    )(x, *ws)


def _ffn_kernel(x_ref, w1_ref, w3_ref, w2_ref, g_ref, b_ref, o_ref, xb_ref, acc_ref, *, n_ff):
    xb_ref[...] = x_ref[...].astype(BF16)

    def part(c):
        xb = xb_ref[...]
        h1 = jnp.dot(xb, w1_ref[c], preferred_element_type=F32)
        h3 = jnp.dot(xb, w3_ref[c], preferred_element_type=F32)
        h = (h1 * _sigmoid(h1)) * h3
        return jnp.dot(h.astype(BF16), w2_ref[c], preferred_element_type=F32)

    acc_ref[...] = part(0)

    def body(c, carry):
        acc_ref[...] += part(c)
        return carry

    lax.fori_loop(1, n_ff, body, 0)
    z = ALPHA * x_ref[...] + 0.5 * acc_ref[...]
    o_ref[...] = _layer_norm(z, g_ref[...], b_ref[...])


def ffn_ln(x, w1, w3, w2, lead, g, b, tm=1024):
    n, d = x.shape
    tm = _row_tile(n, tm)
    n_ff = D_FF // FF_TILE
    resident = lambda a: pl.BlockSpec((None,) * len(lead) + a.shape[len(lead):],
                                      lambda i: lead + (0, 0, 0), pipeline_mode=pl.Buffered(1))
    return pl.pallas_call(
        functools.partial(_ffn_kernel, n_ff=n_ff),
        grid=(n // tm,),
        in_specs=[
            pl.BlockSpec((tm, d), lambda i: (i, 0)),
            resident(w1), resident(w3), resident(w2),
            pl.BlockSpec((1, d), lambda i: (0, 0)),
            pl.BlockSpec((1, d), lambda i: (0, 0)),
        ],
        out_specs=pl.BlockSpec((tm, d), lambda i: (i, 0)),
        out_shape=jax.ShapeDtypeStruct((n, d), F32),
        scratch_shapes=[pltpu.VMEM((tm, d), BF16), pltpu.VMEM((tm, d), F32)],
        compiler_params=_params("parallel"),
        name="ffn_ln",
    )(x, w1, w3, w2, g, b)


def _proj_ln_kernel(x_ref, *refs, n_in):
    a_refs = refs[:n_in]
    w_refs = refs[n_in:2 * n_in]
    g_ref, b_ref, o_ref = refs[2 * n_in:]
    f = None
    for a_ref, w_ref in zip(a_refs, w_refs):
        t = jnp.dot(a_ref[...].astype(BF16), w_ref[...], preferred_element_type=F32)
        f = t if f is None else f + t
    o_ref[...] = _layer_norm(ALPHA * x_ref[...] + f, g_ref[...], b_ref[...])


def proj_ln(x, acts, ws, g, b, tm=512):
    n, d = x.shape
    tm = _row_tile(n, tm)
    n_in = len(acts)
    return pl.pallas_call(
        functools.partial(_proj_ln_kernel, n_in=n_in),
        grid=(n // tm,),
        in_specs=[pl.BlockSpec((tm, d), lambda i: (i, 0))]
        + [pl.BlockSpec((tm, a.shape[1]), lambda i: (i, 0)) for a in acts]
        + [pl.BlockSpec(w.shape, lambda i: (0, 0)) for w in ws]
        + [pl.BlockSpec((1, d), lambda i: (0, 0))] * 2,
        out_specs=pl.BlockSpec((tm, d), lambda i: (i, 0)),
        out_shape=jax.ShapeDtypeStruct((n, d), F32),
        compiler_params=_params("parallel"),
        name="proj_ln",
    )(x, *acts, *ws, g, b)


def _rwkv_prep_kernel(p_ref, prev_ref, mu_ref, w0_ref, wup_ref, a0_ref, aup_ref, gup_ref,
                      kk_ref, ka_ref, rk_ref, seg_ref, *refs, chunked, tiles_per_seq):
    if chunked:
        tril_ref, ones_ref = refs[:2]
        refs = refs[2:]
    g_out, bonus_out = refs[-2:]
    p = p_ref[...]
    if chunked:
        first = (pl.program_id(0) % tiles_per_seq) == 0
        above = jnp.where(first, 0.0, prev_ref[7:8, :])
        top = lax.broadcasted_iota(jnp.int32, p.shape, 0) == 0
        prev = jnp.where(top, above, pltpu.roll(p, 1, 0))
    else:
        prev = prev_ref[...]
    xm = p + (prev - p) * mu_ref[...]
    r = xm[:, 0:RW_WIDTH]
    k = xm[:, RW_WIDTH:2 * RW_WIDTH]
    v = xm[:, 2 * RW_WIDTH:3 * RW_WIDTH]
    wa = xm[:, LORA_WA_START:LORA_G_START]
    gl = xm[:, LORA_G_START:RW_COLS]

    def seg_sum(x):
        return jnp.dot(_split3(x), seg_ref[...], preferred_element_type=F32)

    lw = jnp.dot(jnp.tanh(wa).astype(BF16), wup_ref[...], preferred_element_type=F32)
    la = jnp.dot(wa.astype(BF16), aup_ref[...], preferred_element_type=F32)
    z = -(w0_ref[...] + lw)
    softplus = jnp.maximum(z, 0.0) + jnp.log(1.0 + jnp.exp(-jnp.abs(z)))
    w_log = -softplus - 0.5
    log_decay = -jnp.exp(w_log)
    a = _sigmoid(a0_ref[...] + la)
    g = jnp.dot(_sigmoid(gl).astype(BF16), gup_ref[...], preferred_element_type=F32)
    kk = k * kk_ref[...]
    nrm = jnp.sqrt(seg_sum(kk * kk))
    kk = kk / jnp.maximum(nrm, NORM_EPS)
    k_mod = k * (1.0 + (a - 1.0) * ka_ref[...])
    kka = kk * a
    if chunked:
        parts = _split3(log_decay)

        def time_sum(m_ref):
            s3 = jnp.dot(m_ref[...], parts, preferred_element_type=F32)
            return s3[:, :RW_WIDTH] + s3[:, RW_WIDTH:2 * RW_WIDTH] + s3[:, 2 * RW_WIDTH:]

        cum = time_sum(tril_ref)
        tot = time_sum(ones_ref)
        grow = jnp.exp(-cum)
        rest = jnp.exp(tot - cum)
        outs = (kk * jnp.exp(cum - log_decay), r * jnp.exp(cum), -kka * grow, k_mod * grow,
                -kka * rest, k_mod * rest, v, jnp.exp(tot))
    else:
        outs = (r, jnp.exp(log_decay), k_mod, v, kk, kka)
    for o_ref, val in zip(refs, outs):
        for pr in range(RW_PAIRS):
            o_ref[pr] = val[:, pr * LANES:(pr + 1) * LANES].astype(o_ref.dtype)
    g_out[...] = g
    bonus_out[...] = seg_sum(r * k_mod * rk_ref[...]) * v


def rwkv_prep(p, prev, mu, w0, wup, a0, aup, gup, k_k, k_a, r_k, seg, seq_len, tm=256):
    n = p.shape[0]
    chunked = seq_len > 1
    tm = _row_tile(n, tm)
    row = lambda c: pl.BlockSpec((tm, c), lambda i: (i, 0))
    prev_spec = row(RW_COLS)
    if chunked:
        assert seq_len % tm == 0
        prev_spec = pl.BlockSpec((8, RW_COLS), lambda i: (jnp.maximum(i * (tm // 8) - 1, 0), 0))
    full = lambda a: pl.BlockSpec(a.shape, lambda i: (0, 0))
    consts = (mu, w0, wup, a0, aup, gup, k_k, k_a, r_k, seg)
    dtypes = [F32] * 6
    if chunked:
        assert tm % CHUNK == 0
        t_idx = jnp.arange(tm)
        same = (t_idx[:, None] // CHUNK) == (t_idx[None, :] // CHUNK)
        consts += ((same & (t_idx[None, :] <= t_idx[:, None])).astype(BF16), same.astype(BF16))
        dtypes = [BF16] * 7 + [F32]
    pair = pl.BlockSpec((RW_PAIRS, tm, LANES), lambda i: (0, i, 0))
    return pl.pallas_call(
        functools.partial(_rwkv_prep_kernel, chunked=chunked, tiles_per_seq=max(seq_len // tm, 1)),
        grid=(n // tm,),
        in_specs=[row(RW_COLS), prev_spec] + [full(c) for c in consts],
        out_specs=[pair] * len(dtypes) + [row(RW_WIDTH)] * 2,
        out_shape=[jax.ShapeDtypeStruct((RW_PAIRS, n, LANES), dt) for dt in dtypes]
        + [jax.ShapeDtypeStruct((n, RW_WIDTH), F32)] * 2,
        compiler_params=_params("parallel"),
        name="rwkv_prep",
    )(p, prev, *consts)


def _wkv_step_kernel(r_ref, w_ref, k_ref, v_ref, kk_ref, kka_ref, s0_ref, y_ref, s_ref, *, bb):
    sub = lax.broadcasted_iota(jnp.int32, (HEAD_DIM, LANES), 0)
    lane = lax.broadcasted_iota(jnp.int32, (HEAD_DIM, LANES), 1)
    diag = (lane % HEAD_DIM) == sub
    rj = lax.broadcasted_iota(jnp.int32, (2 * LANES, LANES), 0)
    cl = lax.broadcasted_iota(jnp.int32, (2 * LANES, LANES), 1)
    ones2 = jnp.where(((rj % LANES) // HEAD_DIM) == (cl // HEAD_DIM), 1.0, 0.0).astype(BF16)

    def head_sums(xs):
        out = jnp.dot(_split2(jnp.concatenate(xs, axis=0)), ones2, preferred_element_type=F32)
        return [out[i * HEAD_DIM:(i + 1) * HEAD_DIM] for i in range(len(xs))]

    pairs = [(b, p) for b in range(bb) for p in range(RW_PAIRS)]
    n = len(pairs)
    row = lambda ref, i: ref[pairs[i][1], pairs[i][0]]
    tiles = lambda x: x.reshape(HEAD_DIM // 8, 8, LANES)
    flat = lambda x: x.reshape(HEAD_DIM, LANES)
    s = [tiles(s0_ref[b, p]) for b, p in pairs]
    red = head_sums([flat(s[i] * row(kk_ref, i)) for i in range(n)]
                    + [flat(jnp.where(tiles(diag), row(v_ref, i), 0.0)) for i in range(n)])
    s = [s[i] * row(w_ref, i) - tiles(red[i]) * row(kka_ref, i) + tiles(red[n + i]) * row(k_ref, i)
         for i in range(n)]
    yb = head_sums([flat(s[i] * row(r_ref, i)) for i in range(n)])
    for i, (b, p) in enumerate(pairs):
        s_ref[b, p] = flat(s[i])
        y_ref[p, b] = jnp.sum(jnp.where(diag, yb[i], 0.0), axis=0, keepdims=True)


def wkv_step(r, w, k, v, kk, kka, s0, bb=8):
    _, nb, t, _ = r.shape
    assert t == 1 and nb % bb == 0
    seq = pl.BlockSpec((RW_PAIRS, bb, 1, LANES), lambda i: (0, i, 0, 0))
    st = pl.BlockSpec((bb, RW_PAIRS, HEAD_DIM, LANES), lambda i: (i, 0, 0, 0))
    return pl.pallas_call(
        functools.partial(_wkv_step_kernel, bb=bb),
        grid=(nb // bb,),
        in_specs=[seq] * 6 + [st],
        out_specs=[seq, st],
        out_shape=[jax.ShapeDtypeStruct((RW_PAIRS, nb, 1, LANES), F32),
                   jax.ShapeDtypeStruct(s0.shape, F32)],
        compiler_params=_params("parallel"),
        name="wkv_step",
    )(r, w, k, v, kk, kka, s0)


def _dot_nt(a, b):
    return lax.dot_general(a, b, (((1,), (1,)), ((), ())), preferred_element_type=F32)


def _dot_tn(a, b):
    return lax.dot_general(a, b, (((0,), (0,)), ((), ())), preferred_element_type=F32)


def _dot(a, b):
    return jnp.dot(a, b, preferred_element_type=F32)


def _wkv_chunk_kernel(qt_ref, rt_ref, bh_ref, kh_ref, bb_ref, kb_ref, v_ref, gc_ref, s0_ref,
                      y_ref, s_ref, *, nb, tc):
    @pl.when(pl.program_id(1) == 0)
    def _():
        s_ref[...] = s0_ref[...]

    c = CHUNK
    assert c == HEAD_DIM and 2 * c == LANES
    row = lax.broadcasted_iota(jnp.int32, (c, LANES), 0)
    lane = lax.broadcasted_iota(jnp.int32, (c, LANES), 1)
    head_a = lane < HEAD_DIM
    head_a2 = lax.broadcasted_iota(jnp.int32, (2 * c, LANES), 1) < HEAD_DIM
    m_mask = (lane >= c) & (lane - c < row)
    incl = (lane % c) <= row
    r64 = lax.broadcasted_iota(jnp.int32, (c, c), 0)
    c64 = lax.broadcasted_iota(jnp.int32, (c, c), 1)
    strict = c64 < r64
    eye = jnp.where(c64 == r64, 1.0, 0.0)
    pairs = [(b, p) for b in range(nb) for p in range(RW_PAIRS)]
    bf = lambda x: x.astype(BF16)
    stack = lambda x, y: jnp.concatenate([x, y], axis=0)

    def chunk(ci, carry):
        t0 = pl.multiple_of(ci * c, c)
        ld = lambda ref, b, p: ref[p, b, pl.ds(t0, c), :]
        qr = [stack(ld(qt_ref, b, p), ld(rt_ref, b, p)) for b, p in pairs]
        bk = [stack(ld(bh_ref, b, p), ld(kh_ref, b, p)) for b, p in pairs]
        zero = jnp.zeros((), BF16)
        e = [_dot_nt(q, stack(jnp.where(head_a2, x, zero), jnp.where(head_a2, zero, x)))
             for q, x in zip(qr, bk)]

        pw = [jnp.where(strict, x[:c, h * LANES:h * LANES + c], 0.0) for x in e for h in range(2)]
        inv = [eye + n for n in pw]
        for _ in range(5):
            pb = [bf(x) for x in pw]
            pw = [_dot(x, x) for x in pb]
            inv = [t + _dot(bf(x), bf(t)) for x, t in zip(pw, inv)]

        s = [s_ref[b, p] for b, p in pairs]
        ff = [_dot_nt(q, bf(stack(jnp.where(head_a, x, 0.0), jnp.where(head_a, 0.0, x))))
              for q, x in zip(qr, s)]
        v = [ld(v_ref, b, p) for b, p in pairs]
        vv = [stack(x, x) for x in v]
        heads = lambda fn: jnp.where(head_a, fn(0), fn(1))
        g = [f[:c] + heads(lambda h: _dot(bf(jnp.where(m_mask, x[:c, h * LANES:(h + 1) * LANES], 0.0)), w))
             for f, x, w in zip(ff, e, vv)]
        u = [heads(lambda h: _dot(bf(inv[2 * i + h]), bf(x))) for i, x in enumerate(g)]
        uv = [stack(bf(x), w) for x, w in zip(u, v)]
        y = [f[c:] + heads(lambda h: _dot(bf(jnp.where(incl, x[c:, h * LANES:(h + 1) * LANES], 0.0)), w))
             for f, x, w in zip(ff, e, uv)]
        add = [_dot_tn(w, stack(ld(bb_ref, b, p), ld(kb_ref, b, p))) for w, (b, p) in zip(uv, pairs)]
        for i, (b, p) in enumerate(pairs):
            y_ref[p, b, pl.ds(t0, c), :] = y[i]
            gc = gc_ref[p, b, pl.ds(t0, 8), :]
            kept = (s[i].reshape(c // 8, 8, LANES) * gc).reshape(c, LANES)
            s_ref[b, p] = kept + jnp.where(head_a, add[i][:c], add[i][c:])
        return carry

    lax.fori_loop(0, tc // c, chunk, 0)


def wkv_chunked(qt, rt, bh, kh, bb, kb, v, gc, s0, tc=256):
    _, nb, t, _ = qt.shape
    seq = pl.BlockSpec((RW_PAIRS, nb, tc, LANES), lambda i, j: (0, 0, j, 0))
    st = pl.BlockSpec((nb, RW_PAIRS, HEAD_DIM, LANES), lambda i, j: (0, 0, 0, 0))
    return pl.pallas_call(
        functools.partial(_wkv_chunk_kernel, nb=nb, tc=tc),
        grid=(1, t // tc),
        in_specs=[seq] * 8 + [st],
        out_specs=[seq, st],
        out_shape=[jax.ShapeDtypeStruct((RW_PAIRS, nb, t, LANES), F32),
                   jax.ShapeDtypeStruct(s0.shape, F32)],
        compiler_params=_params("arbitrary", "arbitrary"),
        name="wkv_chunked",
    )(qt, rt, bh, kh, bb, kb, v, gc, s0)


def _rwkv_post_kernel(y_ref, bonus_ref, g_ref, gng_ref, gnb_ref, seg_ref, o_ref):
    def seg_mean(x):
        return jnp.dot(_split3(x), seg_ref[...], preferred_element_type=F32) * (1.0 / HEAD_DIM)

    y = jnp.concatenate([y_ref[pr] for pr in range(RW_PAIRS)], axis=1)
    d = y - seg_mean(y)
    var = seg_mean(d * d)
    yn = d * lax.rsqrt(var + GN_EPS) * gng_ref[...] + gnb_ref[...]
    o_ref[...] = (yn + bonus_ref[...]) * g_ref[...]


def rwkv_post(y, bonus, g, gn_g, gn_b, seg, tm=512):
    n = y.shape[1]
    tm = _row_tile(n, tm)
    row = pl.BlockSpec((tm, RW_WIDTH), lambda i: (i, 0))
    full = lambda a: pl.BlockSpec(a.shape, lambda i: (0, 0))
    return pl.pallas_call(
        _rwkv_post_kernel,
        grid=(n // tm,),
        in_specs=[pl.BlockSpec((RW_PAIRS, tm, LANES), lambda i: (0, i, 0)), row, row,
                  full(gn_g), full(gn_b), full(seg)],
        out_specs=row,
        out_shape=jax.ShapeDtypeStruct((n, RW_WIDTH), F32),
        compiler_params=_params("parallel"),
        name="rwkv_post",
    )(y, bonus, g, gn_g, gn_b, seg)


def _alibi_slope(h):
    return 2.0 ** (-8.0 * (h + 1) / SW_HEADS)


def _swa_prompt_kernel(sink_ref, q_ref, kp_ref, kc_ref, vp_ref, vc_ref, o_ref):
    n = pl.program_id(1)
    q = q_ref[0]
    kw = jnp.concatenate([kp_ref[0], kc_ref[0]], axis=0).astype(BF16)
    vw = jnp.concatenate([vp_ref[0], vc_ref[0]], axis=0).astype(BF16)
    i = lax.broadcasted_iota(jnp.int32, (WINDOW, 2 * WINDOW), 0)
    j = lax.broadcasted_iota(jnp.int32, (WINDOW, 2 * WINDOW), 1)
    dist = WINDOW + i - j
    valid = (dist >= 0) & (dist < WINDOW) & ((n - 1) * WINDOW + j >= 0)
    distf = dist.astype(F32)
    for c in range(SW_KV_HEADS):
        kc = kw[:, c * HEAD_DIM:(c + 1) * HEAD_DIM]
        vc = vw[:, c * HEAD_DIM:(c + 1) * HEAD_DIM]
        for gi in range(SW_GROUP):
            h = c * SW_GROUP + gi
            qh = q[:, h * HEAD_DIM:(h + 1) * HEAD_DIM].astype(BF16)
            s = lax.dot_general(qh, kc, (((1,), (1,)), ((), ())),
                                preferred_element_type=F32) * HEAD_DIM ** -0.5
            logits = jnp.where(valid, s - _alibi_slope(h) * distf, -jnp.inf)
            sink = sink_ref[h]
            m = jnp.maximum(jnp.max(logits, axis=-1, keepdims=True), sink)
            e = jnp.exp(logits - m)
            prob = e / (jnp.sum(e, axis=-1, keepdims=True) + jnp.exp(sink - m))
            o_ref[0, :, h * HEAD_DIM:(h + 1) * HEAD_DIM] = jnp.dot(
                prob.astype(BF16), vc, preferred_element_type=F32)


def swa_prompt(q, k, v, sinks):
    nb, t, _ = q.shape
    cur = lambda w: pl.BlockSpec((1, WINDOW, w), lambda b, n: (b, n, 0))
    prv = lambda w: pl.BlockSpec((1, WINDOW, w), lambda b, n: (b, jnp.maximum(n - 1, 0), 0))
    return pl.pallas_call(
        _swa_prompt_kernel,
        grid=(nb, t // WINDOW),
        in_specs=[pl.BlockSpec(memory_space=pltpu.SMEM), cur(SW_WIDTH),
                  prv(KV_WIDTH), cur(KV_WIDTH), prv(KV_WIDTH), cur(KV_WIDTH)],
        out_specs=cur(SW_WIDTH),
        out_shape=jax.ShapeDtypeStruct((nb, t, SW_WIDTH), F32),
        compiler_params=_params("parallel", "parallel"),
        name="swa_prompt",
    )(sinks, q, k, k, v, v)


def _swa_sample_kernel(sink_ref, slope_ref, q_ref, kn_ref, vn_ref, ck_ref, cv_ref,
                       o_ref, nk_ref, nv_ref, *, bb):
    last = lax.broadcasted_iota(jnp.int32, (WINDOW, KV_WIDTH), 0) == WINDOW - 1
    j = lax.broadcasted_iota(jnp.int32, (SW_HEADS, WINDOW), 1)
    bias = slope_ref[...] * (WINDOW - 1 - j).astype(F32)
    sink = sink_ref[...]
    seqs = range(bb)
    kw = [jnp.where(last, kn_ref[b], pltpu.roll(ck_ref[b], WINDOW - 1, 0)) for b in seqs]
    vw = [jnp.where(last, vn_ref[b], pltpu.roll(cv_ref[b], WINDOW - 1, 0)) for b in seqs]
    s = [_dot_nt(q_ref[b].astype(BF16), kw[b].astype(BF16)) * HEAD_DIM ** -0.5 - bias for b in seqs]
    m = [jnp.maximum(jnp.max(x, axis=-1, keepdims=True), sink) for x in s]
    e = [jnp.exp(x - mx) for x, mx in zip(s, m)]
    prob = [x / (jnp.sum(x, axis=-1, keepdims=True) + jnp.exp(sink - mx)) for x, mx in zip(e, m)]
    o = [_dot(prob[b].astype(BF16), vw[b].astype(BF16)) for b in seqs]
    for b in seqs:
        nk_ref[b] = kw[b]
        nv_ref[b] = vw[b]
        o_ref[b] = o[b]


def swa_sample(q, kn, vn, ck, cv, sinks, slopes, bb=8):
    nb = q.shape[0]
    assert nb % bb == 0
    own = (jnp.arange(SW_HEADS) // SW_GROUP)[:, None] == jnp.arange(SW_KV_HEADS)[None, :]
    q2 = jnp.where(own[None, :, :, None], q[:, :, None, :], 0.0).reshape(nb, SW_HEADS, KV_WIDTH)
    per_b = lambda s: pl.BlockSpec((bb,) + s, lambda b: (b, 0, 0))
    full = lambda a: pl.BlockSpec(a.shape, lambda b: (0, 0))
    o2, nk, nv = pl.pallas_call(
        functools.partial(_swa_sample_kernel, bb=bb),
        grid=(nb // bb,),
        in_specs=[full(sinks), full(slopes), per_b((SW_HEADS, KV_WIDTH)), per_b((1, KV_WIDTH)),
                  per_b((1, KV_WIDTH)), per_b((WINDOW, KV_WIDTH)), per_b((WINDOW, KV_WIDTH))],
        out_specs=[per_b((SW_HEADS, KV_WIDTH)), per_b((WINDOW, KV_WIDTH)), per_b((WINDOW, KV_WIDTH))],
        out_shape=[jax.ShapeDtypeStruct((nb, SW_HEADS, KV_WIDTH), F32),
                   jax.ShapeDtypeStruct((nb, WINDOW, KV_WIDTH), F32),
                   jax.ShapeDtypeStruct((nb, WINDOW, KV_WIDTH), F32)],
        compiler_params=_params("parallel"),
        name="swa_sample",
    )(sinks, slopes, q2, kn, vn, ck, cv)
    o = jnp.sum(jnp.where(own[None, :, :, None], o2.reshape(nb, SW_HEADS, SW_KV_HEADS, HEAD_DIM), 0.0), axis=2)
    return o, nk, nv


def _mem_attn_kernel(q_ref, mk_ref, mv_ref, o_ref, *, rows):
    q = q_ref[0]
    if rows != q.shape[0]:
        q = jnp.broadcast_to(q, (rows, q.shape[1]))
    for h in range(MEM_HEADS):
        cols = slice(h * MEM_HEAD_DIM, (h + 1) * MEM_HEAD_DIM)
        s = lax.dot_general(q[:, cols].astype(BF16), mk_ref[0, :, cols].astype(BF16),
                            (((1,), (1,)), ((), ())),
                            preferred_element_type=F32) * MEM_HEAD_DIM ** -0.5
        m = jnp.max(s, axis=-1, keepdims=True)
        e = jnp.exp(s - m)
        prob = e / jnp.sum(e, axis=-1, keepdims=True)
        o = jnp.dot(prob.astype(BF16), mv_ref[0, :, cols].astype(BF16), preferred_element_type=F32)
        o_ref[0, :, cols] = o[:o_ref.shape[1]]


def mem_attn(q, mk, mv, layer, tm=512):
    ng, t, d = q.shape
    tm = _row_tile(t, tm)
    rows = max(tm, 8)
    mem = pl.BlockSpec((None, 1, MEM_TOKENS, d), lambda g, i: (layer, g, 0, 0))
    return pl.pallas_call(
        functools.partial(_mem_attn_kernel, rows=rows),
        grid=(ng, t // tm),
        in_specs=[pl.BlockSpec((1, tm, d), lambda g, i: (g, i, 0)), mem, mem],
        out_specs=pl.BlockSpec((1, tm, d), lambda g, i: (g, i, 0)),
        out_shape=jax.ShapeDtypeStruct((ng, t, d), F32),
        compiler_params=_params("parallel", "parallel"),
        name="mem_attn",
    )(q, mk, mv)


def _mem_attn_token_kernel(q_ref, mk_ref, mv_ref, o_ref, *, bb):
    rows = MEM_TOKENS * MEM_HEADS
    lane = lax.broadcasted_iota(jnp.int32, (MEM_HEADS, rows), 1)
    head = lax.broadcasted_iota(jnp.int32, (MEM_HEADS, rows), 0)
    own = (lane % MEM_HEADS) == head
    for b in range(bb):
        q = q_ref[b]
        q4 = jnp.concatenate([q[:, h * MEM_HEAD_DIM:(h + 1) * MEM_HEAD_DIM] for h in range(MEM_HEADS)],
                             axis=0).astype(BF16)
        k2 = mk_ref[b].reshape(rows, MEM_HEAD_DIM).astype(BF16)
        v2 = mv_ref[b].reshape(rows, MEM_HEAD_DIM).astype(BF16)
        s = jnp.where(own, _dot_nt(q4, k2) * MEM_HEAD_DIM ** -0.5, -jnp.inf)
        m = jnp.max(s, axis=-1, keepdims=True)
        e = jnp.exp(s - m)
        prob = e / jnp.sum(e, axis=-1, keepdims=True)
        o = _dot(prob.astype(BF16), v2)
        for h in range(MEM_HEADS):
            o_ref[b, :, h * MEM_HEAD_DIM:(h + 1) * MEM_HEAD_DIM] = o[h:h + 1]


def mem_attn_token(q, mk, mv, layer, bb=4):
    ng, _, d = q.shape
    assert ng % bb == 0
    row = pl.BlockSpec((bb, 1, d), lambda g: (g, 0, 0))
    mem = pl.BlockSpec((None, bb, MEM_TOKENS, MEM_HEADS, MEM_HEAD_DIM), lambda g: (layer, g, 0, 0, 0))
    return pl.pallas_call(
        functools.partial(_mem_attn_token_kernel, bb=bb),
        grid=(ng // bb,),
        in_specs=[row, mem, mem],
        out_specs=row,
        out_shape=jax.ShapeDtypeStruct((ng, 1, d), F32),
        compiler_params=_params("parallel"),
        name="mem_attn_token",
    )(q, mk, mv)


def _pair_state(s):
    nb = s.shape[0]
    s = s.reshape(nb, RW_PAIRS, 2, HEAD_DIM, HEAD_DIM)
    return jnp.swapaxes(s, 2, 3).reshape(nb, RW_PAIRS, HEAD_DIM, LANES)


def _unpair_state(s):
    nb = s.shape[0]
    s = s.reshape(nb, RW_PAIRS, HEAD_DIM, 2, HEAD_DIM)
    return jnp.swapaxes(s, 2, 3).reshape(nb, RW_HEADS, HEAD_DIM, HEAD_DIM)


def kernel(x_prompt, x_sample, mem_prompt, state_wkv, state_shift, cache_win_k, cache_win_v,
           cache_mem_k, cache_mem_v, ln_g, ln_b, ffn_w1, ffn_w3, ffn_w2, w_in, rw_mu, rw_w0,
           rw_w_up, rw_a0, rw_a_up, rw_g_up, rw_k_k, rw_k_a, rw_r_k, rw_gn_g, rw_gn_b, sw_sinks,
           w_out, mem_wq, mem_wk, mem_wv, mem_wo):
    depth = ln_g.shape[0]
    bp, tp, d = x_prompt.shape
    bs, ts, _ = x_sample.shape
    assert ts == 1 and cache_win_k.shape[2] == WINDOW and tp % WINDOW == 0

    n_ff = D_FF // FF_TILE
    up = lambda w: jnp.swapaxes(w.astype(BF16).reshape(depth, 2, d, n_ff, FF_TILE), 2, 3)
    w1b, w3b = up(ffn_w1), up(ffn_w3)
    w2b = ffn_w2.astype(BF16).reshape(depth, 2, n_ff, FF_TILE, d)
    w_in_b = w_in.astype(BF16)
    w_out_b = w_out.astype(BF16)
    wqb, wkb, wvb, wob = (w.astype(BF16) for w in (mem_wq, mem_wk, mem_wv, mem_wo))
    zpad = jnp.zeros((depth, D_W_LORA, RW_WIDTH), BF16)
    wup_b = jnp.concatenate([rw_w_up.astype(BF16), zpad], axis=1)
    aup_b = jnp.concatenate([zpad, rw_a_up.astype(BF16)], axis=1)
    gup_b = rw_g_up.astype(BF16)
    hid = jnp.arange(3 * RW_WIDTH) % RW_WIDTH // HEAD_DIM
    seg3 = (hid[:, None] == (jnp.arange(RW_WIDTH) // HEAD_DIM)[None, :]).astype(BF16)
    slopes = jnp.asarray([[_alibi_slope(h)] for h in range(SW_HEADS)], F32)
    q0, k0, v0 = RW_COLS, RW_COLS + SW_WIDTH, RW_COLS + SW_WIDTH + KV_WIDTH
    row = lambda a: a.reshape(1, -1)

    def layer(l, x, nb, t, prev_fn, s0, swa_fn, mem_fn):
        x = ffn_ln(x, w1b, w3b, w2b, (l, 0), row(ln_g[l, 0]), row(ln_b[l, 0]))
        p_rw, q, k, v = matmul_multi(
            x, [w_in_b[l, :, :q0], w_in_b[l, :, q0:k0], w_in_b[l, :, k0:v0], w_in_b[l, :, v0:]])
        *ops, g, bonus = rwkv_prep(
            p_rw, prev_fn(p_rw), row(rw_mu[l]), row(rw_w0[l]), wup_b[l], row(rw_a0[l]), aup_b[l],
            gup_b[l], row(rw_k_k[l]), row(rw_k_a[l]), row(rw_r_k[l]), seg3, seq_len=t)
        ops = [a.reshape(RW_PAIRS, nb, t, LANES) for a in ops]
        if t > 1:
            y, s_fin = wkv_chunked(*ops, s0)
        else:
            y, s_fin = wkv_step(*ops, s0)
        y_rw = rwkv_post(y.reshape(RW_PAIRS, nb * t, LANES), bonus, g, row(rw_gn_g[l]), row(rw_gn_b[l]),
                         seg3)
        y_sw, win_k, win_v = swa_fn(q, k, v)
        shift = x.reshape(nb, t, d)[:, -1]
        x = proj_ln(x, [y_rw, y_sw], [w_out_b[l, :RW_WIDTH], w_out_b[l, RW_WIDTH:]],
                    row(ln_g[l, 1]), row(ln_b[l, 1]))
        (qm,) = matmul_multi(x, [wqb[l]])
        o = mem_fn(qm)
        x = proj_ln(x, [o], [wob[l]], row(ln_g[l, 2]), row(ln_b[l, 2]))
        x = ffn_ln(x, w1b, w3b, w2b, (l, 1), row(ln_g[l, 3]), row(ln_b[l, 3]))
        return x, _unpair_state(s_fin), shift, win_k, win_v

    xp = x_prompt.reshape(bp * tp, d)
    p_wkv, p_shift, p_wk, p_wv, p_mk, p_mv = [], [], [], [], [], []
    for l in range(depth):
        prev_prompt = lambda p_rw: p_rw

        def swa_p(q, k, v, l=l):
            k3 = k.reshape(bp, tp, KV_WIDTH)
            v3 = v.reshape(bp, tp, KV_WIDTH)
            y = swa_prompt(q.reshape(bp, tp, SW_WIDTH), k3, v3, sw_sinks[l])
            tail = lambda a: a[:, -WINDOW:].reshape(bp, WINDOW, SW_KV_HEADS, HEAD_DIM)
            return y.reshape(bp * tp, SW_WIDTH), tail(k3), tail(v3)

        mk, mv = matmul_multi(mem_prompt.reshape(bp * MEM_TOKENS, d), [wkb[l], wvb[l]])
        mk = mk.reshape(bp, MEM_TOKENS, d)
        mv = mv.reshape(bp, MEM_TOKENS, d)

        def mem_p(qm, mk=mk, mv=mv):
            return mem_attn(qm.reshape(bp, tp, d), mk[None], mv[None], 0).reshape(bp * tp, d)

        s0 = jnp.zeros((bp, RW_PAIRS, HEAD_DIM, LANES), F32)
        xp, s_fin, shift, wk_, wv_ = layer(l, xp, bp, tp, prev_prompt, s0, swa_p, mem_p)
        p_wkv.append(s_fin)
        p_shift.append(shift)
        p_wk.append(wk_)
        p_wv.append(wv_)
        p_mk.append(mk.reshape(bp, MEM_TOKENS, MEM_HEADS, MEM_HEAD_DIM))
        p_mv.append(mv.reshape(bp, MEM_TOKENS, MEM_HEADS, MEM_HEAD_DIM))

    xs = x_sample.reshape(bs, d)
    s_wkv, s_shift, s_wk, s_wv = [], [], [], []
    for l in range(depth):
        def prev_sample(p_rw, l=l):
            (prev,) = matmul_multi(state_shift[l], [w_in_b[l, :, :q0]])
            return prev

        def swa_s(q, k, v, l=l):
            o, nk, nv = swa_sample(
                q.reshape(bs, SW_HEADS, HEAD_DIM), k.reshape(bs, 1, KV_WIDTH), v.reshape(bs, 1, KV_WIDTH),
                cache_win_k[l].reshape(bs, WINDOW, KV_WIDTH), cache_win_v[l].reshape(bs, WINDOW, KV_WIDTH),
                sw_sinks[l].reshape(SW_HEADS, 1), slopes)
            unflat = lambda a: a.reshape(bs, WINDOW, SW_KV_HEADS, HEAD_DIM)
            return o.reshape(bs, SW_WIDTH), unflat(nk), unflat(nv)

        def mem_s(qm, l=l):
            return mem_attn_token(qm.reshape(bs, 1, d), cache_mem_k, cache_mem_v, l).reshape(bs, d)

        xs, s_fin, shift, wk_, wv_ = layer(l, xs, bs, 1, prev_sample, _pair_state(state_wkv[l]),
                                           swa_s, mem_s)
        s_wkv.append(s_fin)
        s_shift.append(shift)
        s_wk.append(wk_)
        s_wv.append(wv_)

    return (xp.reshape(bp, tp, d), xs.reshape(bs, 1, d),
            jnp.stack(p_wkv), jnp.stack(p_shift), jnp.stack(p_wk), jnp.stack(p_wv),
            jnp.stack(p_mk), jnp.stack(p_mv),
            jnp.stack(s_wkv), jnp.stack(s_shift), jnp.stack(s_wk), jnp.stack(s_wv))
```

```python
import functools

import jax
import jax.numpy as jnp
from jax import lax
from jax.experimental import pallas as pl
from jax.experimental.pallas import tpu as pltpu

F32 = jnp.float32
BF16 = jnp.bfloat16

D_MODEL = 1024
HEAD_DIM = 64
RW_WIDTH = 512
RW_HEADS = 8
RW_PAIRS = RW_HEADS // 2
SW_WIDTH = 512
SW_HEADS = 8
SW_KV_HEADS = 2
SW_GROUP = SW_HEADS // SW_KV_HEADS
KV_WIDTH = SW_KV_HEADS * HEAD_DIM
WINDOW = 128
D_W_LORA = 64
D_A_LORA = 64
D_G_LORA = 128
RW_COLS = 3 * RW_WIDTH + D_W_LORA + D_A_LORA + D_G_LORA
LORA_WA_START = 3 * RW_WIDTH
LORA_G_START = LORA_WA_START + D_W_LORA + D_A_LORA
MEM_TOKENS = 256
MEM_HEADS = 4
MEM_HEAD_DIM = D_MODEL // MEM_HEADS
D_FF = 2816
DEPTH = 4
ALPHA = (2.0 * DEPTH) ** 0.25
LN_EPS = 1e-5
GN_EPS = 64e-5
NORM_EPS = 1e-12

CHUNK = 64
FF_TILE = 256
LANES = 128
VMEM_LIMIT = 56 * 1024 * 1024


def _params(*semantics):
    return pltpu.CompilerParams(dimension_semantics=semantics, vmem_limit_bytes=VMEM_LIMIT)


def _row_tile(n, want):
    return want if n % want == 0 else n


def _layer_norm(z, g, b):
    mu = jnp.mean(z, axis=-1, keepdims=True)
    d = z - mu
    var = jnp.mean(d * d, axis=-1, keepdims=True)
    return d * lax.rsqrt(var + LN_EPS) * g + b


def _sigmoid(x):
    return 1.0 / (1.0 + jnp.exp(-x))


def _split3(x):
    hi = x.astype(BF16)
    r1 = x - hi.astype(F32)
    mid = r1.astype(BF16)
    lo = (r1 - mid.astype(F32)).astype(BF16)
    return jnp.concatenate([hi, mid, lo], axis=1)


def _split2(x):
    hi = x.astype(BF16)
    lo = (x - hi.astype(F32)).astype(BF16)
    return jnp.concatenate([hi, lo], axis=1)


def _mm_kernel(x_ref, *refs, n_out):
    xb = x_ref[...].astype(BF16)
    for w_ref, o_ref in zip(refs[:n_out], refs[n_out:]):
        o_ref[...] = jnp.dot(xb, w_ref[...], preferred_element_type=F32)


def matmul_multi(x, ws, tm=512):
    n, k = x.shape
    tm = _row_tile(n, tm)
    n_out = len(ws)
    return pl.pallas_call(
        functools.partial(_mm_kernel, n_out=n_out),
        grid=(n // tm,),
        in_specs=[pl.BlockSpec((tm, k), lambda i: (i, 0))]
        + [pl.BlockSpec(w.shape, lambda i: (0, 0)) for w in ws],
        out_specs=[pl.BlockSpec((tm, w.shape[1]), lambda i: (i, 0)) for w in ws],
        out_shape=[jax.ShapeDtypeStruct((n, w.shape[1]), F32) for w in ws],
        compiler_params=_params("parallel"),
        name="matmul_multi",
    )(x, *ws)


def _ffn_kernel(x_ref, w1_ref, w3_ref, w2_ref, g_ref, b_ref, o_ref, xb_ref, acc_ref, *, n_ff):
    xb_ref[...] = x_ref[...].astype(BF16)

    def part(c):
        xb = xb_ref[...]
        h1 = jnp.dot(xb, w1_ref[c], preferred_element_type=F32)
        h3 = jnp.dot(xb, w3_ref[c], preferred_element_type=F32)
        h = (h1 * _sigmoid(h1)) * h3
        return jnp.dot(h.astype(BF16), w2_ref[c], preferred_element_type=F32)

    acc_ref[...] = part(0)

    def body(c, carry):
        acc_ref[...] += part(c)
        return carry

    lax.fori_loop(1, n_ff, body, 0)
    z = ALPHA * x_ref[...] + 0.5 * acc_ref[...]
    o_ref[...] = _layer_norm(z, g_ref[...], b_ref[...])


def ffn_ln(x, w1, w3, w2, lead, g, b, tm=1024):
    n, d = x.shape
    tm = _row_tile(n, tm)
    n_ff = D_FF // FF_TILE
    resident = lambda a: pl.BlockSpec((None,) * len(lead) + a.shape[len(lead):],
                                      lambda i: lead + (0, 0, 0), pipeline_mode=pl.Buffered(1))
    return pl.pallas_call(
        functools.partial(_ffn_kernel, n_ff=n_ff),
        grid=(n // tm,),
        in_specs=[
            pl.BlockSpec((tm, d), lambda i: (i, 0)),
            resident(w1), resident(w3), resident(w2),
            pl.BlockSpec((1, d), lambda i: (0, 0)),
            pl.BlockSpec((1, d), lambda i: (0, 0)),
        ],
        out_specs=pl.BlockSpec((tm, d), lambda i: (i, 0)),
        out_shape=jax.ShapeDtypeStruct((n, d), F32),
        scratch_shapes=[pltpu.VMEM((tm, d), BF16), pltpu.VMEM((tm, d), F32)],
        compiler_params=_params("parallel"),
        name="ffn_ln",
    )(x, w1, w3, w2, g, b)


def _proj_ln_kernel(x_ref, *refs, n_in):
    a_refs = refs[:n_in]
    w_refs = refs[n_in:2 * n_in]
    g_ref, b_ref, o_ref = refs[2 * n_in:]
    f = None
    for a_ref, w_ref in zip(a_refs, w_refs):
        t = jnp.dot(a_ref[...].astype(BF16), w_ref[...], preferred_element_type=F32)
        f = t if f is None else f + t
    o_ref[...] = _layer_norm(ALPHA * x_ref[...] + f, g_ref[...], b_ref[...])


def proj_ln(x, acts, ws, g, b, tm=512):
    n, d = x.shape
    tm = _row_tile(n, tm)
    n_in = len(acts)
    return pl.pallas_call(
        functools.partial(_proj_ln_kernel, n_in=n_in),
        grid=(n // tm,),
        in_specs=[pl.BlockSpec((tm, d), lambda i: (i, 0))]
        + [pl.BlockSpec((tm, a.shape[1]), lambda i: (i, 0)) for a in acts]
        + [pl.BlockSpec(w.shape, lambda i: (0, 0)) for w in ws]
        + [pl.BlockSpec((1, d), lambda i: (0, 0))] * 2,
        out_specs=pl.BlockSpec((tm, d), lambda i: (i, 0)),
        out_shape=jax.ShapeDtypeStruct((n, d), F32),
        compiler_params=_params("parallel"),
        name="proj_ln",
    )(x, *acts, *ws, g, b)


def _rwkv_prep_kernel(p_ref, prev_ref, mu_ref, w0_ref, wup_ref, a0_ref, aup_ref, gup_ref,
                      kk_ref, ka_ref, rk_ref, seg_ref, *refs, chunked, tiles_per_seq):
    if chunked:
        tril_ref, ones_ref = refs[:2]
        refs = refs[2:]
    g_out, bonus_out = refs[-2:]
    p = p_ref[...]
    if chunked:
        first = (pl.program_id(0) % tiles_per_seq) == 0
        above = jnp.where(first, 0.0, prev_ref[7:8, :])
        top = lax.broadcasted_iota(jnp.int32, p.shape, 0) == 0
        prev = jnp.where(top, above, pltpu.roll(p, 1, 0))
    else:
        prev = prev_ref[...]
    xm = p + (prev - p) * mu_ref[...]
    r = xm[:, 0:RW_WIDTH]
    k = xm[:, RW_WIDTH:2 * RW_WIDTH]
    v = xm[:, 2 * RW_WIDTH:3 * RW_WIDTH]
    wa = xm[:, LORA_WA_START:LORA_G_START]
    gl = xm[:, LORA_G_START:RW_COLS]

    def seg_sum(x):
        return jnp.dot(_split3(x), seg_ref[...], preferred_element_type=F32)

    lw = jnp.dot(jnp.tanh(wa).astype(BF16), wup_ref[...], preferred_element_type=F32)
    la = jnp.dot(wa.astype(BF16), aup_ref[...], preferred_element_type=F32)
    z = -(w0_ref[...] + lw)
    softplus = jnp.maximum(z, 0.0) + jnp.log(1.0 + jnp.exp(-jnp.abs(z)))
    w_log = -softplus - 0.5
    log_decay = -jnp.exp(w_log)
    a = _sigmoid(a0_ref[...] + la)
    g = jnp.dot(_sigmoid(gl).astype(BF16), gup_ref[...], preferred_element_type=F32)
    kk = k * kk_ref[...]
    nrm = jnp.sqrt(seg_sum(kk * kk))
    kk = kk / jnp.maximum(nrm, NORM_EPS)
    k_mod = k * (1.0 + (a - 1.0) * ka_ref[...])
    kka = kk * a
    if chunked:
        parts = _split3(log_decay)

        def time_sum(m_ref):
            s3 = jnp.dot(m_ref[...], parts, preferred_element_type=F32)
            return s3[:, :RW_WIDTH] + s3[:, RW_WIDTH:2 * RW_WIDTH] + s3[:, 2 * RW_WIDTH:]

        cum = time_sum(tril_ref)
        tot = time_sum(ones_ref)
        grow = jnp.exp(-cum)
        rest = jnp.exp(tot - cum)
        outs = (kk * jnp.exp(cum - log_decay), r * jnp.exp(cum), -kka * grow, k_mod * grow,
                -kka * rest, k_mod * rest, v, jnp.exp(tot))
    else:
        outs = (r, jnp.exp(log_decay), k_mod, v, kk, kka)
    for o_ref, val in zip(refs, outs):
        for pr in range(RW_PAIRS):
            o_ref[pr] = val[:, pr * LANES:(pr + 1) * LANES].astype(o_ref.dtype)
    g_out[...] = g
    bonus_out[...] = seg_sum(r * k_mod * rk_ref[...]) * v


def rwkv_prep(p, prev, mu, w0, wup, a0, aup, gup, k_k, k_a, r_k, seg, seq_len, tm=256):
    n = p.shape[0]
    chunked = seq_len > 1
    tm = _row_tile(n, tm)
    row = lambda c: pl.BlockSpec((tm, c), lambda i: (i, 0))
    prev_spec = row(RW_COLS)
    if chunked:
        assert seq_len % tm == 0
        prev_spec = pl.BlockSpec((8, RW_COLS), lambda i: (jnp.maximum(i * (tm // 8) - 1, 0), 0))
    full = lambda a: pl.BlockSpec(a.shape, lambda i: (0, 0))
    consts = (mu, w0, wup, a0, aup, gup, k_k, k_a, r_k, seg)
    dtypes = [F32] * 6
    if chunked:
        assert tm % CHUNK == 0
        t_idx = jnp.arange(tm)
        same = (t_idx[:, None] // CHUNK) == (t_idx[None, :] // CHUNK)
        consts += ((same & (t_idx[None, :] <= t_idx[:, None])).astype(BF16), same.astype(BF16))
        dtypes = [BF16] * 7 + [F32]
    pair = pl.BlockSpec((RW_PAIRS, tm, LANES), lambda i: (0, i, 0))
    return pl.pallas_call(
        functools.partial(_rwkv_prep_kernel, chunked=chunked, tiles_per_seq=max(seq_len // tm, 1)),
        grid=(n // tm,),
        in_specs=[row(RW_COLS), prev_spec] + [full(c) for c in consts],
        out_specs=[pair] * len(dtypes) + [row(RW_WIDTH)] * 2,
        out_shape=[jax.ShapeDtypeStruct((RW_PAIRS, n, LANES), dt) for dt in dtypes]
        + [jax.ShapeDtypeStruct((n, RW_WIDTH), F32)] * 2,
        compiler_params=_params("parallel"),
        name="rwkv_prep",
    )(p, prev, *consts)


def _wkv_step_kernel(r_ref, w_ref, k_ref, v_ref, kk_ref, kka_ref, s0_ref, y_ref, s_ref, *, bb):
    sub = lax.broadcasted_iota(jnp.int32, (HEAD_DIM, LANES), 0)
    lane = lax.broadcasted_iota(jnp.int32, (HEAD_DIM, LANES), 1)
    diag = (lane % HEAD_DIM) == sub
    rj = lax.broadcasted_iota(jnp.int32, (2 * LANES, LANES), 0)
    cl = lax.broadcasted_iota(jnp.int32, (2 * LANES, LANES), 1)
    ones2 = jnp.where(((rj % LANES) // HEAD_DIM) == (cl // HEAD_DIM), 1.0, 0.0).astype(BF16)

    def head_sums(xs):
        out = jnp.dot(_split2(jnp.concatenate(xs, axis=0)), ones2, preferred_element_type=F32)
        return [out[i * HEAD_DIM:(i + 1) * HEAD_DIM] for i in range(len(xs))]

    pairs = [(b, p) for b in range(bb) for p in range(RW_PAIRS)]
    n = len(pairs)
    row = lambda ref, i: ref[pairs[i][1], pairs[i][0]]
    tiles = lambda x: x.reshape(HEAD_DIM // 8, 8, LANES)
    flat = lambda x: x.reshape(HEAD_DIM, LANES)
    s = [tiles(s0_ref[b, p]) for b, p in pairs]
    red = head_sums([flat(s[i] * row(kk_ref, i)) for i in range(n)]
                    + [flat(jnp.where(tiles(diag), row(v_ref, i), 0.0)) for i in range(n)])
    s = [s[i] * row(w_ref, i) - tiles(red[i]) * row(kka_ref, i) + tiles(red[n + i]) * row(k_ref, i)
         for i in range(n)]
    yb = head_sums([flat(s[i] * row(r_ref, i)) for i in range(n)])
    for i, (b, p) in enumerate(pairs):
        s_ref[b, p] = flat(s[i])
        y_ref[p, b] = jnp.sum(jnp.where(diag, yb[i], 0.0), axis=0, keepdims=True)


def wkv_step(r, w, k, v, kk, kka, s0, bb=8):
    _, nb, t, _ = r.shape
    assert t == 1 and nb % bb == 0
    seq = pl.BlockSpec((RW_PAIRS, bb, 1, LANES), lambda i: (0, i, 0, 0))
    st = pl.BlockSpec((bb, RW_PAIRS, HEAD_DIM, LANES), lambda i: (i, 0, 0, 0))
    return pl.pallas_call(
        functools.partial(_wkv_step_kernel, bb=bb),
        grid=(nb // bb,),
        in_specs=[seq] * 6 + [st],
        out_specs=[seq, st],
        out_shape=[jax.ShapeDtypeStruct((RW_PAIRS, nb, 1, LANES), F32),
                   jax.ShapeDtypeStruct(s0.shape, F32)],
        compiler_params=_params("parallel"),
        name="wkv_step",
    )(r, w, k, v, kk, kka, s0)


def _dot_nt(a, b):
    return lax.dot_general(a, b, (((1,), (1,)), ((), ())), preferred_element_type=F32)


def _dot_tn(a, b):
    return lax.dot_general(a, b, (((0,), (0,)), ((), ())), preferred_element_type=F32)


def _dot(a, b):
    return jnp.dot(a, b, preferred_element_type=F32)


def _wkv_chunk_kernel(qt_ref, rt_ref, bh_ref, kh_ref, bb_ref, kb_ref, v_ref, gc_ref, s0_ref,
                      y_ref, s_ref, *, nb, tc):
    @pl.when(pl.program_id(1) == 0)
    def _():
        s_ref[...] = s0_ref[...]

    c = CHUNK
    assert c == HEAD_DIM and 2 * c == LANES
    row = lax.broadcasted_iota(jnp.int32, (c, LANES), 0)
    lane = lax.broadcasted_iota(jnp.int32, (c, LANES), 1)
    head_a = lane < HEAD_DIM
    head_a2 = lax.broadcasted_iota(jnp.int32, (2 * c, LANES), 1) < HEAD_DIM
    m_mask = (lane >= c) & (lane - c < row)
    incl = (lane % c) <= row
    r64 = lax.broadcasted_iota(jnp.int32, (c, c), 0)
    c64 = lax.broadcasted_iota(jnp.int32, (c, c), 1)
    strict = c64 < r64
    eye = jnp.where(c64 == r64, 1.0, 0.0)
    pairs = [(b, p) for b in range(nb) for p in range(RW_PAIRS)]
    bf = lambda x: x.astype(BF16)
    stack = lambda x, y: jnp.concatenate([x, y], axis=0)

    def chunk(ci, carry):
        t0 = pl.multiple_of(ci * c, c)
        ld = lambda ref, b, p: ref[p, b, pl.ds(t0, c), :]
        qr = [stack(ld(qt_ref, b, p), ld(rt_ref, b, p)) for b, p in pairs]
        bk = [stack(ld(bh_ref, b, p), ld(kh_ref, b, p)) for b, p in pairs]
        zero = jnp.zeros((), BF16)
        e = [_dot_nt(q, stack(jnp.where(head_a2, x, zero), jnp.where(head_a2, zero, x)))
             for q, x in zip(qr, bk)]

        pw = [jnp.where(strict, x[:c, h * LANES:h * LANES + c], 0.0) for x in e for h in range(2)]
        inv = [eye + n for n in pw]
        for _ in range(5):
            pb = [bf(x) for x in pw]
            pw = [_dot(x, x) for x in pb]
            inv = [t + _dot(bf(x), bf(t)) for x, t in zip(pw, inv)]

        s = [s_ref[b, p] for b, p in pairs]
        ff = [_dot_nt(q, bf(stack(jnp.where(head_a, x, 0.0), jnp.where(head_a, 0.0, x))))
              for q, x in zip(qr, s)]
        v = [ld(v_ref, b, p) for b, p in pairs]
        vv = [stack(x, x) for x in v]
        heads = lambda fn: jnp.where(head_a, fn(0), fn(1))
        g = [f[:c] + heads(lambda h: _dot(bf(jnp.where(m_mask, x[:c, h * LANES:(h + 1) * LANES], 0.0)), w))
             for f, x, w in zip(ff, e, vv)]
        u = [heads(lambda h: _dot(bf(inv[2 * i + h]), bf(x))) for i, x in enumerate(g)]
        uv = [stack(bf(x), w) for x, w in zip(u, v)]
        y = [f[c:] + heads(lambda h: _dot(bf(jnp.where(incl, x[c:, h * LANES:(h + 1) * LANES], 0.0)), w))
             for f, x, w in zip(ff, e, uv)]
        add = [_dot_tn(w, stack(ld(bb_ref, b, p), ld(kb_ref, b, p))) for w, (b, p) in zip(uv, pairs)]
        for i, (b, p) in enumerate(pairs):
            y_ref[p, b, pl.ds(t0, c), :] = y[i]
            gc = gc_ref[p, b, pl.ds(t0, 8), :]
            kept = (s[i].reshape(c // 8, 8, LANES) * gc).reshape(c, LANES)
            s_ref[b, p] = kept + jnp.where(head_a, add[i][:c], add[i][c:])
        return carry

    lax.fori_loop(0, tc // c, chunk, 0)


def wkv_chunked(qt, rt, bh, kh, bb, kb, v, gc, s0, tc=256):
    _, nb, t, _ = qt.shape
    seq = pl.BlockSpec((RW_PAIRS, nb, tc, LANES), lambda i, j: (0, 0, j, 0))
    st = pl.BlockSpec((nb, RW_PAIRS, HEAD_DIM, LANES), lambda i, j: (0, 0, 0, 0))
    return pl.pallas_call(
        functools.partial(_wkv_chunk_kernel, nb=nb, tc=tc),
        grid=(1, t // tc),
        in_specs=[seq] * 8 + [st],
        out_specs=[seq, st],
        out_shape=[jax.ShapeDtypeStruct((RW_PAIRS, nb, t, LANES), F32),
                   jax.ShapeDtypeStruct(s0.shape, F32)],
        compiler_params=_params("arbitrary", "arbitrary"),
        name="wkv_chunked",
    )(qt, rt, bh, kh, bb, kb, v, gc, s0)


def _rwkv_post_kernel(y_ref, bonus_ref, g_ref, gng_ref, gnb_ref, seg_ref, o_ref):
    def seg_mean(x):
        return jnp.dot(_split3(x), seg_ref[...], preferred_element_type=F32) * (1.0 / HEAD_DIM)

    y = jnp.concatenate([y_ref[pr] for pr in range(RW_PAIRS)], axis=1)
    d = y - seg_mean(y)
    var = seg_mean(d * d)
    yn = d * lax.rsqrt(var + GN_EPS) * gng_ref[...] + gnb_ref[...]
    o_ref[...] = (yn + bonus_ref[...]) * g_ref[...]


def rwkv_post(y, bonus, g, gn_g, gn_b, seg, tm=512):
    n = y.shape[1]
    tm = _row_tile(n, tm)
    row = pl.BlockSpec((tm, RW_WIDTH), lambda i: (i, 0))
    full = lambda a: pl.BlockSpec(a.shape, lambda i: (0, 0))
    return pl.pallas_call(
        _rwkv_post_kernel,
        grid=(n // tm,),
        in_specs=[pl.BlockSpec((RW_PAIRS, tm, LANES), lambda i: (0, i, 0)), row, row,
                  full(gn_g), full(gn_b), full(seg)],
        out_specs=row,
        out_shape=jax.ShapeDtypeStruct((n, RW_WIDTH), F32),
        compiler_params=_params("parallel"),
        name="rwkv_post",
    )(y, bonus, g, gn_g, gn_b, seg)


def _alibi_slope(h):
    return 2.0 ** (-8.0 * (h + 1) / SW_HEADS)


def _swa_prompt_kernel(sink_ref, bias_ref, q_ref, kp_ref, kc_ref, vp_ref, vc_ref, o_ref):
    qb = (q_ref[0] * HEAD_DIM ** -0.5).astype(BF16)
    kw = jnp.concatenate([kp_ref[0], kc_ref[0]], axis=0).astype(BF16)
    vw = jnp.concatenate([vp_ref[0], vc_ref[0]], axis=0).astype(BF16)
    kx = pltpu.roll(kw, HEAD_DIM, 1)
    vx = pltpu.roll(vw, HEAD_DIM, 1)
    low_kv = lax.broadcasted_iota(jnp.int32, kw.shape, 1) < HEAD_DIM
    low_q = lax.broadcasted_iota(jnp.int32, (WINDOW, LANES), 1) < HEAD_DIM
    zero = jnp.zeros((), BF16)
    heads = range(SW_HEADS)
    kc = [jnp.where(low_kv, kw, kx), jnp.where(low_kv, kx, kw)]
    vc = [jnp.where(low_kv, vw, vx), jnp.where(low_kv, vx, vw)]
    q2 = [qb[:, p * LANES:(p + 1) * LANES] for p in range(SW_HEADS // 2)]
    qh = [jnp.where(low_q, q2[h // 2], zero) if h % 2 == 0 else jnp.where(low_q, zero, q2[h // 2]) for h in heads]
    logits = [_dot_nt(qh[h], kc[h // SW_GROUP]) + bias_ref[h] for h in heads]
    m = [jnp.maximum(jnp.max(logits[h], axis=-1, keepdims=True), sink_ref[h]) for h in heads]
    e = [jnp.exp(logits[h] - m[h]) for h in heads]
    den = [jnp.sum(e[h], axis=-1, keepdims=True) + jnp.exp(sink_ref[h] - m[h]) for h in heads]
    o = [_dot(e[h].astype(BF16), vc[h // SW_GROUP]) / den[h] for h in heads]
    for p in range(SW_HEADS // 2):
        o_ref[0, :, p * LANES:(p + 1) * LANES] = jnp.where(low_q, o[2 * p], o[2 * p + 1])


def swa_prompt(q, k, v, sinks):
    nb, t, _ = q.shape
    i = jnp.arange(WINDOW)[:, None]
    j = jnp.arange(2 * WINDOW)[None, :]
    dist = WINDOW + i - j
    valid = (dist >= 0) & (dist < WINDOW)
    slopes = jnp.asarray([_alibi_slope(h) for h in range(SW_HEADS)], F32)[:, None, None]
    table = lambda ok: jnp.where(ok[None], -slopes * dist.astype(F32)[None], -jnp.inf)
    bias = jnp.stack([table(valid & (j >= WINDOW)), table(valid)])
    cur = lambda w: pl.BlockSpec((1, WINDOW, w), lambda b, n: (b, n, 0))
    prv = lambda w: pl.BlockSpec((1, WINDOW, w), lambda b, n: (b, jnp.maximum(n - 1, 0), 0))
    return pl.pallas_call(
        _swa_prompt_kernel,
        grid=(nb, t // WINDOW),
        in_specs=[pl.BlockSpec(memory_space=pltpu.SMEM),
                  pl.BlockSpec((None, SW_HEADS, WINDOW, 2 * WINDOW), lambda b, n: (jnp.minimum(n, 1), 0, 0, 0)),
                  cur(SW_WIDTH), prv(KV_WIDTH), cur(KV_WIDTH), prv(KV_WIDTH), cur(KV_WIDTH)],
        out_specs=cur(SW_WIDTH),
        out_shape=jax.ShapeDtypeStruct((nb, t, SW_WIDTH), F32),
        compiler_params=_params("parallel", "parallel"),
        name="swa_prompt",
    )(sinks, bias, q, k, k, v, v)


def _swa_sample_kernel(sink_ref, slope_ref, q_ref, kn_ref, vn_ref, ck_ref, cv_ref,
                       o_ref, nk_ref, nv_ref, *, bb):
    last = lax.broadcasted_iota(jnp.int32, (WINDOW, KV_WIDTH), 0) == WINDOW - 1
    j = lax.broadcasted_iota(jnp.int32, (SW_HEADS, WINDOW), 1)
    bias = slope_ref[...] * (WINDOW - 1 - j).astype(F32)
    sink = sink_ref[...]
    seqs = range(bb)
    kw = [jnp.where(last, kn_ref[b], pltpu.roll(ck_ref[b], WINDOW - 1, 0)) for b in seqs]
    vw = [jnp.where(last, vn_ref[b], pltpu.roll(cv_ref[b], WINDOW - 1, 0)) for b in seqs]
    s = [_dot_nt(q_ref[b].astype(BF16), kw[b].astype(BF16)) * HEAD_DIM ** -0.5 - bias for b in seqs]
    m = [jnp.maximum(jnp.max(x, axis=-1, keepdims=True), sink) for x in s]
    e = [jnp.exp(x - mx) for x, mx in zip(s, m)]
    prob = [x / (jnp.sum(x, axis=-1, keepdims=True) + jnp.exp(sink - mx)) for x, mx in zip(e, m)]
    o = [_dot(prob[b].astype(BF16), vw[b].astype(BF16)) for b in seqs]
    for b in seqs:
        nk_ref[b] = kw[b]
        nv_ref[b] = vw[b]
        o_ref[b] = o[b]


def swa_sample(q, kn, vn, ck, cv, sinks, slopes, bb=8):
    nb = q.shape[0]
    assert nb % bb == 0
    own = (jnp.arange(SW_HEADS) // SW_GROUP)[:, None] == jnp.arange(SW_KV_HEADS)[None, :]
    q2 = jnp.where(own[None, :, :, None], q[:, :, None, :], 0.0).reshape(nb, SW_HEADS, KV_WIDTH)
    per_b = lambda s: pl.BlockSpec((bb,) + s, lambda b: (b, 0, 0))
    full = lambda a: pl.BlockSpec(a.shape, lambda b: (0, 0))
    o2, nk, nv = pl.pallas_call(
        functools.partial(_swa_sample_kernel, bb=bb),
        grid=(nb // bb,),
        in_specs=[full(sinks), full(slopes), per_b((SW_HEADS, KV_WIDTH)), per_b((1, KV_WIDTH)),
                  per_b((1, KV_WIDTH)), per_b((WINDOW, KV_WIDTH)), per_b((WINDOW, KV_WIDTH))],
        out_specs=[per_b((SW_HEADS, KV_WIDTH)), per_b((WINDOW, KV_WIDTH)), per_b((WINDOW, KV_WIDTH))],
        out_shape=[jax.ShapeDtypeStruct((nb, SW_HEADS, KV_WIDTH), F32),
                   jax.ShapeDtypeStruct((nb, WINDOW, KV_WIDTH), F32),
                   jax.ShapeDtypeStruct((nb, WINDOW, KV_WIDTH), F32)],
        compiler_params=_params("parallel"),
        name="swa_sample",
    )(sinks, slopes, q2, kn, vn, ck, cv)
    o = jnp.sum(jnp.where(own[None, :, :, None], o2.reshape(nb, SW_HEADS, SW_KV_HEADS, HEAD_DIM), 0.0), axis=2)
    return o, nk, nv


def _mem_block_kernel(x_ref, wq_ref, mk_ref, mv_ref, wo_ref, g_ref, b_ref, o_ref):
    x = x_ref[0]
    qb = _dot(x.astype(BF16), wq_ref[...]).astype(BF16)
    outs = []
    for h in range(MEM_HEADS):
        cols = slice(h * MEM_HEAD_DIM, (h + 1) * MEM_HEAD_DIM)
        s = _dot_nt(qb[:, cols], mk_ref[0, :, cols].astype(BF16)) * MEM_HEAD_DIM ** -0.5
        e = jnp.exp(s - jnp.max(s, axis=-1, keepdims=True))
        den = jnp.sum(e, axis=-1, keepdims=True)
        outs.append(_dot(e.astype(BF16), mv_ref[0, :, cols].astype(BF16)) / den)
    o = jnp.concatenate(outs, axis=1).astype(BF16)
    o_ref[0] = _layer_norm(ALPHA * x + _dot(o, wo_ref[...]), g_ref[...], b_ref[...])


def mem_block(x, wq, mk, mv, wo, g, b, tm=512):
    ng, t, d = x.shape
    tm = _row_tile(t, tm)
    row = pl.BlockSpec((1, tm, d), lambda gi, i: (gi, i, 0))
    mem = pl.BlockSpec((1, MEM_TOKENS, d), lambda gi, i: (gi, 0, 0))
    full = lambda a: pl.BlockSpec(a.shape, lambda gi, i: (0, 0))
    return pl.pallas_call(
        _mem_block_kernel,
        grid=(ng, t // tm),
        in_specs=[row, full(wq), mem, mem, full(wo), full(g), full(b)],
        out_specs=row,
        out_shape=jax.ShapeDtypeStruct((ng, t, d), F32),
        compiler_params=_params("parallel", "parallel"),
        name="mem_block",
    )(x, wq, mk, mv, wo, g, b)


def _mem_attn_token_kernel(q_ref, mk_ref, mv_ref, o_ref, *, bb):
    rows = MEM_TOKENS * MEM_HEADS
    lane = lax.broadcasted_iota(jnp.int32, (MEM_HEADS, rows), 1)
    head = lax.broadcasted_iota(jnp.int32, (MEM_HEADS, rows), 0)
    own = (lane % MEM_HEADS) == head
    for b in range(bb):
        q = q_ref[b]
        q4 = jnp.concatenate([q[:, h * MEM_HEAD_DIM:(h + 1) * MEM_HEAD_DIM] for h in range(MEM_HEADS)],
                             axis=0).astype(BF16)
        k2 = mk_ref[b].reshape(rows, MEM_HEAD_DIM).astype(BF16)
        v2 = mv_ref[b].reshape(rows, MEM_HEAD_DIM).astype(BF16)
        s = jnp.where(own, _dot_nt(q4, k2) * MEM_HEAD_DIM ** -0.5, -jnp.inf)
        m = jnp.max(s, axis=-1, keepdims=True)
        e = jnp.exp(s - m)
        prob = e / jnp.sum(e, axis=-1, keepdims=True)
        o = _dot(prob.astype(BF16), v2)
        for h in range(MEM_HEADS):
            o_ref[b, :, h * MEM_HEAD_DIM:(h + 1) * MEM_HEAD_DIM] = o[h:h + 1]


def mem_attn_token(q, mk, mv, layer, bb=4):
    ng, _, d = q.shape
    assert ng % bb == 0
    row = pl.BlockSpec((bb, 1, d), lambda g: (g, 0, 0))
    mem = pl.BlockSpec((None, bb, MEM_TOKENS, MEM_HEADS, MEM_HEAD_DIM), lambda g: (layer, g, 0, 0, 0))
    return pl.pallas_call(
        functools.partial(_mem_attn_token_kernel, bb=bb),
        grid=(ng // bb,),
        in_specs=[row, mem, mem],
        out_specs=row,
        out_shape=jax.ShapeDtypeStruct((ng, 1, d), F32),
        compiler_params=_params("parallel"),
        name="mem_attn_token",
    )(q, mk, mv)


def _pair_state(s):
    nb = s.shape[0]
    s = s.reshape(nb, RW_PAIRS, 2, HEAD_DIM, HEAD_DIM)
    return jnp.swapaxes(s, 2, 3).reshape(nb, RW_PAIRS, HEAD_DIM, LANES)


def _unpair_state(s):
    nb = s.shape[0]
    s = s.reshape(nb, RW_PAIRS, HEAD_DIM, 2, HEAD_DIM)
    return jnp.swapaxes(s, 2, 3).reshape(nb, RW_HEADS, HEAD_DIM, HEAD_DIM)


def kernel(x_prompt, x_sample, mem_prompt, state_wkv, state_shift, cache_win_k, cache_win_v,
           cache_mem_k, cache_mem_v, ln_g, ln_b, ffn_w1, ffn_w3, ffn_w2, w_in, rw_mu, rw_w0,
           rw_w_up, rw_a0, rw_a_up, rw_g_up, rw_k_k, rw_k_a, rw_r_k, rw_gn_g, rw_gn_b, sw_sinks,
           w_out, mem_wq, mem_wk, mem_wv, mem_wo):
    depth = ln_g.shape[0]
    bp, tp, d = x_prompt.shape
    bs, ts, _ = x_sample.shape
    assert ts == 1 and cache_win_k.shape[2] == WINDOW and tp % WINDOW == 0

    n_ff = D_FF // FF_TILE
    up = lambda w: jnp.swapaxes(w.astype(BF16).reshape(depth, 2, d, n_ff, FF_TILE), 2, 3)
    w1b, w3b = up(ffn_w1), up(ffn_w3)
    w2b = ffn_w2.astype(BF16).reshape(depth, 2, n_ff, FF_TILE, d)
    w_in_b = w_in.astype(BF16)
    w_out_b = w_out.astype(BF16)
    wqb, wkb, wvb, wob = (w.astype(BF16) for w in (mem_wq, mem_wk, mem_wv, mem_wo))
    zpad = jnp.zeros((depth, D_W_LORA, RW_WIDTH), BF16)
    wup_b = jnp.concatenate([rw_w_up.astype(BF16), zpad], axis=1)
    aup_b = jnp.concatenate([zpad, rw_a_up.astype(BF16)], axis=1)
    gup_b = rw_g_up.astype(BF16)
    hid = jnp.arange(3 * RW_WIDTH) % RW_WIDTH // HEAD_DIM
    seg3 = (hid[:, None] == (jnp.arange(RW_WIDTH) // HEAD_DIM)[None, :]).astype(BF16)
    slopes = jnp.asarray([[_alibi_slope(h)] for h in range(SW_HEADS)], F32)
    q0, k0, v0 = RW_COLS, RW_COLS + SW_WIDTH, RW_COLS + SW_WIDTH + KV_WIDTH
    row = lambda a: a.reshape(1, -1)

    def layer(l, x, nb, t, prev_fn, s0, swa_fn, mem_fn):
        x = ffn_ln(x, w1b, w3b, w2b, (l, 0), row(ln_g[l, 0]), row(ln_b[l, 0]))
        p_rw, q, k, v = matmul_multi(
            x, [w_in_b[l, :, :q0], w_in_b[l, :, q0:k0], w_in_b[l, :, k0:v0], w_in_b[l, :, v0:]])
        *ops, g, bonus = rwkv_prep(
            p_rw, prev_fn(p_rw), row(rw_mu[l]), row(rw_w0[l]), wup_b[l], row(rw_a0[l]), aup_b[l],
            gup_b[l], row(rw_k_k[l]), row(rw_k_a[l]), row(rw_r_k[l]), seg3, seq_len=t)
        ops = [a.reshape(RW_PAIRS, nb, t, LANES) for a in ops]
        if t > 1:
            y, s_fin = wkv_chunked(*ops, s0)
        else:
            y, s_fin = wkv_step(*ops, s0)
        y_rw = rwkv_post(y.reshape(RW_PAIRS, nb * t, LANES), bonus, g, row(rw_gn_g[l]), row(rw_gn_b[l]),
                         seg3)
        y_sw, win_k, win_v = swa_fn(q, k, v)
        shift = x.reshape(nb, t, d)[:, -1]
        x = proj_ln(x, [y_rw, y_sw], [w_out_b[l, :RW_WIDTH], w_out_b[l, RW_WIDTH:]],
                    row(ln_g[l, 1]), row(ln_b[l, 1]))
        x = mem_fn(x)
        x = ffn_ln(x, w1b, w3b, w2b, (l, 1), row(ln_g[l, 3]), row(ln_b[l, 3]))
        return x, _unpair_state(s_fin), shift, win_k, win_v

    xp = x_prompt.reshape(bp * tp, d)
    p_wkv, p_shift, p_wk, p_wv, p_mk, p_mv = [], [], [], [], [], []
    for l in range(depth):
        prev_prompt = lambda p_rw: p_rw

        def swa_p(q, k, v, l=l):
            k3 = k.reshape(bp, tp, KV_WIDTH)
            v3 = v.reshape(bp, tp, KV_WIDTH)
            y = swa_prompt(q.reshape(bp, tp, SW_WIDTH), k3, v3, sw_sinks[l])
            tail = lambda a: a[:, -WINDOW:].reshape(bp, WINDOW, SW_KV_HEADS, HEAD_DIM)
            return y.reshape(bp * tp, SW_WIDTH), tail(k3), tail(v3)

        mk, mv = matmul_multi(mem_prompt.reshape(bp * MEM_TOKENS, d), [wkb[l], wvb[l]])
        mk = mk.reshape(bp, MEM_TOKENS, d)
        mv = mv.reshape(bp, MEM_TOKENS, d)

        def mem_p(x, l=l, mk=mk, mv=mv):
            return mem_block(x.reshape(bp, tp, d), wqb[l], mk, mv, wob[l], row(ln_g[l, 2]),
                             row(ln_b[l, 2])).reshape(bp * tp, d)

        s0 = jnp.zeros((bp, RW_PAIRS, HEAD_DIM, LANES), F32)
        xp, s_fin, shift, wk_, wv_ = layer(l, xp, bp, tp, prev_prompt, s0, swa_p, mem_p)
        p_wkv.append(s_fin)
        p_shift.append(shift)
        p_wk.append(wk_)
        p_wv.append(wv_)
        p_mk.append(mk.reshape(bp, MEM_TOKENS, MEM_HEADS, MEM_HEAD_DIM))
        p_mv.append(mv.reshape(bp, MEM_TOKENS, MEM_HEADS, MEM_HEAD_DIM))

    xs = x_sample.reshape(bs, d)
    s_wkv, s_shift, s_wk, s_wv = [], [], [], []
    for l in range(depth):
        def prev_sample(p_rw, l=l):
            (prev,) = matmul_multi(state_shift[l], [w_in_b[l, :, :q0]])
            return prev

        def swa_s(q, k, v, l=l):
            o, nk, nv = swa_sample(
                q.reshape(bs, SW_HEADS, HEAD_DIM), k.reshape(bs, 1, KV_WIDTH), v.reshape(bs, 1, KV_WIDTH),
                cache_win_k[l].reshape(bs, WINDOW, KV_WIDTH), cache_win_v[l].reshape(bs, WINDOW, KV_WIDTH),
                sw_sinks[l].reshape(SW_HEADS, 1), slopes)
            unflat = lambda a: a.reshape(bs, WINDOW, SW_KV_HEADS, HEAD_DIM)
            return o.reshape(bs, SW_WIDTH), unflat(nk), unflat(nv)

        def mem_s(x, l=l):
            (qm,) = matmul_multi(x, [wqb[l]])
            o = mem_attn_token(qm.reshape(bs, 1, d), cache_mem_k, cache_mem_v, l).reshape(bs, d)
            return proj_ln(x, [o], [wob[l]], row(ln_g[l, 2]), row(ln_b[l, 2]))

        xs, s_fin, shift, wk_, wv_ = layer(l, xs, bs, 1, prev_sample, _pair_state(state_wkv[l]),
                                           swa_s, mem_s)
        s_wkv.append(s_fin)
        s_shift.append(shift)
        s_wk.append(wk_)
        s_wv.append(wv_)

    return (xp.reshape(bp, tp, d), xs.reshape(bs, 1, d),
            jnp.stack(p_wkv), jnp.stack(p_shift), jnp.stack(p_wk), jnp.stack(p_wv),
            jnp.stack(p_mk), jnp.stack(p_mv),
            jnp.stack(s_wkv), jnp.stack(s_shift), jnp.stack(s_wk), jnp.stack(s_wv))
```

```python
import functools

import jax
import jax.numpy as jnp
from jax import lax
from jax.experimental import pallas as pl
from jax.experimental.pallas import tpu as pltpu

F32 = jnp.float32
BF16 = jnp.bfloat16

D_MODEL = 1024
HEAD_DIM = 64
RW_WIDTH = 512
RW_HEADS = 8
RW_PAIRS = RW_HEADS // 2
SW_WIDTH = 512
SW_HEADS = 8
SW_KV_HEADS = 2
SW_GROUP = SW_HEADS // SW_KV_HEADS
KV_WIDTH = SW_KV_HEADS * HEAD_DIM
WINDOW = 128
D_W_LORA = 64
D_A_LORA = 64
D_G_LORA = 128
RW_COLS = 3 * RW_WIDTH + D_W_LORA + D_A_LORA + D_G_LORA
LORA_WA_START = 3 * RW_WIDTH
LORA_G_START = LORA_WA_START + D_W_LORA + D_A_LORA
MEM_TOKENS = 256
MEM_HEADS = 4
MEM_HEAD_DIM = D_MODEL // MEM_HEADS
D_FF = 2816
DEPTH = 4
ALPHA = (2.0 * DEPTH) ** 0.25
LN_EPS = 1e-5
GN_EPS = 64e-5
NORM_EPS = 1e-12

CHUNK = 64
FF_TILE = 256
LANES = 128
VMEM_LIMIT = 56 * 1024 * 1024


def _params(*semantics):
    return pltpu.CompilerParams(dimension_semantics=semantics, vmem_limit_bytes=VMEM_LIMIT)


def _row_tile(n, want):
    return want if n % want == 0 else n


def _layer_norm(z, g, b):
    mu = jnp.mean(z, axis=-1, keepdims=True)
    d = z - mu
    var = jnp.mean(d * d, axis=-1, keepdims=True)
    return d * lax.rsqrt(var + LN_EPS) * g + b


def _sigmoid(x):
    return 1.0 / (1.0 + jnp.exp(-x))


def _split3(x):
    hi = x.astype(BF16)
    r1 = x - hi.astype(F32)
    mid = r1.astype(BF16)
    lo = (r1 - mid.astype(F32)).astype(BF16)
    return jnp.concatenate([hi, mid, lo], axis=1)


def _split2(x):
    hi = x.astype(BF16)
    lo = (x - hi.astype(F32)).astype(BF16)
    return jnp.concatenate([hi, lo], axis=1)


def _mm_kernel(x_ref, *refs, n_out):
    xb = x_ref[...].astype(BF16)
    for w_ref, o_ref in zip(refs[:n_out], refs[n_out:]):
        o_ref[...] = jnp.dot(xb, w_ref[...], preferred_element_type=F32)


def matmul_multi(x, ws, tm=512):
    n, k = x.shape
    tm = _row_tile(n, tm)
    n_out = len(ws)
    return pl.pallas_call(
        functools.partial(_mm_kernel, n_out=n_out),
        grid=(n // tm,),
        in_specs=[pl.BlockSpec((tm, k), lambda i: (i, 0))]
        + [pl.BlockSpec(w.shape, lambda i: (0, 0)) for w in ws],
        out_specs=[pl.BlockSpec((tm, w.shape[1]), lambda i: (i, 0)) for w in ws],
        out_shape=[jax.ShapeDtypeStruct((n, w.shape[1]), F32) for w in ws],
        compiler_params=_params("parallel"),
        name="matmul_multi",
    )(x, *ws)


def _ffn_kernel(x_ref, w1_ref, w3_ref, w2_ref, g_ref, b_ref, o_ref, xb_ref, acc_ref, *, n_ff):
    xb_ref[...] = x_ref[...].astype(BF16)

    def part(c):
        cols = slice(c * FF_TILE, (c + 1) * FF_TILE)
        xb = xb_ref[...]
        h1 = jnp.dot(xb, w1_ref[:, cols], preferred_element_type=F32)
        h3 = jnp.dot(xb, w3_ref[:, cols], preferred_element_type=F32)
        h = (h1 * _sigmoid(h1)) * h3
        return jnp.dot(h.astype(BF16), w2_ref[cols, :], preferred_element_type=F32)

    acc_ref[...] = part(0)
    for c in range(1, n_ff):
        acc_ref[...] += part(c)
    z = ALPHA * x_ref[...] + 0.5 * acc_ref[...]
    o_ref[...] = _layer_norm(z, g_ref[...], b_ref[...])


def ffn_ln(x, w1, w3, w2, lead, g, b, tm=1024):
    n, d = x.shape
    tm = _row_tile(n, tm)
    n_ff = D_FF // FF_TILE
    resident = lambda a: pl.BlockSpec((None,) * len(lead) + a.shape[len(lead):],
                                      lambda i: lead + (0, 0), pipeline_mode=pl.Buffered(1))
    return pl.pallas_call(
        functools.partial(_ffn_kernel, n_ff=n_ff),
        grid=(n // tm,),
        in_specs=[
            pl.BlockSpec((tm, d), lambda i: (i, 0)),
            resident(w1), resident(w3), resident(w2),
            pl.BlockSpec((1, d), lambda i: (0, 0)),
            pl.BlockSpec((1, d), lambda i: (0, 0)),
        ],
        out_specs=pl.BlockSpec((tm, d), lambda i: (i, 0)),
        out_shape=jax.ShapeDtypeStruct((n, d), F32),
        scratch_shapes=[pltpu.VMEM((tm, d), BF16), pltpu.VMEM((tm, d), F32)],
        compiler_params=_params("parallel"),
        name="ffn_ln",
    )(x, w1, w3, w2, g, b)


def _proj_ln_kernel(x_ref, *refs, n_in):
    a_refs = refs[:n_in]
    w_refs = refs[n_in:2 * n_in]
    g_ref, b_ref, o_ref = refs[2 * n_in:]
    f = None
    for a_ref, w_ref in zip(a_refs, w_refs):
        t = jnp.dot(a_ref[...].astype(BF16), w_ref[...], preferred_element_type=F32)
        f = t if f is None else f + t
    o_ref[...] = _layer_norm(ALPHA * x_ref[...] + f, g_ref[...], b_ref[...])


def proj_ln(x, acts, ws, g, b, tm=512):
    n, d = x.shape
    tm = _row_tile(n, tm)
    n_in = len(acts)
    return pl.pallas_call(
        functools.partial(_proj_ln_kernel, n_in=n_in),
        grid=(n // tm,),
        in_specs=[pl.BlockSpec((tm, d), lambda i: (i, 0))]
        + [pl.BlockSpec((tm, a.shape[1]), lambda i: (i, 0)) for a in acts]
        + [pl.BlockSpec(w.shape, lambda i: (0, 0)) for w in ws]
        + [pl.BlockSpec((1, d), lambda i: (0, 0))] * 2,
        out_specs=pl.BlockSpec((tm, d), lambda i: (i, 0)),
        out_shape=jax.ShapeDtypeStruct((n, d), F32),
        compiler_params=_params("parallel"),
        name="proj_ln",
    )(x, *acts, *ws, g, b)


def _rwkv_prep_kernel(p_ref, prev_ref, mu_ref, w0_ref, wup_ref, a0_ref, aup_ref, gup_ref,
                      kk_ref, ka_ref, rk_ref, seg_ref, *refs, chunked, tiles_per_seq):
    if chunked:
        tril_ref, ones_ref = refs[:2]
        refs = refs[2:]
    g_out, bonus_out = refs[-2:]
    p = p_ref[...]
    if chunked:
        first = (pl.program_id(0) % tiles_per_seq) == 0
        above = jnp.where(first, 0.0, prev_ref[7:8, :])
        top = lax.broadcasted_iota(jnp.int32, p.shape, 0) == 0
        prev = jnp.where(top, above, pltpu.roll(p, 1, 0))
    else:
        prev = prev_ref[...]
    xm = p + (prev - p) * mu_ref[...]
    r = xm[:, 0:RW_WIDTH]
    k = xm[:, RW_WIDTH:2 * RW_WIDTH]
    v = xm[:, 2 * RW_WIDTH:3 * RW_WIDTH]
    wa = xm[:, LORA_WA_START:LORA_G_START]
    gl = xm[:, LORA_G_START:RW_COLS]

    def seg_sum(x):
        return jnp.dot(_split3(x), seg_ref[...], preferred_element_type=F32)

    lw = jnp.dot(jnp.tanh(wa).astype(BF16), wup_ref[...], preferred_element_type=F32)
    la = jnp.dot(wa.astype(BF16), aup_ref[...], preferred_element_type=F32)
    z = -(w0_ref[...] + lw)
    softplus = jnp.maximum(z, 0.0) + jnp.log(1.0 + jnp.exp(-jnp.abs(z)))
    w_log = -softplus - 0.5
    log_decay = -jnp.exp(w_log)
    a = _sigmoid(a0_ref[...] + la)
    g = jnp.dot(_sigmoid(gl).astype(BF16), gup_ref[...], preferred_element_type=F32)
    kk = k * kk_ref[...]
    nrm = jnp.sqrt(seg_sum(kk * kk))
    kk = kk / jnp.maximum(nrm, NORM_EPS)
    k_mod = k * (1.0 + (a - 1.0) * ka_ref[...])
    kka = kk * a
    if chunked:
        parts = _split3(log_decay)

        def time_sum(m_ref):
            s3 = jnp.dot(m_ref[...], parts, preferred_element_type=F32)
            return s3[:, :RW_WIDTH] + s3[:, RW_WIDTH:2 * RW_WIDTH] + s3[:, 2 * RW_WIDTH:]

        cum = time_sum(tril_ref)
        tot = time_sum(ones_ref)
        grow = jnp.exp(-cum)
        rest = jnp.exp(tot - cum)
        outs = (kk * jnp.exp(cum - log_decay), r * jnp.exp(cum), -kka * grow, k_mod * grow,
                -kka * rest, k_mod * rest, v, jnp.exp(tot))
    else:
        outs = (r, jnp.exp(log_decay), k_mod, v, kk, kka)
    for o_ref, val in zip(refs, outs):
        for pr in range(RW_PAIRS):
            o_ref[pr] = val[:, pr * LANES:(pr + 1) * LANES].astype(o_ref.dtype)
    g_out[...] = g
    bonus_out[...] = seg_sum(r * k_mod * rk_ref[...]) * v


def rwkv_prep(p, prev, mu, w0, wup, a0, aup, gup, k_k, k_a, r_k, seg, seq_len, tm=256):
    n = p.shape[0]
    chunked = seq_len > 1
    tm = _row_tile(n, tm)
    row = lambda c: pl.BlockSpec((tm, c), lambda i: (i, 0))
    prev_spec = row(RW_COLS)
    if chunked:
        assert seq_len % tm == 0
        prev_spec = pl.BlockSpec((8, RW_COLS), lambda i: (jnp.maximum(i * (tm // 8) - 1, 0), 0))
    full = lambda a: pl.BlockSpec(a.shape, lambda i: (0, 0))
    consts = (mu, w0, wup, a0, aup, gup, k_k, k_a, r_k, seg)
    dtypes = [F32] * 6
    if chunked:
        assert tm % CHUNK == 0
        t_idx = jnp.arange(tm)
        same = (t_idx[:, None] // CHUNK) == (t_idx[None, :] // CHUNK)
        consts += ((same & (t_idx[None, :] <= t_idx[:, None])).astype(BF16), same.astype(BF16))
        dtypes = [BF16] * 7 + [F32]
    pair = pl.BlockSpec((RW_PAIRS, tm, LANES), lambda i: (0, i, 0))
    return pl.pallas_call(
        functools.partial(_rwkv_prep_kernel, chunked=chunked, tiles_per_seq=max(seq_len // tm, 1)),
        grid=(n // tm,),
        in_specs=[row(RW_COLS), prev_spec] + [full(c) for c in consts],
        out_specs=[pair] * len(dtypes) + [row(RW_WIDTH)] * 2,
        out_shape=[jax.ShapeDtypeStruct((RW_PAIRS, n, LANES), dt) for dt in dtypes]
        + [jax.ShapeDtypeStruct((n, RW_WIDTH), F32)] * 2,
        compiler_params=_params("parallel"),
        name="rwkv_prep",
    )(p, prev, *consts)


def _wkv_step_kernel(r_ref, w_ref, k_ref, v_ref, kk_ref, kka_ref, s0_ref, y_ref, s_ref, *, bb):
    sub = lax.broadcasted_iota(jnp.int32, (HEAD_DIM, LANES), 0)
    lane = lax.broadcasted_iota(jnp.int32, (HEAD_DIM, LANES), 1)
    diag = (lane % HEAD_DIM) == sub
    rj = lax.broadcasted_iota(jnp.int32, (2 * LANES, LANES), 0)
    cl = lax.broadcasted_iota(jnp.int32, (2 * LANES, LANES), 1)
    ones2 = jnp.where(((rj % LANES) // HEAD_DIM) == (cl // HEAD_DIM), 1.0, 0.0).astype(BF16)

    def head_sums(xs):
        out = jnp.dot(_split2(jnp.concatenate(xs, axis=0)), ones2, preferred_element_type=F32)
        return [out[i * HEAD_DIM:(i + 1) * HEAD_DIM] for i in range(len(xs))]

    pairs = [(b, p) for b in range(bb) for p in range(RW_PAIRS)]
    n = len(pairs)
    row = lambda ref, i: ref[pairs[i][1], pairs[i][0]]
    tiles = lambda x: x.reshape(HEAD_DIM // 8, 8, LANES)
    flat = lambda x: x.reshape(HEAD_DIM, LANES)
    s = [tiles(s0_ref[b, p]) for b, p in pairs]
    red = head_sums([flat(s[i] * row(kk_ref, i)) for i in range(n)]
                    + [flat(jnp.where(tiles(diag), row(v_ref, i), 0.0)) for i in range(n)])
    s = [s[i] * row(w_ref, i) - tiles(red[i]) * row(kka_ref, i) + tiles(red[n + i]) * row(k_ref, i)
         for i in range(n)]
    yb = head_sums([flat(s[i] * row(r_ref, i)) for i in range(n)])
    for i, (b, p) in enumerate(pairs):
        s_ref[b, p] = flat(s[i])
        y_ref[p, b] = jnp.sum(jnp.where(diag, yb[i], 0.0), axis=0, keepdims=True)


def wkv_step(r, w, k, v, kk, kka, s0, bb=8):
    _, nb, t, _ = r.shape
    assert t == 1 and nb % bb == 0
    seq = pl.BlockSpec((RW_PAIRS, bb, 1, LANES), lambda i: (0, i, 0, 0))
    st = pl.BlockSpec((bb, RW_PAIRS, HEAD_DIM, LANES), lambda i: (i, 0, 0, 0))
    return pl.pallas_call(
        functools.partial(_wkv_step_kernel, bb=bb),
        grid=(nb // bb,),
        in_specs=[seq] * 6 + [st],
        out_specs=[seq, st],
        out_shape=[jax.ShapeDtypeStruct((RW_PAIRS, nb, 1, LANES), F32),
                   jax.ShapeDtypeStruct(s0.shape, F32)],
        compiler_params=_params("parallel"),
        name="wkv_step",
    )(r, w, k, v, kk, kka, s0)


def _dot_nt(a, b):
    return lax.dot_general(a, b, (((1,), (1,)), ((), ())), preferred_element_type=F32)


def _dot_tn(a, b):
    return lax.dot_general(a, b, (((0,), (0,)), ((), ())), preferred_element_type=F32)


def _dot(a, b):
    return jnp.dot(a, b, preferred_element_type=F32)


def _wkv_chunk_kernel(qt_ref, rt_ref, bh_ref, kh_ref, bb_ref, kb_ref, v_ref, gc_ref, s0_ref,
                      y_ref, s_ref, *, nb, tc):
    @pl.when(pl.program_id(1) == 0)
    def _():
        s_ref[...] = s0_ref[...]

    c = CHUNK
    assert c == HEAD_DIM and 2 * c == LANES
    row = lax.broadcasted_iota(jnp.int32, (c, LANES), 0)
    lane = lax.broadcasted_iota(jnp.int32, (c, LANES), 1)
    head_a = lane < HEAD_DIM
    head_a2 = lax.broadcasted_iota(jnp.int32, (2 * c, LANES), 1) < HEAD_DIM
    m_mask = (lane >= c) & (lane - c < row)
    incl = (lane % c) <= row
    r64 = lax.broadcasted_iota(jnp.int32, (c, c), 0)
    c64 = lax.broadcasted_iota(jnp.int32, (c, c), 1)
    strict = c64 < r64
    eye = jnp.where(c64 == r64, 1.0, 0.0)
    pairs = [(b, p) for b in range(nb) for p in range(RW_PAIRS)]
    bf = lambda x: x.astype(BF16)
    stack = lambda x, y: jnp.concatenate([x, y], axis=0)

    def chunk(ci, carry):
        t0 = pl.multiple_of(ci * c, c)
        ld = lambda ref, b, p: ref[p, b, pl.ds(t0, c), :]
        qr = [stack(ld(qt_ref, b, p), ld(rt_ref, b, p)) for b, p in pairs]
        bk = [stack(ld(bh_ref, b, p), ld(kh_ref, b, p)) for b, p in pairs]
        zero = jnp.zeros((), BF16)
        e = [_dot_nt(q, stack(jnp.where(head_a2, x, zero), jnp.where(head_a2, zero, x)))
             for q, x in zip(qr, bk)]

        pw = [jnp.where(strict, x[:c, h * LANES:h * LANES + c], 0.0) for x in e for h in range(2)]
        inv = [eye + n for n in pw]
        for _ in range(5):
            pb = [bf(x) for x in pw]
            pw = [_dot(x, x) for x in pb]
            inv = [t + _dot(bf(x), bf(t)) for x, t in zip(pw, inv)]

        s = [s_ref[b, p] for b, p in pairs]
        ff = [_dot_nt(q, bf(stack(jnp.where(head_a, x, 0.0), jnp.where(head_a, 0.0, x))))
              for q, x in zip(qr, s)]
        v = [ld(v_ref, b, p) for b, p in pairs]
        vv = [stack(x, x) for x in v]
        heads = lambda fn: jnp.where(head_a, fn(0), fn(1))
        g = [f[:c] + heads(lambda h: _dot(bf(jnp.where(m_mask, x[:c, h * LANES:(h + 1) * LANES], 0.0)), w))
             for f, x, w in zip(ff, e, vv)]
        u = [heads(lambda h: _dot(bf(inv[2 * i + h]), bf(x))) for i, x in enumerate(g)]
        uv = [stack(bf(x), w) for x, w in zip(u, v)]
        y = [f[c:] + heads(lambda h: _dot(bf(jnp.where(incl, x[c:, h * LANES:(h + 1) * LANES], 0.0)), w))
             for f, x, w in zip(ff, e, uv)]
        add = [_dot_tn(w, stack(ld(bb_ref, b, p), ld(kb_ref, b, p))) for w, (b, p) in zip(uv, pairs)]
        for i, (b, p) in enumerate(pairs):
            y_ref[p, b, pl.ds(t0, c), :] = y[i]
            gc = gc_ref[p, b, pl.ds(t0, 8), :]
            kept = (s[i].reshape(c // 8, 8, LANES) * gc).reshape(c, LANES)
            s_ref[b, p] = kept + jnp.where(head_a, add[i][:c], add[i][c:])
        return carry

    lax.fori_loop(0, tc // c, chunk, 0)


def wkv_chunked(qt, rt, bh, kh, bb, kb, v, gc, s0, tc=256):
    _, nb, t, _ = qt.shape
    seq = pl.BlockSpec((RW_PAIRS, nb, tc, LANES), lambda i, j: (0, 0, j, 0))
    st = pl.BlockSpec((nb, RW_PAIRS, HEAD_DIM, LANES), lambda i, j: (0, 0, 0, 0))
    return pl.pallas_call(
        functools.partial(_wkv_chunk_kernel, nb=nb, tc=tc),
        grid=(1, t // tc),
        in_specs=[seq] * 8 + [st],
        out_specs=[seq, st],
        out_shape=[jax.ShapeDtypeStruct((RW_PAIRS, nb, t, LANES), F32),
                   jax.ShapeDtypeStruct(s0.shape, F32)],
        compiler_params=_params("arbitrary", "arbitrary"),
        name="wkv_chunked",
    )(qt, rt, bh, kh, bb, kb, v, gc, s0)


def _mix_out_kernel(x_ref, y_ref, bonus_ref, g_ref, ysw_ref, gng_ref, gnb_ref, seg_ref, wrw_ref, wsw_ref,
                    lng_ref, lnb_ref, o_ref):
    def seg_mean(x):
        return jnp.dot(_split3(x), seg_ref[...], preferred_element_type=F32) * (1.0 / HEAD_DIM)

    y = jnp.concatenate([y_ref[pr] for pr in range(RW_PAIRS)], axis=1)
    d = y - seg_mean(y)
    var = seg_mean(d * d)
    yn = d * lax.rsqrt(var + GN_EPS) * gng_ref[...] + gnb_ref[...]
    y_rw = (yn + bonus_ref[...]) * g_ref[...]
    f = _dot(y_rw.astype(BF16), wrw_ref[...]) + _dot(ysw_ref[...].astype(BF16), wsw_ref[...])
    o_ref[...] = _layer_norm(ALPHA * x_ref[...] + f, lng_ref[...], lnb_ref[...])


def mix_out_ln(x, y, bonus, g, y_sw, gn_g, gn_b, seg, w_rw, w_sw, ln_g, ln_b, tm=512):
    n, d = x.shape
    tm = _row_tile(n, tm)
    row = lambda c: pl.BlockSpec((tm, c), lambda i: (i, 0))
    full = lambda a: pl.BlockSpec(a.shape, lambda i: (0, 0))
    consts = (gn_g, gn_b, seg, w_rw, w_sw, ln_g, ln_b)
    return pl.pallas_call(
        _mix_out_kernel,
        grid=(n // tm,),
        in_specs=[row(d), pl.BlockSpec((RW_PAIRS, tm, LANES), lambda i: (0, i, 0)), row(RW_WIDTH),
                  row(RW_WIDTH), row(SW_WIDTH)] + [full(c) for c in consts],
        out_specs=row(d),
        out_shape=jax.ShapeDtypeStruct((n, d), F32),
        compiler_params=_params("parallel"),
        name="mix_out_ln",
    )(x, y, bonus, g, y_sw, *consts)


def _alibi_slope(h):
    return 2.0 ** (-8.0 * (h + 1) / SW_HEADS)


def _swa_prompt_kernel(sink_ref, bias_ref, q_ref, kp_ref, kc_ref, vp_ref, vc_ref, o_ref):
    qb = (q_ref[0] * HEAD_DIM ** -0.5).astype(BF16)
    kw = jnp.concatenate([kp_ref[0], kc_ref[0]], axis=0).astype(BF16)
    vw = jnp.concatenate([vp_ref[0], vc_ref[0]], axis=0).astype(BF16)
    kx = pltpu.roll(kw, HEAD_DIM, 1)
    vx = pltpu.roll(vw, HEAD_DIM, 1)
    low_kv = lax.broadcasted_iota(jnp.int32, kw.shape, 1) < HEAD_DIM
    low_q = lax.broadcasted_iota(jnp.int32, (WINDOW, LANES), 1) < HEAD_DIM
    zero = jnp.zeros((), BF16)
    heads = range(SW_HEADS)
    kc = [jnp.where(low_kv, kw, kx), jnp.where(low_kv, kx, kw)]
    vc = [jnp.where(low_kv, vw, vx), jnp.where(low_kv, vx, vw)]
    q2 = [qb[:, p * LANES:(p + 1) * LANES] for p in range(SW_HEADS // 2)]
    qh = [jnp.where(low_q, q2[h // 2], zero) if h % 2 == 0 else jnp.where(low_q, zero, q2[h // 2]) for h in heads]
    logits = [_dot_nt(qh[h], kc[h // SW_GROUP]) + bias_ref[h] for h in heads]
    m = [jnp.maximum(jnp.max(logits[h], axis=-1, keepdims=True), sink_ref[h]) for h in heads]
    e = [jnp.exp(logits[h] - m[h]) for h in heads]
    den = [jnp.sum(e[h], axis=-1, keepdims=True) + jnp.exp(sink_ref[h] - m[h]) for h in heads]
    o = [_dot(e[h].astype(BF16), vc[h // SW_GROUP]) / den[h] for h in heads]
    for p in range(SW_HEADS // 2):
        o_ref[0, :, p * LANES:(p + 1) * LANES] = jnp.where(low_q, o[2 * p], o[2 * p + 1])


def swa_prompt(q, k, v, sinks):
    nb, t, _ = q.shape
    i = jnp.arange(WINDOW)[:, None]
    j = jnp.arange(2 * WINDOW)[None, :]
    dist = WINDOW + i - j
    valid = (dist >= 0) & (dist < WINDOW)
    slopes = jnp.asarray([_alibi_slope(h) for h in range(SW_HEADS)], F32)[:, None, None]
    table = lambda ok: jnp.where(ok[None], -slopes * dist.astype(F32)[None], -jnp.inf)
    bias = jnp.stack([table(valid & (j >= WINDOW)), table(valid)])
    cur = lambda w: pl.BlockSpec((1, WINDOW, w), lambda b, n: (b, n, 0))
    prv = lambda w: pl.BlockSpec((1, WINDOW, w), lambda b, n: (b, jnp.maximum(n - 1, 0), 0))
    return pl.pallas_call(
        _swa_prompt_kernel,
        grid=(nb, t // WINDOW),
        in_specs=[pl.BlockSpec(memory_space=pltpu.SMEM),
                  pl.BlockSpec((None, SW_HEADS, WINDOW, 2 * WINDOW), lambda b, n: (jnp.minimum(n, 1), 0, 0, 0)),
                  cur(SW_WIDTH), prv(KV_WIDTH), cur(KV_WIDTH), prv(KV_WIDTH), cur(KV_WIDTH)],
        out_specs=cur(SW_WIDTH),
        out_shape=jax.ShapeDtypeStruct((nb, t, SW_WIDTH), F32),
        compiler_params=_params("parallel", "parallel"),
        name="swa_prompt",
    )(sinks, bias, q, k, k, v, v)


def _swa_sample_kernel(sink_ref, slope_ref, q_ref, kn_ref, vn_ref, ck_ref, cv_ref,
                       o_ref, nk_ref, nv_ref, *, bb):
    last = lax.broadcasted_iota(jnp.int32, (WINDOW, KV_WIDTH), 0) == WINDOW - 1
    j = lax.broadcasted_iota(jnp.int32, (SW_HEADS, WINDOW), 1)
    bias = slope_ref[...] * (WINDOW - 1 - j).astype(F32)
    sink = sink_ref[...]
    seqs = range(bb)
    kw = [jnp.where(last, kn_ref[b], pltpu.roll(ck_ref[b], WINDOW - 1, 0)) for b in seqs]
    vw = [jnp.where(last, vn_ref[b], pltpu.roll(cv_ref[b], WINDOW - 1, 0)) for b in seqs]
    s = [_dot_nt(q_ref[b].astype(BF16), kw[b].astype(BF16)) * HEAD_DIM ** -0.5 - bias for b in seqs]
    m = [jnp.maximum(jnp.max(x, axis=-1, keepdims=True), sink) for x in s]
    e = [jnp.exp(x - mx) for x, mx in zip(s, m)]
    prob = [x / (jnp.sum(x, axis=-1, keepdims=True) + jnp.exp(sink - mx)) for x, mx in zip(e, m)]
    o = [_dot(prob[b].astype(BF16), vw[b].astype(BF16)) for b in seqs]
    for b in seqs:
        nk_ref[b] = kw[b]
        nv_ref[b] = vw[b]
        o_ref[b] = o[b]


def swa_sample(q, kn, vn, ck, cv, sinks, slopes, bb=8):
    nb = q.shape[0]
    assert nb % bb == 0
    own = (jnp.arange(SW_HEADS) // SW_GROUP)[:, None] == jnp.arange(SW_KV_HEADS)[None, :]
    q2 = jnp.where(own[None, :, :, None], q[:, :, None, :], 0.0).reshape(nb, SW_HEADS, KV_WIDTH)
    per_b = lambda s: pl.BlockSpec((bb,) + s, lambda b: (b, 0, 0))
    full = lambda a: pl.BlockSpec(a.shape, lambda b: (0, 0))
    o2, nk, nv = pl.pallas_call(
        functools.partial(_swa_sample_kernel, bb=bb),
        grid=(nb // bb,),
        in_specs=[full(sinks), full(slopes), per_b((SW_HEADS, KV_WIDTH)), per_b((1, KV_WIDTH)),
                  per_b((1, KV_WIDTH)), per_b((WINDOW, KV_WIDTH)), per_b((WINDOW, KV_WIDTH))],
        out_specs=[per_b((SW_HEADS, KV_WIDTH)), per_b((WINDOW, KV_WIDTH)), per_b((WINDOW, KV_WIDTH))],
        out_shape=[jax.ShapeDtypeStruct((nb, SW_HEADS, KV_WIDTH), F32),
                   jax.ShapeDtypeStruct((nb, WINDOW, KV_WIDTH), F32),
                   jax.ShapeDtypeStruct((nb, WINDOW, KV_WIDTH), F32)],
        compiler_params=_params("parallel"),
        name="swa_sample",
    )(sinks, slopes, q2, kn, vn, ck, cv)
    o = jnp.sum(jnp.where(own[None, :, :, None], o2.reshape(nb, SW_HEADS, SW_KV_HEADS, HEAD_DIM), 0.0), axis=2)
    return o, nk, nv


def _mem_block_kernel(x_ref, wq_ref, mk_ref, mv_ref, wo_ref, g_ref, b_ref, o_ref):
    x = x_ref[0]
    qb = _dot(x.astype(BF16), wq_ref[...]).astype(BF16)
    outs = []
    for h in range(MEM_HEADS):
        cols = slice(h * MEM_HEAD_DIM, (h + 1) * MEM_HEAD_DIM)
        s = _dot_nt(qb[:, cols], mk_ref[0, :, cols].astype(BF16)) * MEM_HEAD_DIM ** -0.5
        e = jnp.exp(s - jnp.max(s, axis=-1, keepdims=True))
        den = jnp.sum(e, axis=-1, keepdims=True)
        outs.append(_dot(e.astype(BF16), mv_ref[0, :, cols].astype(BF16)) / den)
    o = jnp.concatenate(outs, axis=1).astype(BF16)
    o_ref[0] = _layer_norm(ALPHA * x + _dot(o, wo_ref[...]), g_ref[...], b_ref[...])


def mem_block(x, wq, mk, mv, wo, g, b, tm=512):
    ng, t, d = x.shape
    tm = _row_tile(t, tm)
    row = pl.BlockSpec((1, tm, d), lambda gi, i: (gi, i, 0))
    mem = pl.BlockSpec((1, MEM_TOKENS, d), lambda gi, i: (gi, 0, 0))
    full = lambda a: pl.BlockSpec(a.shape, lambda gi, i: (0, 0))
    return pl.pallas_call(
        _mem_block_kernel,
        grid=(ng, t // tm),
        in_specs=[row, full(wq), mem, mem, full(wo), full(g), full(b)],
        out_specs=row,
        out_shape=jax.ShapeDtypeStruct((ng, t, d), F32),
        compiler_params=_params("parallel", "parallel"),
        name="mem_block",
    )(x, wq, mk, mv, wo, g, b)


def _mem_attn_token_kernel(q_ref, mk_ref, mv_ref, o_ref, *, bb):
    rows = MEM_TOKENS * MEM_HEADS
    lane = lax.broadcasted_iota(jnp.int32, (MEM_HEADS, rows), 1)
    head = lax.broadcasted_iota(jnp.int32, (MEM_HEADS, rows), 0)
    own = (lane % MEM_HEADS) == head
    for b in range(bb):
        q = q_ref[b]
        q4 = jnp.concatenate([q[:, h * MEM_HEAD_DIM:(h + 1) * MEM_HEAD_DIM] for h in range(MEM_HEADS)],
                             axis=0).astype(BF16)
        k2 = mk_ref[b].reshape(rows, MEM_HEAD_DIM).astype(BF16)
        v2 = mv_ref[b].reshape(rows, MEM_HEAD_DIM).astype(BF16)
        s = jnp.where(own, _dot_nt(q4, k2) * MEM_HEAD_DIM ** -0.5, -jnp.inf)
        m = jnp.max(s, axis=-1, keepdims=True)
        e = jnp.exp(s - m)
        prob = e / jnp.sum(e, axis=-1, keepdims=True)
        o = _dot(prob.astype(BF16), v2)
        for h in range(MEM_HEADS):
            o_ref[b, :, h * MEM_HEAD_DIM:(h + 1) * MEM_HEAD_DIM] = o[h:h + 1]


def mem_attn_token(q, mk, mv, layer, bb=4):
    ng, _, d = q.shape
    assert ng % bb == 0
    row = pl.BlockSpec((bb, 1, d), lambda g: (g, 0, 0))
    mem = pl.BlockSpec((None, bb, MEM_TOKENS, MEM_HEADS, MEM_HEAD_DIM), lambda g: (layer, g, 0, 0, 0))
    return pl.pallas_call(
        functools.partial(_mem_attn_token_kernel, bb=bb),
        grid=(ng // bb,),
        in_specs=[row, mem, mem],
        out_specs=row,
        out_shape=jax.ShapeDtypeStruct((ng, 1, d), F32),
        compiler_params=_params("parallel"),
        name="mem_attn_token",
    )(q, mk, mv)


def _pair_state(s):
    nb = s.shape[0]
    s = s.reshape(nb, RW_PAIRS, 2, HEAD_DIM, HEAD_DIM)
    return jnp.swapaxes(s, 2, 3).reshape(nb, RW_PAIRS, HEAD_DIM, LANES)


def _unpair_state(s):
    nb = s.shape[0]
    s = s.reshape(nb, RW_PAIRS, HEAD_DIM, 2, HEAD_DIM)
    return jnp.swapaxes(s, 2, 3).reshape(nb, RW_HEADS, HEAD_DIM, HEAD_DIM)


def kernel(x_prompt, x_sample, mem_prompt, state_wkv, state_shift, cache_win_k, cache_win_v,
           cache_mem_k, cache_mem_v, ln_g, ln_b, ffn_w1, ffn_w3, ffn_w2, w_in, rw_mu, rw_w0,
           rw_w_up, rw_a0, rw_a_up, rw_g_up, rw_k_k, rw_k_a, rw_r_k, rw_gn_g, rw_gn_b, sw_sinks,
           w_out, mem_wq, mem_wk, mem_wv, mem_wo):
    depth = ln_g.shape[0]
    bp, tp, d = x_prompt.shape
    bs, ts, _ = x_sample.shape
    assert ts == 1 and cache_win_k.shape[2] == WINDOW and tp % WINDOW == 0

    w1b, w3b, w2b = (w.astype(BF16) for w in (ffn_w1, ffn_w3, ffn_w2))
    w_in_b = w_in.astype(BF16)
    w_out_b = w_out.astype(BF16)
    wqb, wkb, wvb, wob = (w.astype(BF16) for w in (mem_wq, mem_wk, mem_wv, mem_wo))
    zpad = jnp.zeros((depth, D_W_LORA, RW_WIDTH), BF16)
    wup_b = jnp.concatenate([rw_w_up.astype(BF16), zpad], axis=1)
    aup_b = jnp.concatenate([zpad, rw_a_up.astype(BF16)], axis=1)
    gup_b = rw_g_up.astype(BF16)
    hid = jnp.arange(3 * RW_WIDTH) % RW_WIDTH // HEAD_DIM
    seg3 = (hid[:, None] == (jnp.arange(RW_WIDTH) // HEAD_DIM)[None, :]).astype(BF16)
    slopes = jnp.asarray([[_alibi_slope(h)] for h in range(SW_HEADS)], F32)
    q0, k0, v0 = RW_COLS, RW_COLS + SW_WIDTH, RW_COLS + SW_WIDTH + KV_WIDTH
    row = lambda a: a.reshape(1, -1)

    def layer(l, x, nb, t, prev_fn, s0, swa_fn, mem_fn):
        x = ffn_ln(x, w1b, w3b, w2b, (l, 0), row(ln_g[l, 0]), row(ln_b[l, 0]))
        p_rw, q, k, v = matmul_multi(
            x, [w_in_b[l, :, :q0], w_in_b[l, :, q0:k0], w_in_b[l, :, k0:v0], w_in_b[l, :, v0:]])
        *ops, g, bonus = rwkv_prep(
            p_rw, prev_fn(p_rw), row(rw_mu[l]), row(rw_w0[l]), wup_b[l], row(rw_a0[l]), aup_b[l],
            gup_b[l], row(rw_k_k[l]), row(rw_k_a[l]), row(rw_r_k[l]), seg3, seq_len=t)
        ops = [a.reshape(RW_PAIRS, nb, t, LANES) for a in ops]
        if t > 1:
            y, s_fin = wkv_chunked(*ops, s0)
        else:
            y, s_fin = wkv_step(*ops, s0)
        y_sw, win_k, win_v = swa_fn(q, k, v)
        shift = x.reshape(nb, t, d)[:, -1]
        x = mix_out_ln(x, y.reshape(RW_PAIRS, nb * t, LANES), bonus, g, y_sw, row(rw_gn_g[l]),
                       row(rw_gn_b[l]), seg3, w_out_b[l, :RW_WIDTH], w_out_b[l, RW_WIDTH:],
                       row(ln_g[l, 1]), row(ln_b[l, 1]))
        x = mem_fn(x)
        x = ffn_ln(x, w1b, w3b, w2b, (l, 1), row(ln_g[l, 3]), row(ln_b[l, 3]))
        return x, _unpair_state(s_fin), shift, win_k, win_v

    xp = x_prompt.reshape(bp * tp, d)
    p_wkv, p_shift, p_wk, p_wv, p_mk, p_mv = [], [], [], [], [], []
    for l in range(depth):
        prev_prompt = lambda p_rw: p_rw

        def swa_p(q, k, v, l=l):
            k3 = k.reshape(bp, tp, KV_WIDTH)
            v3 = v.reshape(bp, tp, KV_WIDTH)
            y = swa_prompt(q.reshape(bp, tp, SW_WIDTH), k3, v3, sw_sinks[l])
            tail = lambda a: a[:, -WINDOW:].reshape(bp, WINDOW, SW_KV_HEADS, HEAD_DIM)
            return y.reshape(bp * tp, SW_WIDTH), tail(k3), tail(v3)

        mk, mv = matmul_multi(mem_prompt.reshape(bp * MEM_TOKENS, d), [wkb[l], wvb[l]])
        mk = mk.reshape(bp, MEM_TOKENS, d)
        mv = mv.reshape(bp, MEM_TOKENS, d)

        def mem_p(x, l=l, mk=mk, mv=mv):
            return mem_block(x.reshape(bp, tp, d), wqb[l], mk, mv, wob[l], row(ln_g[l, 2]),
                             row(ln_b[l, 2])).reshape(bp * tp, d)

        s0 = jnp.zeros((bp, RW_PAIRS, HEAD_DIM, LANES), F32)
        xp, s_fin, shift, wk_, wv_ = layer(l, xp, bp, tp, prev_prompt, s0, swa_p, mem_p)
        p_wkv.append(s_fin)
        p_shift.append(shift)
        p_wk.append(wk_)
        p_wv.append(wv_)
        p_mk.append(mk.reshape(bp, MEM_TOKENS, MEM_HEADS, MEM_HEAD_DIM))
        p_mv.append(mv.reshape(bp, MEM_TOKENS, MEM_HEADS, MEM_HEAD_DIM))

    xs = x_sample.reshape(bs, d)
    s_wkv, s_shift, s_wk, s_wv = [], [], [], []
    for l in range(depth):
        def prev_sample(p_rw, l=l):
            (prev,) = matmul_multi(state_shift[l], [w_in_b[l, :, :q0]])
            return prev

        def swa_s(q, k, v, l=l):
            o, nk, nv = swa_sample(
                q.reshape(bs, SW_HEADS, HEAD_DIM), k.reshape(bs, 1, KV_WIDTH), v.reshape(bs, 1, KV_WIDTH),
                cache_win_k[l].reshape(bs, WINDOW, KV_WIDTH), cache_win_v[l].reshape(bs, WINDOW, KV_WIDTH),
                sw_sinks[l].reshape(SW_HEADS, 1), slopes)
            unflat = lambda a: a.reshape(bs, WINDOW, SW_KV_HEADS, HEAD_DIM)
            return o.reshape(bs, SW_WIDTH), unflat(nk), unflat(nv)

        def mem_s(x, l=l):
            (qm,) = matmul_multi(x, [wqb[l]])
            o = mem_attn_token(qm.reshape(bs, 1, d), cache_mem_k, cache_mem_v, l).reshape(bs, d)
            return proj_ln(x, [o], [wob[l]], row(ln_g[l, 2]), row(ln_b[l, 2]))

        xs, s_fin, shift, wk_, wv_ = layer(l, xs, bs, 1, prev_sample, _pair_state(state_wkv[l]),
                                           swa_s, mem_s)
        s_wkv.append(s_fin)
        s_shift.append(shift)
        s_wk.append(wk_)
        s_wv.append(wv_)

    return (xp.reshape(bp, tp, d), xs.reshape(bs, 1, d),
            jnp.stack(p_wkv), jnp.stack(p_shift), jnp.stack(p_wk), jnp.stack(p_wv),
            jnp.stack(p_mk), jnp.stack(p_mv),
            jnp.stack(s_wkv), jnp.stack(s_shift), jnp.stack(s_wk), jnp.stack(s_wv))
```

```python
import functools

import jax
import jax.numpy as jnp
from jax import lax
from jax.experimental import pallas as pl
from jax.experimental.pallas import tpu as pltpu

F32 = jnp.float32
BF16 = jnp.bfloat16

D_MODEL = 1024
HEAD_DIM = 64
RW_WIDTH = 512
RW_HEADS = 8
RW_PAIRS = RW_HEADS // 2
SW_WIDTH = 512
SW_HEADS = 8
SW_KV_HEADS = 2
SW_GROUP = SW_HEADS // SW_KV_HEADS
KV_WIDTH = SW_KV_HEADS * HEAD_DIM
WINDOW = 128
D_W_LORA = 64
D_A_LORA = 64
D_G_LORA = 128
RW_COLS = 3 * RW_WIDTH + D_W_LORA + D_A_LORA + D_G_LORA
LORA_WA_START = 3 * RW_WIDTH
LORA_G_START = LORA_WA_START + D_W_LORA + D_A_LORA
MEM_TOKENS = 256
MEM_HEADS = 4
MEM_HEAD_DIM = D_MODEL // MEM_HEADS
D_FF = 2816
DEPTH = 4
ALPHA = (2.0 * DEPTH) ** 0.25
LN_EPS = 1e-5
GN_EPS = 64e-5
NORM_EPS = 1e-12

CHUNK = 64
FF_TILE = 256
LANES = 128
VMEM_LIMIT = 56 * 1024 * 1024


def _params(*semantics):
    return pltpu.CompilerParams(dimension_semantics=semantics, vmem_limit_bytes=VMEM_LIMIT)


def _row_tile(n, want):
    return want if n % want == 0 else n


def _layer_norm(z, g, b):
    mu = jnp.mean(z, axis=-1, keepdims=True)
    d = z - mu
    var = jnp.mean(d * d, axis=-1, keepdims=True)
    return d * lax.rsqrt(var + LN_EPS) * g + b


def _sigmoid(x):
    return 1.0 / (1.0 + jnp.exp(-x))


def _split3(x):
    hi = x.astype(BF16)
    r1 = x - hi.astype(F32)
    mid = r1.astype(BF16)
    lo = (r1 - mid.astype(F32)).astype(BF16)
    return jnp.concatenate([hi, mid, lo], axis=1)


def _split2(x):
    hi = x.astype(BF16)
    lo = (x - hi.astype(F32)).astype(BF16)
    return jnp.concatenate([hi, lo], axis=1)


def _mm_kernel(x_ref, *refs, n_out):
    xb = x_ref[...].astype(BF16)
    for w_ref, o_ref in zip(refs[:n_out], refs[n_out:]):
        o_ref[...] = jnp.dot(xb, w_ref[...], preferred_element_type=F32)


def matmul_multi(x, ws, tm=512):
    n, k = x.shape
    tm = _row_tile(n, tm)
    n_out = len(ws)
    return pl.pallas_call(
        functools.partial(_mm_kernel, n_out=n_out),
        grid=(n // tm,),
        in_specs=[pl.BlockSpec((tm, k), lambda i: (i, 0))]
        + [pl.BlockSpec(w.shape, lambda i: (0, 0)) for w in ws],
        out_specs=[pl.BlockSpec((tm, w.shape[1]), lambda i: (i, 0)) for w in ws],
        out_shape=[jax.ShapeDtypeStruct((n, w.shape[1]), F32) for w in ws],
        compiler_params=_params("parallel"),
        name="matmul_multi",
    )(x, *ws)


def _ffn_kernel(x_ref, w1_ref, w3_ref, w2_ref, g_ref, b_ref, o_ref, xb_ref, acc_ref, *, n_ff):
    xb_ref[...] = x_ref[...].astype(BF16)

    def part(c):
        cols = slice(c * FF_TILE, (c + 1) * FF_TILE)
        xb = xb_ref[...]
        h1 = jnp.dot(xb, w1_ref[:, cols], preferred_element_type=F32)
        h3 = jnp.dot(xb, w3_ref[:, cols], preferred_element_type=F32)
        h = (h1 * _sigmoid(h1)) * h3
        return jnp.dot(h.astype(BF16), w2_ref[cols, :], preferred_element_type=F32)

    acc_ref[...] = part(0)
    for c in range(1, n_ff):
        acc_ref[...] += part(c)
    z = ALPHA * x_ref[...] + 0.5 * acc_ref[...]
    o_ref[...] = _layer_norm(z, g_ref[...], b_ref[...])


def ffn_ln(x, w1, w3, w2, lead, g, b, tm=1024):
    n, d = x.shape
    tm = _row_tile(n, tm)
    n_ff = D_FF // FF_TILE
    resident = lambda a: pl.BlockSpec((None,) * len(lead) + a.shape[len(lead):],
                                      lambda i: lead + (0, 0), pipeline_mode=pl.Buffered(1))
    return pl.pallas_call(
        functools.partial(_ffn_kernel, n_ff=n_ff),
        grid=(n // tm,),
        in_specs=[
            pl.BlockSpec((tm, d), lambda i: (i, 0)),
            resident(w1), resident(w3), resident(w2),
            pl.BlockSpec((1, d), lambda i: (0, 0)),
            pl.BlockSpec((1, d), lambda i: (0, 0)),
        ],
        out_specs=pl.BlockSpec((tm, d), lambda i: (i, 0)),
        out_shape=jax.ShapeDtypeStruct((n, d), F32),
        scratch_shapes=[pltpu.VMEM((tm, d), BF16), pltpu.VMEM((tm, d), F32)],
        compiler_params=_params("parallel"),
        name="ffn_ln",
    )(x, w1, w3, w2, g, b)


def _proj_ln_kernel(x_ref, *refs, n_in):
    a_refs = refs[:n_in]
    w_refs = refs[n_in:2 * n_in]
    g_ref, b_ref, o_ref = refs[2 * n_in:]
    f = None
    for a_ref, w_ref in zip(a_refs, w_refs):
        t = jnp.dot(a_ref[...].astype(BF16), w_ref[...], preferred_element_type=F32)
        f = t if f is None else f + t
    o_ref[...] = _layer_norm(ALPHA * x_ref[...] + f, g_ref[...], b_ref[...])


def proj_ln(x, acts, ws, g, b, tm=512):
    n, d = x.shape
    tm = _row_tile(n, tm)
    n_in = len(acts)
    return pl.pallas_call(
        functools.partial(_proj_ln_kernel, n_in=n_in),
        grid=(n // tm,),
        in_specs=[pl.BlockSpec((tm, d), lambda i: (i, 0))]
        + [pl.BlockSpec((tm, a.shape[1]), lambda i: (i, 0)) for a in acts]
        + [pl.BlockSpec(w.shape, lambda i: (0, 0)) for w in ws]
        + [pl.BlockSpec((1, d), lambda i: (0, 0))] * 2,
        out_specs=pl.BlockSpec((tm, d), lambda i: (i, 0)),
        out_shape=jax.ShapeDtypeStruct((n, d), F32),
        compiler_params=_params("parallel"),
        name="proj_ln",
    )(x, *acts, *ws, g, b)


def _rwkv_prep_kernel(p_ref, prev_ref, mu_ref, w0_ref, wup_ref, a0_ref, aup_ref, gup_ref,
                      kk_ref, ka_ref, rk_ref, seg_ref, *refs, chunked, tiles_per_seq):
    if chunked:
        tril_ref, ones_ref = refs[:2]
        refs = refs[2:]
    g_out, bonus_out = refs[-2:]
    p = p_ref[...]
    if chunked:
        first = (pl.program_id(0) % tiles_per_seq) == 0
        above = jnp.where(first, 0.0, prev_ref[7:8, :])
        top = lax.broadcasted_iota(jnp.int32, p.shape, 0) == 0
        prev = jnp.where(top, above, pltpu.roll(p, 1, 0))
    else:
        prev = prev_ref[...]
    xm = p + (prev - p) * mu_ref[...]
    r = xm[:, 0:RW_WIDTH]
    k = xm[:, RW_WIDTH:2 * RW_WIDTH]
    v = xm[:, 2 * RW_WIDTH:3 * RW_WIDTH]
    wa = xm[:, LORA_WA_START:LORA_G_START]
    gl = xm[:, LORA_G_START:RW_COLS]

    def seg_sum(x):
        return jnp.dot(_split2(x), seg_ref[...], preferred_element_type=F32)

    lw = jnp.dot(jnp.tanh(wa).astype(BF16), wup_ref[...], preferred_element_type=F32)
    la = jnp.dot(wa.astype(BF16), aup_ref[...], preferred_element_type=F32)
    z = -(w0_ref[...] + lw)
    softplus = jnp.maximum(z, 0.0) + jnp.log(1.0 + jnp.exp(-jnp.abs(z)))
    w_log = -softplus - 0.5
    log_decay = -jnp.exp(w_log)
    a = _sigmoid(a0_ref[...] + la)
    g = jnp.dot(_sigmoid(gl).astype(BF16), gup_ref[...], preferred_element_type=F32)
    kk = k * kk_ref[...]
    nrm = jnp.sqrt(seg_sum(kk * kk))
    kk = kk / jnp.maximum(nrm, NORM_EPS)
    k_mod = k * (1.0 + (a - 1.0) * ka_ref[...])
    kka = kk * a
    if chunked:
        parts = _split3(log_decay)

        def time_sum(m_ref):
            s3 = jnp.dot(m_ref[...], parts, preferred_element_type=F32)
            return s3[:, :RW_WIDTH] + s3[:, RW_WIDTH:2 * RW_WIDTH] + s3[:, 2 * RW_WIDTH:]

        cum = time_sum(tril_ref)
        tot = time_sum(ones_ref)
        grow = jnp.exp(-cum)
        rest = jnp.exp(tot - cum)
        outs = (kk * jnp.exp(cum - log_decay), r * jnp.exp(cum), -kka * grow, k_mod * grow,
                -kka * rest, k_mod * rest, v, jnp.exp(tot))
    else:
        outs = (r, jnp.exp(log_decay), k_mod, v, kk, kka)
    for o_ref, val in zip(refs, outs):
        for pr in range(RW_PAIRS):
            o_ref[pr] = val[:, pr * LANES:(pr + 1) * LANES].astype(o_ref.dtype)
    g_out[...] = g
    bonus_out[...] = seg_sum(r * k_mod * rk_ref[...]) * v


def rwkv_prep(p, prev, mu, w0, wup, a0, aup, gup, k_k, k_a, r_k, seg, seq_len, tm=256):
    n = p.shape[0]
    chunked = seq_len > 1
    tm = _row_tile(n, tm)
    row = lambda c: pl.BlockSpec((tm, c), lambda i: (i, 0))
    prev_spec = row(RW_COLS)
    if chunked:
        assert seq_len % tm == 0
        prev_spec = pl.BlockSpec((8, RW_COLS), lambda i: (jnp.maximum(i * (tm // 8) - 1, 0), 0))
    full = lambda a: pl.BlockSpec(a.shape, lambda i: (0, 0))
    consts = (mu, w0, wup, a0, aup, gup, k_k, k_a, r_k, seg)
    dtypes = [F32] * 6
    if chunked:
        assert tm % CHUNK == 0
        t_idx = jnp.arange(tm)
        same = (t_idx[:, None] // CHUNK) == (t_idx[None, :] // CHUNK)
        consts += ((same & (t_idx[None, :] <= t_idx[:, None])).astype(BF16), same.astype(BF16))
        dtypes = [BF16] * 7 + [F32]
    pair = pl.BlockSpec((RW_PAIRS, tm, LANES), lambda i: (0, i, 0))
    return pl.pallas_call(
        functools.partial(_rwkv_prep_kernel, chunked=chunked, tiles_per_seq=max(seq_len // tm, 1)),
        grid=(n // tm,),
        in_specs=[row(RW_COLS), prev_spec] + [full(c) for c in consts],
        out_specs=[pair] * len(dtypes) + [row(RW_WIDTH)] * 2,
        out_shape=[jax.ShapeDtypeStruct((RW_PAIRS, n, LANES), dt) for dt in dtypes]
        + [jax.ShapeDtypeStruct((n, RW_WIDTH), F32)] * 2,
        compiler_params=_params("parallel"),
        name="rwkv_prep",
    )(p, prev, *consts)


def _wkv_step_kernel(r_ref, w_ref, k_ref, v_ref, kk_ref, kka_ref, s0_ref, y_ref, s_ref, *, bb):
    sub = lax.broadcasted_iota(jnp.int32, (HEAD_DIM, LANES), 0)
    lane = lax.broadcasted_iota(jnp.int32, (HEAD_DIM, LANES), 1)
    diag = (lane % HEAD_DIM) == sub
    rj = lax.broadcasted_iota(jnp.int32, (2 * LANES, LANES), 0)
    cl = lax.broadcasted_iota(jnp.int32, (2 * LANES, LANES), 1)
    ones2 = jnp.where(((rj % LANES) // HEAD_DIM) == (cl // HEAD_DIM), 1.0, 0.0).astype(BF16)

    def head_sums(xs):
        out = jnp.dot(_split2(jnp.concatenate(xs, axis=0)), ones2, preferred_element_type=F32)
        return [out[i * HEAD_DIM:(i + 1) * HEAD_DIM] for i in range(len(xs))]

    pairs = [(b, p) for b in range(bb) for p in range(RW_PAIRS)]
    n = len(pairs)
    row = lambda ref, i: ref[pairs[i][1], pairs[i][0]]
    tiles = lambda x: x.reshape(HEAD_DIM // 8, 8, LANES)
    flat = lambda x: x.reshape(HEAD_DIM, LANES)
    s = [tiles(s0_ref[b, p]) for b, p in pairs]
    red = head_sums([flat(s[i] * row(kk_ref, i)) for i in range(n)]
                    + [flat(jnp.where(tiles(diag), row(v_ref, i), 0.0)) for i in range(n)])
    s = [s[i] * row(w_ref, i) - tiles(red[i]) * row(kka_ref, i) + tiles(red[n + i]) * row(k_ref, i)
         for i in range(n)]
    yb = head_sums([flat(s[i] * row(r_ref, i)) for i in range(n)])
    for i, (b, p) in enumerate(pairs):
        s_ref[b, p] = flat(s[i])
        y_ref[p, b] = jnp.sum(jnp.where(diag, yb[i], 0.0), axis=0, keepdims=True)


def wkv_step(r, w, k, v, kk, kka, s0, bb=8):
    _, nb, t, _ = r.shape
    assert t == 1 and nb % bb == 0
    seq = pl.BlockSpec((RW_PAIRS, bb, 1, LANES), lambda i: (0, i, 0, 0))
    st = pl.BlockSpec((bb, RW_PAIRS, HEAD_DIM, LANES), lambda i: (i, 0, 0, 0))
    return pl.pallas_call(
        functools.partial(_wkv_step_kernel, bb=bb),
        grid=(nb // bb,),
        in_specs=[seq] * 6 + [st],
        out_specs=[seq, st],
        out_shape=[jax.ShapeDtypeStruct((RW_PAIRS, nb, 1, LANES), F32),
                   jax.ShapeDtypeStruct(s0.shape, F32)],
        compiler_params=_params("parallel"),
        name="wkv_step",
    )(r, w, k, v, kk, kka, s0)


def _dot_nt(a, b):
    return lax.dot_general(a, b, (((1,), (1,)), ((), ())), preferred_element_type=F32)


def _dot_tn(a, b):
    return lax.dot_general(a, b, (((0,), (0,)), ((), ())), preferred_element_type=F32)


def _dot(a, b):
    return jnp.dot(a, b, preferred_element_type=F32)


def _wkv_chunk_kernel(qt_ref, rt_ref, bh_ref, kh_ref, bb_ref, kb_ref, v_ref, gc_ref, s0_ref,
                      y_ref, s_ref, *, nb, tc):
    @pl.when(pl.program_id(1) == 0)
    def _():
        s_ref[...] = s0_ref[...]

    c = CHUNK
    assert c == HEAD_DIM and 2 * c == LANES
    row = lax.broadcasted_iota(jnp.int32, (c, LANES), 0)
    lane = lax.broadcasted_iota(jnp.int32, (c, LANES), 1)
    head_a = lane < HEAD_DIM
    head_a2 = lax.broadcasted_iota(jnp.int32, (2 * c, LANES), 1) < HEAD_DIM
    strict = (lane % c) < row
    eye = jnp.where((lane % c) == row, 1.0, 0.0)
    row4 = lax.broadcasted_iota(jnp.int32, (c, 2 * LANES), 0)
    col4 = lax.broadcasted_iota(jnp.int32, (c, 2 * LANES), 1)
    kh_cols = (col4 >= c) & (col4 < 3 * c)
    m_mask = kh_cols & ((col4 % c) < row4)
    incl = (col4 % c) <= row4
    pairs = [(b, p) for b in range(nb) for p in range(RW_PAIRS)]
    bf = lambda x: x.astype(BF16)
    stack = lambda *xs: jnp.concatenate(xs, axis=0)
    zero = jnp.zeros((), BF16)
    only_a = lambda x: jnp.where(head_a if x.shape[0] == c else head_a2, x, zero)
    only_b = lambda x: jnp.where(head_a if x.shape[0] == c else head_a2, zero, x)
    per_head = lambda x: stack(only_a(x), only_b(x))
    zeros_c = jnp.zeros((c, LANES), BF16)

    def inverse_stages(ci, ready):
        ld = lambda ref, b, p: ref[p, b, pl.ds(ci * c, c), :]
        qr = [stack(ld(qt_ref, b, p), ld(rt_ref, b, p)) for b, p in pairs]
        bh = [ld(bh_ref, b, p) for b, p in pairs]
        kh = [ld(kh_ref, b, p) for b, p in pairs]
        e = [_dot_nt(q, stack(only_a(stack(x, y)), only_b(stack(y, x)))) for q, x, y in zip(qr, bh, kh)]
        yield
        pw = [jnp.where(strict, jnp.where(head_a, x[:c, :LANES], x[:c, LANES:]), 0.0) for x in e]
        inv = [eye + n for n in pw]
        for _ in range(5):
            pw = [_dot(x, per_head(x)) for x in [bf(x) for x in pw]]
            yield
            inv = [t + _dot(bf(x), per_head(bf(t))) for x, t in zip(pw, inv)]
            yield
        ready[ci] = (qr, e, inv)

    def state_stages(ci, ready):
        qr, e, inv = ready.pop(ci)
        ld = lambda ref, b, p: ref[p, b, pl.ds(ci * c, c), :]
        s = [s_ref[b, p] for b, p in pairs]
        ff = [_dot_nt(q, per_head(bf(x))) for q, x in zip(qr, s)]
        v = [ld(v_ref, b, p) for b, p in pairs]
        yield
        g = [f[:c] + _dot(bf(jnp.where(m_mask, x[:c], 0.0)), stack(zeros_c, only_a(w), only_b(w), zeros_c))
             for f, x, w in zip(ff, e, v)]
        yield
        u = [bf(_dot(bf(t), per_head(bf(x)))) for t, x in zip(inv, g)]
        yield
        y = [f[c:] + _dot(bf(jnp.where(incl, x[c:], 0.0)), stack(only_a(uu), only_a(w), only_b(w), only_b(uu)))
             for f, x, uu, w in zip(ff, e, u, v)]
        add = [_dot_tn(stack(uu, w), stack(ld(bb_ref, b, p), ld(kb_ref, b, p)))
               for uu, w, (b, p) in zip(u, v, pairs)]
        yield
        for i, (b, p) in enumerate(pairs):
            y_ref[p, b, pl.ds(ci * c, c), :] = y[i]
            gc = gc_ref[p, b, pl.ds(ci * c, 8), :]
            kept = (s[i].reshape(c // 8, 8, LANES) * gc).reshape(c, LANES)
            s_ref[b, p] = kept + jnp.where(head_a, add[i][:c], add[i][c:])
        yield

    n_chunks = tc // c
    state_len, inv_len = 5, 11
    ready, inverses, state = {}, {}, None
    for tick in range(-inv_len, state_len * n_chunks):
        for ci in range(n_chunks):
            if ci not in inverses and state_len * ci - inv_len <= tick:
                inverses[ci] = inverse_stages(ci, ready)
        for gen in inverses.values():
            next(gen, None)
        if tick >= 0:
            if tick % state_len == 0:
                state = state_stages(tick // state_len, ready)
            next(state)


def wkv_chunked(qt, rt, bh, kh, bb, kb, v, gc, s0, tc=512):
    _, nb, t, _ = qt.shape
    seq = pl.BlockSpec((RW_PAIRS, nb, tc, LANES), lambda i, j: (0, 0, j, 0))
    st = pl.BlockSpec((nb, RW_PAIRS, HEAD_DIM, LANES), lambda i, j: (0, 0, 0, 0))
    return pl.pallas_call(
        functools.partial(_wkv_chunk_kernel, nb=nb, tc=tc),
        grid=(1, t // tc),
        in_specs=[seq] * 8 + [st],
        out_specs=[seq, st],
        out_shape=[jax.ShapeDtypeStruct((RW_PAIRS, nb, t, LANES), F32),
                   jax.ShapeDtypeStruct(s0.shape, F32)],
        compiler_params=_params("arbitrary", "arbitrary"),
        name="wkv_chunked",
    )(qt, rt, bh, kh, bb, kb, v, gc, s0)


def _mix_out_kernel(x_ref, y_ref, bonus_ref, g_ref, ysw_ref, gng_ref, gnb_ref, seg_ref, wrw_ref, wsw_ref,
                    lng_ref, lnb_ref, o_ref):
    def seg_mean(x):
        return jnp.dot(_split2(x), seg_ref[...], preferred_element_type=F32) * (1.0 / HEAD_DIM)

    y = jnp.concatenate([y_ref[pr] for pr in range(RW_PAIRS)], axis=1)
    d = y - seg_mean(y)
    var = seg_mean(d * d)
    yn = d * lax.rsqrt(var + GN_EPS) * gng_ref[...] + gnb_ref[...]
    y_rw = (yn + bonus_ref[...]) * g_ref[...]
    f = _dot(y_rw.astype(BF16), wrw_ref[...]) + _dot(ysw_ref[...].astype(BF16), wsw_ref[...])
    o_ref[...] = _layer_norm(ALPHA * x_ref[...] + f, lng_ref[...], lnb_ref[...])


def mix_out_ln(x, y, bonus, g, y_sw, gn_g, gn_b, seg, w_rw, w_sw, ln_g, ln_b, tm=512):
    n, d = x.shape
    tm = _row_tile(n, tm)
    row = lambda c: pl.BlockSpec((tm, c), lambda i: (i, 0))
    full = lambda a: pl.BlockSpec(a.shape, lambda i: (0, 0))
    consts = (gn_g, gn_b, seg, w_rw, w_sw, ln_g, ln_b)
    return pl.pallas_call(
        _mix_out_kernel,
        grid=(n // tm,),
        in_specs=[row(d), pl.BlockSpec((RW_PAIRS, tm, LANES), lambda i: (0, i, 0)), row(RW_WIDTH),
                  row(RW_WIDTH), row(SW_WIDTH)] + [full(c) for c in consts],
        out_specs=row(d),
        out_shape=jax.ShapeDtypeStruct((n, d), F32),
        compiler_params=_params("parallel"),
        name="mix_out_ln",
    )(x, y, bonus, g, y_sw, *consts)


def _alibi_slope(h):
    return 2.0 ** (-8.0 * (h + 1) / SW_HEADS)


def _swa_prompt_kernel(sink_ref, bias_ref, q_ref, kp_ref, kc_ref, vp_ref, vc_ref, o_ref):
    qb = (q_ref[0] * HEAD_DIM ** -0.5).astype(BF16)
    kw = jnp.concatenate([kp_ref[0], kc_ref[0]], axis=0).astype(BF16)
    vw = jnp.concatenate([vp_ref[0], vc_ref[0]], axis=0).astype(BF16)
    kx = pltpu.roll(kw, HEAD_DIM, 1)
    vx = pltpu.roll(vw, HEAD_DIM, 1)
    low_kv = lax.broadcasted_iota(jnp.int32, kw.shape, 1) < HEAD_DIM
    low_q = lax.broadcasted_iota(jnp.int32, (WINDOW, LANES), 1) < HEAD_DIM
    zero = jnp.zeros((), BF16)
    heads = range(SW_HEADS)
    kc = [jnp.where(low_kv, kw, kx), jnp.where(low_kv, kx, kw)]
    vc = [jnp.where(low_kv, vw, vx), jnp.where(low_kv, vx, vw)]
    q2 = [qb[:, p * LANES:(p + 1) * LANES] for p in range(SW_HEADS // 2)]
    qh = [jnp.where(low_q, q2[h // 2], zero) if h % 2 == 0 else jnp.where(low_q, zero, q2[h // 2]) for h in heads]
    logits = [_dot_nt(qh[h], kc[h // SW_GROUP]) + bias_ref[h] for h in heads]
    m = [jnp.maximum(jnp.max(logits[h], axis=-1, keepdims=True), sink_ref[h]) for h in heads]
    e = [jnp.exp(logits[h] - m[h]) for h in heads]
    den = [jnp.sum(e[h], axis=-1, keepdims=True) + jnp.exp(sink_ref[h] - m[h]) for h in heads]
    o = [_dot(e[h].astype(BF16), vc[h // SW_GROUP]) / den[h] for h in heads]
    for p in range(SW_HEADS // 2):
        o_ref[0, :, p * LANES:(p + 1) * LANES] = jnp.where(low_q, o[2 * p], o[2 * p + 1])


def swa_prompt(q, k, v, sinks):
    nb, t, _ = q.shape
    i = jnp.arange(WINDOW)[:, None]
    j = jnp.arange(2 * WINDOW)[None, :]
    dist = WINDOW + i - j
    valid = (dist >= 0) & (dist < WINDOW)
    slopes = jnp.asarray([_alibi_slope(h) for h in range(SW_HEADS)], F32)[:, None, None]
    table = lambda ok: jnp.where(ok[None], -slopes * dist.astype(F32)[None], -jnp.inf)
    bias = jnp.stack([table(valid & (j >= WINDOW)), table(valid)])
    cur = lambda w: pl.BlockSpec((1, WINDOW, w), lambda b, n: (b, n, 0))
    prv = lambda w: pl.BlockSpec((1, WINDOW, w), lambda b, n: (b, jnp.maximum(n - 1, 0), 0))
    return pl.pallas_call(
        _swa_prompt_kernel,
        grid=(nb, t // WINDOW),
        in_specs=[pl.BlockSpec(memory_space=pltpu.SMEM),
                  pl.BlockSpec((None, SW_HEADS, WINDOW, 2 * WINDOW), lambda b, n: (jnp.minimum(n, 1), 0, 0, 0)),
                  cur(SW_WIDTH), prv(KV_WIDTH), cur(KV_WIDTH), prv(KV_WIDTH), cur(KV_WIDTH)],
        out_specs=cur(SW_WIDTH),
        out_shape=jax.ShapeDtypeStruct((nb, t, SW_WIDTH), F32),
        compiler_params=_params("parallel", "parallel"),
        name="swa_prompt",
    )(sinks, bias, q, k, k, v, v)


def _swa_sample_kernel(sink_ref, slope_ref, q_ref, kn_ref, vn_ref, ck_ref, cv_ref,
                       o_ref, nk_ref, nv_ref, *, bb):
    last = lax.broadcasted_iota(jnp.int32, (WINDOW, KV_WIDTH), 0) == WINDOW - 1
    j = lax.broadcasted_iota(jnp.int32, (SW_HEADS, WINDOW), 1)
    bias = slope_ref[...] * (WINDOW - 1 - j).astype(F32)
    sink = sink_ref[...]
    seqs = range(bb)
    kw = [jnp.where(last, kn_ref[b], pltpu.roll(ck_ref[b], WINDOW - 1, 0)) for b in seqs]
    vw = [jnp.where(last, vn_ref[b], pltpu.roll(cv_ref[b], WINDOW - 1, 0)) for b in seqs]
    s = [_dot_nt(q_ref[b].astype(BF16), kw[b].astype(BF16)) * HEAD_DIM ** -0.5 - bias for b in seqs]
    m = [jnp.maximum(jnp.max(x, axis=-1, keepdims=True), sink) for x in s]
    e = [jnp.exp(x - mx) for x, mx in zip(s, m)]
    prob = [x / (jnp.sum(x, axis=-1, keepdims=True) + jnp.exp(sink - mx)) for x, mx in zip(e, m)]
    o = [_dot(prob[b].astype(BF16), vw[b].astype(BF16)) for b in seqs]
    for b in seqs:
        nk_ref[b] = kw[b]
        nv_ref[b] = vw[b]
        o_ref[b] = o[b]


def swa_sample(q, kn, vn, ck, cv, sinks, slopes, bb=8):
    nb = q.shape[0]
    assert nb % bb == 0
    own = (jnp.arange(SW_HEADS) // SW_GROUP)[:, None] == jnp.arange(SW_KV_HEADS)[None, :]
    q2 = jnp.where(own[None, :, :, None], q[:, :, None, :], 0.0).reshape(nb, SW_HEADS, KV_WIDTH)
    per_b = lambda s: pl.BlockSpec((bb,) + s, lambda b: (b, 0, 0))
    full = lambda a: pl.BlockSpec(a.shape, lambda b: (0, 0))
    o2, nk, nv = pl.pallas_call(
        functools.partial(_swa_sample_kernel, bb=bb),
        grid=(nb // bb,),
        in_specs=[full(sinks), full(slopes), per_b((SW_HEADS, KV_WIDTH)), per_b((1, KV_WIDTH)),
                  per_b((1, KV_WIDTH)), per_b((WINDOW, KV_WIDTH)), per_b((WINDOW, KV_WIDTH))],
        out_specs=[per_b((SW_HEADS, KV_WIDTH)), per_b((WINDOW, KV_WIDTH)), per_b((WINDOW, KV_WIDTH))],
        out_shape=[jax.ShapeDtypeStruct((nb, SW_HEADS, KV_WIDTH), F32),
                   jax.ShapeDtypeStruct((nb, WINDOW, KV_WIDTH), F32),
                   jax.ShapeDtypeStruct((nb, WINDOW, KV_WIDTH), F32)],
        compiler_params=_params("parallel"),
        name="swa_sample",
    )(sinks, slopes, q2, kn, vn, ck, cv)
    o = jnp.sum(jnp.where(own[None, :, :, None], o2.reshape(nb, SW_HEADS, SW_KV_HEADS, HEAD_DIM), 0.0), axis=2)
    return o, nk, nv


def _mem_block_kernel(x_ref, wq_ref, mk_ref, mv_ref, wo_ref, g_ref, b_ref, o_ref):
    x = x_ref[0]
    qb = _dot(x.astype(BF16), wq_ref[...]).astype(BF16)
    outs = []
    for h in range(MEM_HEADS):
        cols = slice(h * MEM_HEAD_DIM, (h + 1) * MEM_HEAD_DIM)
        s = _dot_nt(qb[:, cols], mk_ref[0, :, cols].astype(BF16)) * MEM_HEAD_DIM ** -0.5
        e = jnp.exp(s - jnp.max(s, axis=-1, keepdims=True))
        den = jnp.sum(e, axis=-1, keepdims=True)
        outs.append(_dot(e.astype(BF16), mv_ref[0, :, cols].astype(BF16)) / den)
    o = jnp.concatenate(outs, axis=1).astype(BF16)
    o_ref[0] = _layer_norm(ALPHA * x + _dot(o, wo_ref[...]), g_ref[...], b_ref[...])


def mem_block(x, wq, mk, mv, wo, g, b, tm=512):
    ng, t, d = x.shape
    tm = _row_tile(t, tm)
    row = pl.BlockSpec((1, tm, d), lambda gi, i: (gi, i, 0))
    mem = pl.BlockSpec((1, MEM_TOKENS, d), lambda gi, i: (gi, 0, 0))
    full = lambda a: pl.BlockSpec(a.shape, lambda gi, i: (0, 0))
    return pl.pallas_call(
        _mem_block_kernel,
        grid=(ng, t // tm),
        in_specs=[row, full(wq), mem, mem, full(wo), full(g), full(b)],
        out_specs=row,
        out_shape=jax.ShapeDtypeStruct((ng, t, d), F32),
        compiler_params=_params("parallel", "parallel"),
        name="mem_block",
    )(x, wq, mk, mv, wo, g, b)


def _mem_attn_token_kernel(q_ref, mk_ref, mv_ref, o_ref, *, bb):
    rows = MEM_TOKENS * MEM_HEADS
    lane = lax.broadcasted_iota(jnp.int32, (MEM_HEADS, rows), 1)
    head = lax.broadcasted_iota(jnp.int32, (MEM_HEADS, rows), 0)
    own = (lane % MEM_HEADS) == head
    for b in range(bb):
        q = q_ref[b]
        q4 = jnp.concatenate([q[:, h * MEM_HEAD_DIM:(h + 1) * MEM_HEAD_DIM] for h in range(MEM_HEADS)],
                             axis=0).astype(BF16)
        k2 = mk_ref[b].reshape(rows, MEM_HEAD_DIM).astype(BF16)
        v2 = mv_ref[b].reshape(rows, MEM_HEAD_DIM).astype(BF16)
        s = jnp.where(own, _dot_nt(q4, k2) * MEM_HEAD_DIM ** -0.5, -jnp.inf)
        m = jnp.max(s, axis=-1, keepdims=True)
        e = jnp.exp(s - m)
        prob = e / jnp.sum(e, axis=-1, keepdims=True)
        o = _dot(prob.astype(BF16), v2)
        for h in range(MEM_HEADS):
            o_ref[b, :, h * MEM_HEAD_DIM:(h + 1) * MEM_HEAD_DIM] = o[h:h + 1]


def mem_attn_token(q, mk, mv, layer, bb=4):
    ng, _, d = q.shape
    assert ng % bb == 0
    row = pl.BlockSpec((bb, 1, d), lambda g: (g, 0, 0))
    mem = pl.BlockSpec((None, bb, MEM_TOKENS, MEM_HEADS, MEM_HEAD_DIM), lambda g: (layer, g, 0, 0, 0))
    return pl.pallas_call(
        functools.partial(_mem_attn_token_kernel, bb=bb),
        grid=(ng // bb,),
        in_specs=[row, mem, mem],
        out_specs=row,
        out_shape=jax.ShapeDtypeStruct((ng, 1, d), F32),
        compiler_params=_params("parallel"),
        name="mem_attn_token",
    )(q, mk, mv)


def _pair_state(s):
    nb = s.shape[0]
    s = s.reshape(nb, RW_PAIRS, 2, HEAD_DIM, HEAD_DIM)
    return jnp.swapaxes(s, 2, 3).reshape(nb, RW_PAIRS, HEAD_DIM, LANES)


def _unpair_state(s):
    nb = s.shape[0]
    s = s.reshape(nb, RW_PAIRS, HEAD_DIM, 2, HEAD_DIM)
    return jnp.swapaxes(s, 2, 3).reshape(nb, RW_HEADS, HEAD_DIM, HEAD_DIM)


def kernel(x_prompt, x_sample, mem_prompt, state_wkv, state_shift, cache_win_k, cache_win_v,
           cache_mem_k, cache_mem_v, ln_g, ln_b, ffn_w1, ffn_w3, ffn_w2, w_in, rw_mu, rw_w0,
           rw_w_up, rw_a0, rw_a_up, rw_g_up, rw_k_k, rw_k_a, rw_r_k, rw_gn_g, rw_gn_b, sw_sinks,
           w_out, mem_wq, mem_wk, mem_wv, mem_wo):
    depth = ln_g.shape[0]
    bp, tp, d = x_prompt.shape
    bs, ts, _ = x_sample.shape
    assert ts == 1 and cache_win_k.shape[2] == WINDOW and tp % WINDOW == 0

    w1b, w3b, w2b = (w.astype(BF16) for w in (ffn_w1, ffn_w3, ffn_w2))
    w_in_b = w_in.astype(BF16)
    w_out_b = w_out.astype(BF16)
    wqb, wkb, wvb, wob = (w.astype(BF16) for w in (mem_wq, mem_wk, mem_wv, mem_wo))
    zpad = jnp.zeros((depth, D_W_LORA, RW_WIDTH), BF16)
    wup_b = jnp.concatenate([rw_w_up.astype(BF16), zpad], axis=1)
    aup_b = jnp.concatenate([zpad, rw_a_up.astype(BF16)], axis=1)
    gup_b = rw_g_up.astype(BF16)
    hid = jnp.arange(2 * RW_WIDTH) % RW_WIDTH // HEAD_DIM
    seg2 = (hid[:, None] == (jnp.arange(RW_WIDTH) // HEAD_DIM)[None, :]).astype(BF16)
    slopes = jnp.asarray([[_alibi_slope(h)] for h in range(SW_HEADS)], F32)
    q0, k0, v0 = RW_COLS, RW_COLS + SW_WIDTH, RW_COLS + SW_WIDTH + KV_WIDTH
    row = lambda a: a.reshape(1, -1)

    def layer(l, x, nb, t, prev_fn, s0, swa_fn, mem_fn):
        x = ffn_ln(x, w1b, w3b, w2b, (l, 0), row(ln_g[l, 0]), row(ln_b[l, 0]))
        p_rw, q, k, v = matmul_multi(
            x, [w_in_b[l, :, :q0], w_in_b[l, :, q0:k0], w_in_b[l, :, k0:v0], w_in_b[l, :, v0:]])
        *ops, g, bonus = rwkv_prep(
            p_rw, prev_fn(p_rw), row(rw_mu[l]), row(rw_w0[l]), wup_b[l], row(rw_a0[l]), aup_b[l],
            gup_b[l], row(rw_k_k[l]), row(rw_k_a[l]), row(rw_r_k[l]), seg2, seq_len=t)
        ops = [a.reshape(RW_PAIRS, nb, t, LANES) for a in ops]
        if t > 1:
            y, s_fin = wkv_chunked(*ops, s0)
        else:
            y, s_fin = wkv_step(*ops, s0)
        y_sw, win_k, win_v = swa_fn(q, k, v)
        shift = x.reshape(nb, t, d)[:, -1]
        x = mix_out_ln(x, y.reshape(RW_PAIRS, nb * t, LANES), bonus, g, y_sw, row(rw_gn_g[l]),
                       row(rw_gn_b[l]), seg2, w_out_b[l, :RW_WIDTH], w_out_b[l, RW_WIDTH:],
                       row(ln_g[l, 1]), row(ln_b[l, 1]))
        x = mem_fn(x)
        x = ffn_ln(x, w1b, w3b, w2b, (l, 1), row(ln_g[l, 3]), row(ln_b[l, 3]))
        return x, _unpair_state(s_fin), shift, win_k, win_v

    xp = x_prompt.reshape(bp * tp, d)
    p_wkv, p_shift, p_wk, p_wv, p_mk, p_mv = [], [], [], [], [], []
    for l in range(depth):
        prev_prompt = lambda p_rw: p_rw

        def swa_p(q, k, v, l=l):
            k3 = k.reshape(bp, tp, KV_WIDTH)
            v3 = v.reshape(bp, tp, KV_WIDTH)
            y = swa_prompt(q.reshape(bp, tp, SW_WIDTH), k3, v3, sw_sinks[l])
            tail = lambda a: a[:, -WINDOW:].reshape(bp, WINDOW, SW_KV_HEADS, HEAD_DIM)
            return y.reshape(bp * tp, SW_WIDTH), tail(k3), tail(v3)

        mk, mv = matmul_multi(mem_prompt.reshape(bp * MEM_TOKENS, d), [wkb[l], wvb[l]])
        mk = mk.reshape(bp, MEM_TOKENS, d)
        mv = mv.reshape(bp, MEM_TOKENS, d)

        def mem_p(x, l=l, mk=mk, mv=mv):
            return mem_block(x.reshape(bp, tp, d), wqb[l], mk, mv, wob[l], row(ln_g[l, 2]),
                             row(ln_b[l, 2])).reshape(bp * tp, d)

        s0 = jnp.zeros((bp, RW_PAIRS, HEAD_DIM, LANES), F32)
        xp, s_fin, shift, wk_, wv_ = layer(l, xp, bp, tp, prev_prompt, s0, swa_p, mem_p)
        p_wkv.append(s_fin)
        p_shift.append(shift)
        p_wk.append(wk_)
        p_wv.append(wv_)
        p_mk.append(mk.reshape(bp, MEM_TOKENS, MEM_HEADS, MEM_HEAD_DIM))
        p_mv.append(mv.reshape(bp, MEM_TOKENS, MEM_HEADS, MEM_HEAD_DIM))

    xs = x_sample.reshape(bs, d)
    s_wkv, s_shift, s_wk, s_wv = [], [], [], []
    for l in range(depth):
        def prev_sample(p_rw, l=l):
            (prev,) = matmul_multi(state_shift[l], [w_in_b[l, :, :q0]])
            return prev

        def swa_s(q, k, v, l=l):
            o, nk, nv = swa_sample(
                q.reshape(bs, SW_HEADS, HEAD_DIM), k.reshape(bs, 1, KV_WIDTH), v.reshape(bs, 1, KV_WIDTH),
                cache_win_k[l].reshape(bs, WINDOW, KV_WIDTH), cache_win_v[l].reshape(bs, WINDOW, KV_WIDTH),
                sw_sinks[l].reshape(SW_HEADS, 1), slopes)
            unflat = lambda a: a.reshape(bs, WINDOW, SW_KV_HEADS, HEAD_DIM)
            return o.reshape(bs, SW_WIDTH), unflat(nk), unflat(nv)

        def mem_s(x, l=l):
            (qm,) = matmul_multi(x, [wqb[l]])
            o = mem_attn_token(qm.reshape(bs, 1, d), cache_mem_k, cache_mem_v, l).reshape(bs, d)
            return proj_ln(x, [o], [wob[l]], row(ln_g[l, 2]), row(ln_b[l, 2]))

        xs, s_fin, shift, wk_, wv_ = layer(l, xs, bs, 1, prev_sample, _pair_state(state_wkv[l]),
                                           swa_s, mem_s)
        s_wkv.append(s_fin)
        s_shift.append(shift)
        s_wk.append(wk_)
        s_wv.append(wv_)

    return (xp.reshape(bp, tp, d), xs.reshape(bs, 1, d),
            jnp.stack(p_wkv), jnp.stack(p_shift), jnp.stack(p_wk), jnp.stack(p_wv),
            jnp.stack(p_mk), jnp.stack(p_mv),
            jnp.stack(s_wkv), jnp.stack(s_shift), jnp.stack(s_wk), jnp.stack(s_wv))
```

```python
import functools

import jax
import jax.numpy as jnp
from jax import lax
from jax.experimental import pallas as pl
from jax.experimental.pallas import tpu as pltpu

F32 = jnp.float32
BF16 = jnp.bfloat16

D_MODEL = 1024
HEAD_DIM = 64
RW_WIDTH = 512
RW_HEADS = 8
RW_PAIRS = RW_HEADS // 2
SW_WIDTH = 512
SW_HEADS = 8
SW_KV_HEADS = 2
SW_GROUP = SW_HEADS // SW_KV_HEADS
KV_WIDTH = SW_KV_HEADS * HEAD_DIM
WINDOW = 128
D_W_LORA = 64
D_A_LORA = 64
D_G_LORA = 128
RW_COLS = 3 * RW_WIDTH + D_W_LORA + D_A_LORA + D_G_LORA
LORA_WA_START = 3 * RW_WIDTH
LORA_G_START = LORA_WA_START + D_W_LORA + D_A_LORA
MEM_TOKENS = 256
MEM_HEADS = 4
MEM_HEAD_DIM = D_MODEL // MEM_HEADS
D_FF = 2816
DEPTH = 4
ALPHA = (2.0 * DEPTH) ** 0.25
LN_EPS = 1e-5
GN_EPS = 64e-5
NORM_EPS = 1e-12

CHUNK = 64
FF_TILE = 256
LANES = 128
VMEM_LIMIT = 56 * 1024 * 1024


def _params(*semantics):
    return pltpu.CompilerParams(dimension_semantics=semantics, vmem_limit_bytes=VMEM_LIMIT)


def _row_tile(n, want):
    return want if n % want == 0 else n


def _layer_norm(z, g, b):
    mu = jnp.mean(z, axis=-1, keepdims=True)
    d = z - mu
    var = jnp.mean(d * d, axis=-1, keepdims=True)
    return d * lax.rsqrt(var + LN_EPS) * g + b


def _sigmoid(x):
    return 1.0 / (1.0 + jnp.exp(-x))


def _split3(x):
    hi = x.astype(BF16)
    r1 = x - hi.astype(F32)
    mid = r1.astype(BF16)
    lo = (r1 - mid.astype(F32)).astype(BF16)
    return jnp.concatenate([hi, mid, lo], axis=1)


def _split2(x):
    hi = x.astype(BF16)
    lo = (x - hi.astype(F32)).astype(BF16)
    return jnp.concatenate([hi, lo], axis=1)


def _mm_kernel(x_ref, *refs, n_out):
    xb = x_ref[...].astype(BF16)
    for w_ref, o_ref in zip(refs[:n_out], refs[n_out:]):
        o_ref[...] = jnp.dot(xb, w_ref[...], preferred_element_type=F32)


def matmul_multi(x, ws, tm=512):
    n, k = x.shape
    tm = _row_tile(n, tm)
    n_out = len(ws)
    return pl.pallas_call(
        functools.partial(_mm_kernel, n_out=n_out),
        grid=(n // tm,),
        in_specs=[pl.BlockSpec((tm, k), lambda i: (i, 0))]
        + [pl.BlockSpec(w.shape, lambda i: (0, 0)) for w in ws],
        out_specs=[pl.BlockSpec((tm, w.shape[1]), lambda i: (i, 0)) for w in ws],
        out_shape=[jax.ShapeDtypeStruct((n, w.shape[1]), F32) for w in ws],
        compiler_params=_params("parallel"),
        name="matmul_multi",
    )(x, *ws)


def _ffn_kernel(x_ref, w1_ref, w3_ref, w2_ref, g_ref, b_ref, o_ref, xb_ref, acc_ref, *, n_ff):
    xb_ref[...] = x_ref[...].astype(BF16)

    def part(c):
        cols = slice(c * FF_TILE, (c + 1) * FF_TILE)
        xb = xb_ref[...]
        h1 = jnp.dot(xb, w1_ref[:, cols], preferred_element_type=F32)
        h3 = jnp.dot(xb, w3_ref[:, cols], preferred_element_type=F32)
        h = (h1 * _sigmoid(h1)) * h3
        return jnp.dot(h.astype(BF16), w2_ref[cols, :], preferred_element_type=F32)

    acc_ref[...] = part(0)
    for c in range(1, n_ff):
        acc_ref[...] += part(c)
    z = ALPHA * x_ref[...] + 0.5 * acc_ref[...]
    o_ref[...] = _layer_norm(z, g_ref[...], b_ref[...])


def ffn_ln(x, w1, w3, w2, lead, g, b, tm=1024):
    n, d = x.shape
    tm = _row_tile(n, tm)
    n_ff = D_FF // FF_TILE
    resident = lambda a: pl.BlockSpec((None,) * len(lead) + a.shape[len(lead):],
                                      lambda i: lead + (0, 0), pipeline_mode=pl.Buffered(1))
    return pl.pallas_call(
        functools.partial(_ffn_kernel, n_ff=n_ff),
        grid=(n // tm,),
        in_specs=[
            pl.BlockSpec((tm, d), lambda i: (i, 0)),
            resident(w1), resident(w3), resident(w2),
            pl.BlockSpec((1, d), lambda i: (0, 0)),
            pl.BlockSpec((1, d), lambda i: (0, 0)),
        ],
        out_specs=pl.BlockSpec((tm, d), lambda i: (i, 0)),
        out_shape=jax.ShapeDtypeStruct((n, d), F32),
        scratch_shapes=[pltpu.VMEM((tm, d), BF16), pltpu.VMEM((tm, d), F32)],
        compiler_params=_params("parallel"),
        name="ffn_ln",
    )(x, w1, w3, w2, g, b)


def _proj_ln_kernel(x_ref, *refs, n_in):
    a_refs = refs[:n_in]
    w_refs = refs[n_in:2 * n_in]
    g_ref, b_ref, o_ref = refs[2 * n_in:]
    f = None
    for a_ref, w_ref in zip(a_refs, w_refs):
        t = jnp.dot(a_ref[...].astype(BF16), w_ref[...], preferred_element_type=F32)
        f = t if f is None else f + t
    o_ref[...] = _layer_norm(ALPHA * x_ref[...] + f, g_ref[...], b_ref[...])


def proj_ln(x, acts, ws, g, b, tm=512):
    n, d = x.shape
    tm = _row_tile(n, tm)
    n_in = len(acts)
    return pl.pallas_call(
        functools.partial(_proj_ln_kernel, n_in=n_in),
        grid=(n // tm,),
        in_specs=[pl.BlockSpec((tm, d), lambda i: (i, 0))]
        + [pl.BlockSpec((tm, a.shape[1]), lambda i: (i, 0)) for a in acts]
        + [pl.BlockSpec(w.shape, lambda i: (0, 0)) for w in ws]
        + [pl.BlockSpec((1, d), lambda i: (0, 0))] * 2,
        out_specs=pl.BlockSpec((tm, d), lambda i: (i, 0)),
        out_shape=jax.ShapeDtypeStruct((n, d), F32),
        compiler_params=_params("parallel"),
        name="proj_ln",
    )(x, *acts, *ws, g, b)


def _rwkv_prep_kernel(p_ref, prev_ref, mu_ref, w0_ref, wup_ref, a0_ref, aup_ref, gup_ref,
                      kk_ref, ka_ref, rk_ref, seg_ref, *refs, chunked, tiles_per_seq):
    if chunked:
        tril_ref, ones_ref = refs[:2]
        refs = refs[2:]
    g_out, bonus_out = refs[-2:]
    p = p_ref[...]
    if chunked:
        first = (pl.program_id(0) % tiles_per_seq) == 0
        above = jnp.where(first, 0.0, prev_ref[7:8, :])
        top = lax.broadcasted_iota(jnp.int32, p.shape, 0) == 0
        prev = jnp.where(top, above, pltpu.roll(p, 1, 0))
    else:
        prev = prev_ref[...]
    xm = p + (prev - p) * mu_ref[...]
    r = xm[:, 0:RW_WIDTH]
    k = xm[:, RW_WIDTH:2 * RW_WIDTH]
    v = xm[:, 2 * RW_WIDTH:3 * RW_WIDTH]
    wa = xm[:, LORA_WA_START:LORA_G_START]
    gl = xm[:, LORA_G_START:RW_COLS]

    def seg_sum(x):
        return jnp.dot(_split2(x), seg_ref[...], preferred_element_type=F32)

    lw = jnp.dot(jnp.tanh(wa).astype(BF16), wup_ref[...], preferred_element_type=F32)
    la = jnp.dot(wa.astype(BF16), aup_ref[...], preferred_element_type=F32)
    z = -(w0_ref[...] + lw)
    softplus = jnp.maximum(z, 0.0) + jnp.log(1.0 + jnp.exp(-jnp.abs(z)))
    w_log = -softplus - 0.5
    log_decay = -jnp.exp(w_log)
    a = _sigmoid(a0_ref[...] + la)
    g = jnp.dot(_sigmoid(gl).astype(BF16), gup_ref[...], preferred_element_type=F32)
    kk = k * kk_ref[...]
    nrm = jnp.sqrt(seg_sum(kk * kk))
    kk = kk / jnp.maximum(nrm, NORM_EPS)
    k_mod = k * (1.0 + (a - 1.0) * ka_ref[...])
    kka = kk * a
    if chunked:
        parts = _split3(log_decay)

        def time_sum(m_ref):
            s3 = jnp.dot(m_ref[...], parts, preferred_element_type=F32)
            return s3[:, :RW_WIDTH] + s3[:, RW_WIDTH:2 * RW_WIDTH] + s3[:, 2 * RW_WIDTH:]

        cum = time_sum(tril_ref)
        tot = time_sum(ones_ref)
        grow = jnp.exp(-cum)
        rest = jnp.exp(tot - cum)
        outs = (kk * jnp.exp(cum - log_decay), r * jnp.exp(cum), -kka * grow, k_mod * grow,
                -kka * rest, k_mod * rest, v, jnp.exp(tot))
    else:
        outs = (r, jnp.exp(log_decay), k_mod, v, kk, kka)
    for o_ref, val in zip(refs, outs):
        for pr in range(RW_PAIRS):
            o_ref[pr] = val[:, pr * LANES:(pr + 1) * LANES].astype(o_ref.dtype)
    g_out[...] = g
    bonus_out[...] = seg_sum(r * k_mod * rk_ref[...]) * v


def rwkv_prep(p, prev, mu, w0, wup, a0, aup, gup, k_k, k_a, r_k, seg, seq_len, tm=256):
    n = p.shape[0]
    chunked = seq_len > 1
    tm = _row_tile(n, tm)
    row = lambda c: pl.BlockSpec((tm, c), lambda i: (i, 0))
    prev_spec = row(RW_COLS)
    if chunked:
        assert seq_len % tm == 0
        prev_spec = pl.BlockSpec((8, RW_COLS), lambda i: (jnp.maximum(i * (tm // 8) - 1, 0), 0))
    full = lambda a: pl.BlockSpec(a.shape, lambda i: (0, 0))
    consts = (mu, w0, wup, a0, aup, gup, k_k, k_a, r_k, seg)
    dtypes = [F32] * 6
    if chunked:
        assert tm % CHUNK == 0
        t_idx = jnp.arange(tm)
        same = (t_idx[:, None] // CHUNK) == (t_idx[None, :] // CHUNK)
        consts += ((same & (t_idx[None, :] <= t_idx[:, None])).astype(BF16), same.astype(BF16))
        dtypes = [BF16] * 7 + [F32]
    pair = pl.BlockSpec((RW_PAIRS, tm, LANES), lambda i: (0, i, 0))
    return pl.pallas_call(
        functools.partial(_rwkv_prep_kernel, chunked=chunked, tiles_per_seq=max(seq_len // tm, 1)),
        grid=(n // tm,),
        in_specs=[row(RW_COLS), prev_spec] + [full(c) for c in consts],
        out_specs=[pair] * len(dtypes) + [row(RW_WIDTH)] * 2,
        out_shape=[jax.ShapeDtypeStruct((RW_PAIRS, n, LANES), dt) for dt in dtypes]
        + [jax.ShapeDtypeStruct((n, RW_WIDTH), F32)] * 2,
        compiler_params=_params("parallel"),
        name="rwkv_prep",
    )(p, prev, *consts)


def _wkv_step_kernel(r_ref, w_ref, k_ref, v_ref, kk_ref, kka_ref, s0_ref, y_ref, s_ref, *, bb):
    sub = lax.broadcasted_iota(jnp.int32, (HEAD_DIM, LANES), 0)
    lane = lax.broadcasted_iota(jnp.int32, (HEAD_DIM, LANES), 1)
    diag = (lane % HEAD_DIM) == sub
    rj = lax.broadcasted_iota(jnp.int32, (2 * LANES, LANES), 0)
    cl = lax.broadcasted_iota(jnp.int32, (2 * LANES, LANES), 1)
    ones2 = jnp.where(((rj % LANES) // HEAD_DIM) == (cl // HEAD_DIM), 1.0, 0.0).astype(BF16)

    def head_sums(xs):
        out = jnp.dot(_split2(jnp.concatenate(xs, axis=0)), ones2, preferred_element_type=F32)
        return [out[i * HEAD_DIM:(i + 1) * HEAD_DIM] for i in range(len(xs))]

    pairs = [(b, p) for b in range(bb) for p in range(RW_PAIRS)]
    n = len(pairs)
    row = lambda ref, i: ref[pairs[i][1], pairs[i][0]]
    tiles = lambda x: x.reshape(HEAD_DIM // 8, 8, LANES)
    flat = lambda x: x.reshape(HEAD_DIM, LANES)
    s = [tiles(s0_ref[b, p]) for b, p in pairs]
    red = head_sums([flat(s[i] * row(kk_ref, i)) for i in range(n)]
                    + [flat(jnp.where(tiles(diag), row(v_ref, i), 0.0)) for i in range(n)])
    s = [s[i] * row(w_ref, i) - tiles(red[i]) * row(kka_ref, i) + tiles(red[n + i]) * row(k_ref, i)
         for i in range(n)]
    yb = head_sums([flat(s[i] * row(r_ref, i)) for i in range(n)])
    for i, (b, p) in enumerate(pairs):
        s_ref[b, p] = flat(s[i])
        y_ref[p, b] = jnp.sum(jnp.where(diag, yb[i], 0.0), axis=0, keepdims=True)


def wkv_step(r, w, k, v, kk, kka, s0, bb=8):
    _, nb, t, _ = r.shape
    assert t == 1 and nb % bb == 0
    seq = pl.BlockSpec((RW_PAIRS, bb, 1, LANES), lambda i: (0, i, 0, 0))
    st = pl.BlockSpec((bb, RW_PAIRS, HEAD_DIM, LANES), lambda i: (i, 0, 0, 0))
    return pl.pallas_call(
        functools.partial(_wkv_step_kernel, bb=bb),
        grid=(nb // bb,),
        in_specs=[seq] * 6 + [st],
        out_specs=[seq, st],
        out_shape=[jax.ShapeDtypeStruct((RW_PAIRS, nb, 1, LANES), F32),
                   jax.ShapeDtypeStruct(s0.shape, F32)],
        compiler_params=_params("parallel"),
        name="wkv_step",
    )(r, w, k, v, kk, kka, s0)


def _dot_nt(a, b):
    return lax.dot_general(a, b, (((1,), (1,)), ((), ())), preferred_element_type=F32)


def _dot_tn(a, b):
    return lax.dot_general(a, b, (((0,), (0,)), ((), ())), preferred_element_type=F32)


def _dot(a, b):
    return jnp.dot(a, b, preferred_element_type=F32)


def _wkv_chunk_kernel(qt_ref, rt_ref, bh_ref, kh_ref, bb_ref, kb_ref, v_ref, gc_ref, s0_ref,
                      y_ref, s_ref, *, nb, tc):
    @pl.when(pl.program_id(1) == 0)
    def _():
        s_ref[...] = s0_ref[...]

    c = CHUNK
    assert c == HEAD_DIM and 2 * c == LANES
    row = lax.broadcasted_iota(jnp.int32, (c, LANES), 0)
    lane = lax.broadcasted_iota(jnp.int32, (c, LANES), 1)
    head_a = lane < HEAD_DIM
    head_a2 = lax.broadcasted_iota(jnp.int32, (2 * c, LANES), 1) < HEAD_DIM
    strict = (lane % c) < row
    eye = jnp.where((lane % c) == row, 1.0, 0.0)
    row4 = lax.broadcasted_iota(jnp.int32, (c, 2 * LANES), 0)
    col4 = lax.broadcasted_iota(jnp.int32, (c, 2 * LANES), 1)
    kh_cols = (col4 >= c) & (col4 < 3 * c)
    m_mask = kh_cols & ((col4 % c) < row4)
    incl = (col4 % c) <= row4
    pairs = [(b, p) for b in range(nb) for p in range(RW_PAIRS)]
    bf = lambda x: x.astype(BF16)
    stack = lambda *xs: jnp.concatenate(xs, axis=0)
    zero = jnp.zeros((), BF16)
    only_a = lambda x: jnp.where(head_a if x.shape[0] == c else head_a2, x, zero)
    only_b = lambda x: jnp.where(head_a if x.shape[0] == c else head_a2, zero, x)
    per_head = lambda x: stack(only_a(x), only_b(x))
    zeros_c = jnp.zeros((c, LANES), BF16)

    def inverse_stages(ci, ready):
        ld = lambda ref, b, p: ref[p, b, pl.ds(ci * c, c), :]
        qr = [stack(ld(qt_ref, b, p), ld(rt_ref, b, p)) for b, p in pairs]
        bh = [ld(bh_ref, b, p) for b, p in pairs]
        kh = [ld(kh_ref, b, p) for b, p in pairs]
        e = [_dot_nt(q, stack(only_a(stack(x, y)), only_b(stack(y, x)))) for q, x, y in zip(qr, bh, kh)]
        yield
        pw = [jnp.where(strict, jnp.where(head_a, x[:c, :LANES], x[:c, LANES:]), 0.0) for x in e]
        inv = [eye + n for n in pw]
        for _ in range(5):
            pw = [_dot(x, per_head(x)) for x in [bf(x) for x in pw]]
            yield
            inv = [t + _dot(bf(x), per_head(bf(t))) for x, t in zip(pw, inv)]
            yield
        ready[ci] = (qr, e, inv)

    def state_stages(ci, ready):
        qr, e, inv = ready.pop(ci)
        ld = lambda ref, b, p: ref[p, b, pl.ds(ci * c, c), :]
        s = [s_ref[b, p] for b, p in pairs]
        ff = [_dot_nt(q, per_head(bf(x))) for q, x in zip(qr, s)]
        v = [ld(v_ref, b, p) for b, p in pairs]
        yield
        g = [f[:c] + _dot(bf(jnp.where(m_mask, x[:c], 0.0)), stack(zeros_c, only_a(w), only_b(w), zeros_c))
             for f, x, w in zip(ff, e, v)]
        yield
        u = [bf(_dot(bf(t), per_head(bf(x)))) for t, x in zip(inv, g)]
        yield
        y = [f[c:] + _dot(bf(jnp.where(incl, x[c:], 0.0)), stack(only_a(uu), only_a(w), only_b(w), only_b(uu)))
             for f, x, uu, w in zip(ff, e, u, v)]
        add = [_dot_tn(stack(uu, w), stack(ld(bb_ref, b, p), ld(kb_ref, b, p)))
               for uu, w, (b, p) in zip(u, v, pairs)]
        yield
        for i, (b, p) in enumerate(pairs):
            y_ref[p, b, pl.ds(ci * c, c), :] = y[i]
            gc = gc_ref[p, b, pl.ds(ci * c, 8), :]
            kept = (s[i].reshape(c // 8, 8, LANES) * gc).reshape(c, LANES)
            s_ref[b, p] = kept + jnp.where(head_a, add[i][:c], add[i][c:])
        yield

    n_chunks = tc // c
    state_len, inv_len = 5, 11
    ready, inverses, state = {}, {}, None
    for tick in range(-inv_len, state_len * n_chunks):
        for ci in range(n_chunks):
            if ci not in inverses and state_len * ci - inv_len <= tick:
                inverses[ci] = inverse_stages(ci, ready)
        for gen in inverses.values():
            next(gen, None)
        if tick >= 0:
            if tick % state_len == 0:
                state = state_stages(tick // state_len, ready)
            next(state)


def wkv_chunked(qt, rt, bh, kh, bb, kb, v, gc, s0, tc=512):
    _, nb, t, _ = qt.shape
    seq = pl.BlockSpec((RW_PAIRS, nb, tc, LANES), lambda i, j: (0, 0, j, 0))
    st = pl.BlockSpec((nb, RW_PAIRS, HEAD_DIM, LANES), lambda i, j: (0, 0, 0, 0))
    return pl.pallas_call(
        functools.partial(_wkv_chunk_kernel, nb=nb, tc=tc),
        grid=(1, t // tc),
        in_specs=[seq] * 8 + [st],
        out_specs=[seq, st],
        out_shape=[jax.ShapeDtypeStruct((RW_PAIRS, nb, t, LANES), F32),
                   jax.ShapeDtypeStruct(s0.shape, F32)],
        compiler_params=_params("arbitrary", "arbitrary"),
        name="wkv_chunked",
    )(qt, rt, bh, kh, bb, kb, v, gc, s0)


def _mix_out_kernel(x_ref, y_ref, bonus_ref, g_ref, ysw_ref, gng_ref, gnb_ref, seg_ref, wrw_ref, wsw_ref,
                    lng_ref, lnb_ref, o_ref):
    def seg_mean(x):
        return jnp.dot(_split2(x), seg_ref[...], preferred_element_type=F32) * (1.0 / HEAD_DIM)

    y = jnp.concatenate([y_ref[pr] for pr in range(RW_PAIRS)], axis=1)
    d = y - seg_mean(y)
    var = seg_mean(d * d)
    yn = d * lax.rsqrt(var + GN_EPS) * gng_ref[...] + gnb_ref[...]
    y_rw = (yn + bonus_ref[...]) * g_ref[...]
    f = _dot(y_rw.astype(BF16), wrw_ref[...]) + _dot(ysw_ref[...].astype(BF16), wsw_ref[...])
    o_ref[...] = _layer_norm(ALPHA * x_ref[...] + f, lng_ref[...], lnb_ref[...])


def mix_out_ln(x, y, bonus, g, y_sw, gn_g, gn_b, seg, w_rw, w_sw, ln_g, ln_b, tm=512):
    n, d = x.shape
    tm = _row_tile(n, tm)
    row = lambda c: pl.BlockSpec((tm, c), lambda i: (i, 0))
    full = lambda a: pl.BlockSpec(a.shape, lambda i: (0, 0))
    consts = (gn_g, gn_b, seg, w_rw, w_sw, ln_g, ln_b)
    return pl.pallas_call(
        _mix_out_kernel,
        grid=(n // tm,),
        in_specs=[row(d), pl.BlockSpec((RW_PAIRS, tm, LANES), lambda i: (0, i, 0)), row(RW_WIDTH),
                  row(RW_WIDTH), row(SW_WIDTH)] + [full(c) for c in consts],
        out_specs=row(d),
        out_shape=jax.ShapeDtypeStruct((n, d), F32),
        compiler_params=_params("parallel"),
        name="mix_out_ln",
    )(x, y, bonus, g, y_sw, *consts)


def _alibi_slope(h):
    return 2.0 ** (-8.0 * (h + 1) / SW_HEADS)


def _swa_prompt_kernel(sink_ref, bias0_ref, bias1_ref, q_ref, kp_ref, kc_ref, vp_ref, vc_ref, o_ref, *, qb_per_step):
    qb = (q_ref[0] * HEAD_DIM ** -0.5).astype(BF16)
    kw = jnp.concatenate([kp_ref[0], kc_ref[0]], axis=0).astype(BF16)
    vw = jnp.concatenate([vp_ref[0], vc_ref[0]], axis=0).astype(BF16)
    kx = pltpu.roll(kw, HEAD_DIM, 1)
    vx = pltpu.roll(vw, HEAD_DIM, 1)
    low_kv = lax.broadcasted_iota(jnp.int32, kw.shape, 1) < HEAD_DIM
    low_q = lax.broadcasted_iota(jnp.int32, (WINDOW, LANES), 1) < HEAD_DIM
    zero = jnp.zeros((), BF16)
    kc = [jnp.where(low_kv, kw, kx), jnp.where(low_kv, kx, kw)]
    vc = [jnp.where(low_kv, vw, vx), jnp.where(low_kv, vx, vw)]
    items = [(j, h) for j in range(qb_per_step) for h in range(SW_HEADS)]
    rows = lambda j: slice(j * WINDOW, (j + 1) * WINDOW)
    keys = lambda j: slice(j * WINDOW, (j + 2) * WINDOW)
    bias = lambda j, h: bias0_ref[h] if j == 0 else bias1_ref[h]
    qh = []
    for j, h in items:
        q2 = qb[rows(j), (h // 2) * LANES:(h // 2 + 1) * LANES]
        qh.append(jnp.where(low_q, q2, zero) if h % 2 == 0 else jnp.where(low_q, zero, q2))
    logits = [_dot_nt(x, kc[h // SW_GROUP][keys(j)]) + bias(j, h) for x, (j, h) in zip(qh, items)]
    m = [jnp.maximum(jnp.max(x, axis=-1, keepdims=True), sink_ref[h]) for x, (j, h) in zip(logits, items)]
    e = [jnp.exp(x - mx) for x, mx in zip(logits, m)]
    den = [jnp.sum(x, axis=-1, keepdims=True) + jnp.exp(sink_ref[h] - mx) for x, mx, (j, h) in zip(e, m, items)]
    o = [_dot(x.astype(BF16), vc[h // SW_GROUP][keys(j)]) / dn for x, dn, (j, h) in zip(e, den, items)]
    for i in range(0, len(items), 2):
        j, h = items[i]
        o_ref[0, rows(j), (h // 2) * LANES:(h // 2 + 1) * LANES] = jnp.where(low_q, o[i], o[i + 1])


def swa_prompt(q, k, v, sinks, qb_per_step=8):
    nb, t, _ = q.shape
    rows = qb_per_step * WINDOW
    assert t % rows == 0
    i = jnp.arange(WINDOW)[:, None]
    j = jnp.arange(2 * WINDOW)[None, :]
    dist = WINDOW + i - j
    valid = (dist >= 0) & (dist < WINDOW)
    slopes = jnp.asarray([_alibi_slope(h) for h in range(SW_HEADS)], F32)[:, None, None]
    table = lambda ok: jnp.where(ok[None], -slopes * dist.astype(F32)[None], -jnp.inf)
    bias = jnp.stack([table(valid & (j >= WINDOW)), table(valid)])
    table_spec = lambda pick: pl.BlockSpec((None, SW_HEADS, WINDOW, 2 * WINDOW), lambda b, n: (pick(n), 0, 0, 0))
    cur = lambda w: pl.BlockSpec((1, rows, w), lambda b, n: (b, n, 0))
    prv = lambda w: pl.BlockSpec((1, WINDOW, w), lambda b, n: (b, jnp.maximum(n * qb_per_step - 1, 0), 0))
    return pl.pallas_call(
        functools.partial(_swa_prompt_kernel, qb_per_step=qb_per_step),
        grid=(nb, t // rows),
        in_specs=[pl.BlockSpec(memory_space=pltpu.SMEM),
                  table_spec(lambda n: jnp.minimum(n, 1)), table_spec(lambda n: 1),
                  cur(SW_WIDTH), prv(KV_WIDTH), cur(KV_WIDTH), prv(KV_WIDTH), cur(KV_WIDTH)],
        out_specs=cur(SW_WIDTH),
        out_shape=jax.ShapeDtypeStruct((nb, t, SW_WIDTH), F32),
        compiler_params=_params("parallel", "parallel"),
        name="swa_prompt",
    )(sinks, bias, bias, q, k, k, v, v)


def _swa_sample_kernel(sink_ref, slope_ref, q_ref, kn_ref, vn_ref, ck_ref, cv_ref,
                       o_ref, nk_ref, nv_ref, *, bb):
    last = lax.broadcasted_iota(jnp.int32, (WINDOW, KV_WIDTH), 0) == WINDOW - 1
    j = lax.broadcasted_iota(jnp.int32, (SW_HEADS, WINDOW), 1)
    bias = slope_ref[...] * (WINDOW - 1 - j).astype(F32)
    sink = sink_ref[...]
    seqs = range(bb)
    kw = [jnp.where(last, kn_ref[b], pltpu.roll(ck_ref[b], WINDOW - 1, 0)) for b in seqs]
    vw = [jnp.where(last, vn_ref[b], pltpu.roll(cv_ref[b], WINDOW - 1, 0)) for b in seqs]
    s = [_dot_nt(q_ref[b].astype(BF16), kw[b].astype(BF16)) * HEAD_DIM ** -0.5 - bias for b in seqs]
    m = [jnp.maximum(jnp.max(x, axis=-1, keepdims=True), sink) for x in s]
    e = [jnp.exp(x - mx) for x, mx in zip(s, m)]
    prob = [x / (jnp.sum(x, axis=-1, keepdims=True) + jnp.exp(sink - mx)) for x, mx in zip(e, m)]
    o = [_dot(prob[b].astype(BF16), vw[b].astype(BF16)) for b in seqs]
    for b in seqs:
        nk_ref[b] = kw[b]
        nv_ref[b] = vw[b]
        o_ref[b] = o[b]


def swa_sample(q, kn, vn, ck, cv, sinks, slopes, bb=8):
    nb = q.shape[0]
    assert nb % bb == 0
    own = (jnp.arange(SW_HEADS) // SW_GROUP)[:, None] == jnp.arange(SW_KV_HEADS)[None, :]
    q2 = jnp.where(own[None, :, :, None], q[:, :, None, :], 0.0).reshape(nb, SW_HEADS, KV_WIDTH)
    per_b = lambda s: pl.BlockSpec((bb,) + s, lambda b: (b, 0, 0))
    full = lambda a: pl.BlockSpec(a.shape, lambda b: (0, 0))
    o2, nk, nv = pl.pallas_call(
        functools.partial(_swa_sample_kernel, bb=bb),
        grid=(nb // bb,),
        in_specs=[full(sinks), full(slopes), per_b((SW_HEADS, KV_WIDTH)), per_b((1, KV_WIDTH)),
                  per_b((1, KV_WIDTH)), per_b((WINDOW, KV_WIDTH)), per_b((WINDOW, KV_WIDTH))],
        out_specs=[per_b((SW_HEADS, KV_WIDTH)), per_b((WINDOW, KV_WIDTH)), per_b((WINDOW, KV_WIDTH))],
        out_shape=[jax.ShapeDtypeStruct((nb, SW_HEADS, KV_WIDTH), F32),
                   jax.ShapeDtypeStruct((nb, WINDOW, KV_WIDTH), F32),
                   jax.ShapeDtypeStruct((nb, WINDOW, KV_WIDTH), F32)],
        compiler_params=_params("parallel"),
        name="swa_sample",
    )(sinks, slopes, q2, kn, vn, ck, cv)
    o = jnp.sum(jnp.where(own[None, :, :, None], o2.reshape(nb, SW_HEADS, SW_KV_HEADS, HEAD_DIM), 0.0), axis=2)
    return o, nk, nv


def _mem_block_kernel(x_ref, wq_ref, mk_ref, mv_ref, wo_ref, g_ref, b_ref, o_ref):
    x = x_ref[0]
    qb = _dot(x.astype(BF16), wq_ref[...]).astype(BF16)
    cols = [slice(h * MEM_HEAD_DIM, (h + 1) * MEM_HEAD_DIM) for h in range(MEM_HEADS)]
    s = [_dot_nt(qb[:, c], mk_ref[0, :, c].astype(BF16)) * MEM_HEAD_DIM ** -0.5 for c in cols]
    e = [jnp.exp(x - jnp.max(x, axis=-1, keepdims=True)) for x in s]
    den = [jnp.sum(x, axis=-1, keepdims=True) for x in e]
    outs = [_dot(x.astype(BF16), mv_ref[0, :, c].astype(BF16)) / d for x, c, d in zip(e, cols, den)]
    o = jnp.concatenate(outs, axis=1).astype(BF16)
    o_ref[0] = _layer_norm(ALPHA * x + _dot(o, wo_ref[...]), g_ref[...], b_ref[...])


def mem_block(x, wq, mk, mv, wo, g, b, tm=512):
    ng, t, d = x.shape
    tm = _row_tile(t, tm)
    row = pl.BlockSpec((1, tm, d), lambda gi, i: (gi, i, 0))
    mem = pl.BlockSpec((1, MEM_TOKENS, d), lambda gi, i: (gi, 0, 0))
    full = lambda a: pl.BlockSpec(a.shape, lambda gi, i: (0, 0))
    return pl.pallas_call(
        _mem_block_kernel,
        grid=(ng, t // tm),
        in_specs=[row, full(wq), mem, mem, full(wo), full(g), full(b)],
        out_specs=row,
        out_shape=jax.ShapeDtypeStruct((ng, t, d), F32),
        compiler_params=_params("parallel", "parallel"),
        name="mem_block",
    )(x, wq, mk, mv, wo, g, b)


def _mem_attn_token_kernel(q_ref, mk_ref, mv_ref, o_ref, *, bb):
    rows = MEM_TOKENS * MEM_HEADS
    lane = lax.broadcasted_iota(jnp.int32, (MEM_HEADS, rows), 1)
    head = lax.broadcasted_iota(jnp.int32, (MEM_HEADS, rows), 0)
    own = (lane % MEM_HEADS) == head
    for b in range(bb):
        q = q_ref[b]
        q4 = jnp.concatenate([q[:, h * MEM_HEAD_DIM:(h + 1) * MEM_HEAD_DIM] for h in range(MEM_HEADS)],
                             axis=0).astype(BF16)
        k2 = mk_ref[b].reshape(rows, MEM_HEAD_DIM).astype(BF16)
        v2 = mv_ref[b].reshape(rows, MEM_HEAD_DIM).astype(BF16)
        s = jnp.where(own, _dot_nt(q4, k2) * MEM_HEAD_DIM ** -0.5, -jnp.inf)
        m = jnp.max(s, axis=-1, keepdims=True)
        e = jnp.exp(s - m)
        prob = e / jnp.sum(e, axis=-1, keepdims=True)
        o = _dot(prob.astype(BF16), v2)
        for h in range(MEM_HEADS):
            o_ref[b, :, h * MEM_HEAD_DIM:(h + 1) * MEM_HEAD_DIM] = o[h:h + 1]


def mem_attn_token(q, mk, mv, layer, bb=4):
    ng, _, d = q.shape
    assert ng % bb == 0
    row = pl.BlockSpec((bb, 1, d), lambda g: (g, 0, 0))
    mem = pl.BlockSpec((None, bb, MEM_TOKENS, MEM_HEADS, MEM_HEAD_DIM), lambda g: (layer, g, 0, 0, 0))
    return pl.pallas_call(
        functools.partial(_mem_attn_token_kernel, bb=bb),
        grid=(ng // bb,),
        in_specs=[row, mem, mem],
        out_specs=row,
        out_shape=jax.ShapeDtypeStruct((ng, 1, d), F32),
        compiler_params=_params("parallel"),
        name="mem_attn_token",
    )(q, mk, mv)


def _pair_state(s):
    nb = s.shape[0]
    s = s.reshape(nb, RW_PAIRS, 2, HEAD_DIM, HEAD_DIM)
    return jnp.swapaxes(s, 2, 3).reshape(nb, RW_PAIRS, HEAD_DIM, LANES)


def _unpair_state(s):
    nb = s.shape[0]
    s = s.reshape(nb, RW_PAIRS, HEAD_DIM, 2, HEAD_DIM)
    return jnp.swapaxes(s, 2, 3).reshape(nb, RW_HEADS, HEAD_DIM, HEAD_DIM)


def kernel(x_prompt, x_sample, mem_prompt, state_wkv, state_shift, cache_win_k, cache_win_v,
           cache_mem_k, cache_mem_v, ln_g, ln_b, ffn_w1, ffn_w3, ffn_w2, w_in, rw_mu, rw_w0,
           rw_w_up, rw_a0, rw_a_up, rw_g_up, rw_k_k, rw_k_a, rw_r_k, rw_gn_g, rw_gn_b, sw_sinks,
           w_out, mem_wq, mem_wk, mem_wv, mem_wo):
    depth = ln_g.shape[0]
    bp, tp, d = x_prompt.shape
    bs, ts, _ = x_sample.shape
    assert ts == 1 and cache_win_k.shape[2] == WINDOW and tp % WINDOW == 0

    w1b, w3b, w2b = (w.astype(BF16) for w in (ffn_w1, ffn_w3, ffn_w2))
    w_in_b = w_in.astype(BF16)
    w_out_b = w_out.astype(BF16)
    wqb, wkb, wvb, wob = (w.astype(BF16) for w in (mem_wq, mem_wk, mem_wv, mem_wo))
    zpad = jnp.zeros((depth, D_W_LORA, RW_WIDTH), BF16)
    wup_b = jnp.concatenate([rw_w_up.astype(BF16), zpad], axis=1)
    aup_b = jnp.concatenate([zpad, rw_a_up.astype(BF16)], axis=1)
    gup_b = rw_g_up.astype(BF16)
    hid = jnp.arange(2 * RW_WIDTH) % RW_WIDTH // HEAD_DIM
    seg2 = (hid[:, None] == (jnp.arange(RW_WIDTH) // HEAD_DIM)[None, :]).astype(BF16)
    slopes = jnp.asarray([[_alibi_slope(h)] for h in range(SW_HEADS)], F32)
    q0, k0, v0 = RW_COLS, RW_COLS + SW_WIDTH, RW_COLS + SW_WIDTH + KV_WIDTH
    row = lambda a: a.reshape(1, -1)

    def layer(l, x, nb, t, prev_fn, s0, swa_fn, mem_fn):
        x = ffn_ln(x, w1b, w3b, w2b, (l, 0), row(ln_g[l, 0]), row(ln_b[l, 0]))
        p_rw, q, k, v = matmul_multi(
            x, [w_in_b[l, :, :q0], w_in_b[l, :, q0:k0], w_in_b[l, :, k0:v0], w_in_b[l, :, v0:]])
        *ops, g, bonus = rwkv_prep(
            p_rw, prev_fn(p_rw), row(rw_mu[l]), row(rw_w0[l]), wup_b[l], row(rw_a0[l]), aup_b[l],
            gup_b[l], row(rw_k_k[l]), row(rw_k_a[l]), row(rw_r_k[l]), seg2, seq_len=t)
        ops = [a.reshape(RW_PAIRS, nb, t, LANES) for a in ops]
        if t > 1:
            y, s_fin = wkv_chunked(*ops, s0)
        else:
            y, s_fin = wkv_step(*ops, s0)
        y_sw, win_k, win_v = swa_fn(q, k, v)
        shift = x.reshape(nb, t, d)[:, -1]
        x = mix_out_ln(x, y.reshape(RW_PAIRS, nb * t, LANES), bonus, g, y_sw, row(rw_gn_g[l]),
                       row(rw_gn_b[l]), seg2, w_out_b[l, :RW_WIDTH], w_out_b[l, RW_WIDTH:],
                       row(ln_g[l, 1]), row(ln_b[l, 1]))
        x = mem_fn(x)
        x = ffn_ln(x, w1b, w3b, w2b, (l, 1), row(ln_g[l, 3]), row(ln_b[l, 3]))
        return x, _unpair_state(s_fin), shift, win_k, win_v

    xp = x_prompt.reshape(bp * tp, d)
    p_wkv, p_shift, p_wk, p_wv, p_mk, p_mv = [], [], [], [], [], []
    for l in range(depth):
        prev_prompt = lambda p_rw: p_rw

        def swa_p(q, k, v, l=l):
            k3 = k.reshape(bp, tp, KV_WIDTH)
            v3 = v.reshape(bp, tp, KV_WIDTH)
            y = swa_prompt(q.reshape(bp, tp, SW_WIDTH), k3, v3, sw_sinks[l])
            tail = lambda a: a[:, -WINDOW:].reshape(bp, WINDOW, SW_KV_HEADS, HEAD_DIM)
            return y.reshape(bp * tp, SW_WIDTH), tail(k3), tail(v3)

        mk, mv = matmul_multi(mem_prompt.reshape(bp * MEM_TOKENS, d), [wkb[l], wvb[l]])
        mk = mk.reshape(bp, MEM_TOKENS, d)
        mv = mv.reshape(bp, MEM_TOKENS, d)

        def mem_p(x, l=l, mk=mk, mv=mv):
            return mem_block(x.reshape(bp, tp, d), wqb[l], mk, mv, wob[l], row(ln_g[l, 2]),
                             row(ln_b[l, 2])).reshape(bp * tp, d)

        s0 = jnp.zeros((bp, RW_PAIRS, HEAD_DIM, LANES), F32)
        xp, s_fin, shift, wk_, wv_ = layer(l, xp, bp, tp, prev_prompt, s0, swa_p, mem_p)
        p_wkv.append(s_fin)
        p_shift.append(shift)
        p_wk.append(wk_)
        p_wv.append(wv_)
        p_mk.append(mk.reshape(bp, MEM_TOKENS, MEM_HEADS, MEM_HEAD_DIM))
        p_mv.append(mv.reshape(bp, MEM_TOKENS, MEM_HEADS, MEM_HEAD_DIM))

    xs = x_sample.reshape(bs, d)
    s_wkv, s_shift, s_wk, s_wv = [], [], [], []
    for l in range(depth):
        def prev_sample(p_rw, l=l):
            (prev,) = matmul_multi(state_shift[l], [w_in_b[l, :, :q0]])
            return prev

        def swa_s(q, k, v, l=l):
            o, nk, nv = swa_sample(
                q.reshape(bs, SW_HEADS, HEAD_DIM), k.reshape(bs, 1, KV_WIDTH), v.reshape(bs, 1, KV_WIDTH),
                cache_win_k[l].reshape(bs, WINDOW, KV_WIDTH), cache_win_v[l].reshape(bs, WINDOW, KV_WIDTH),
                sw_sinks[l].reshape(SW_HEADS, 1), slopes)
            unflat = lambda a: a.reshape(bs, WINDOW, SW_KV_HEADS, HEAD_DIM)
            return o.reshape(bs, SW_WIDTH), unflat(nk), unflat(nv)

        def mem_s(x, l=l):
            (qm,) = matmul_multi(x, [wqb[l]])
            o = mem_attn_token(qm.reshape(bs, 1, d), cache_mem_k, cache_mem_v, l).reshape(bs, d)
            return proj_ln(x, [o], [wob[l]], row(ln_g[l, 2]), row(ln_b[l, 2]))

        xs, s_fin, shift, wk_, wv_ = layer(l, xs, bs, 1, prev_sample, _pair_state(state_wkv[l]),
                                           swa_s, mem_s)
        s_wkv.append(s_fin)
        s_shift.append(shift)
        s_wk.append(wk_)
        s_wv.append(wv_)

    return (xp.reshape(bp, tp, d), xs.reshape(bs, 1, d),
            jnp.stack(p_wkv), jnp.stack(p_shift), jnp.stack(p_wk), jnp.stack(p_wv),
            jnp.stack(p_mk), jnp.stack(p_mv),
            jnp.stack(s_wkv), jnp.stack(s_shift), jnp.stack(s_wk), jnp.stack(s_wv))
```

```python
import functools

import jax
import jax.numpy as jnp
from jax import lax
from jax.experimental import pallas as pl
from jax.experimental.pallas import tpu as pltpu

F32 = jnp.float32
BF16 = jnp.bfloat16

D_MODEL = 1024
HEAD_DIM = 64
RW_WIDTH = 512
RW_HEADS = 8
RW_PAIRS = RW_HEADS // 2
SW_WIDTH = 512
SW_HEADS = 8
SW_KV_HEADS = 2
SW_GROUP = SW_HEADS // SW_KV_HEADS
KV_WIDTH = SW_KV_HEADS * HEAD_DIM
WINDOW = 128
D_W_LORA = 64
D_A_LORA = 64
D_G_LORA = 128
RW_COLS = 3 * RW_WIDTH + D_W_LORA + D_A_LORA + D_G_LORA
LORA_WA_START = 3 * RW_WIDTH
LORA_G_START = LORA_WA_START + D_W_LORA + D_A_LORA
MEM_TOKENS = 256
MEM_HEADS = 4
MEM_HEAD_DIM = D_MODEL // MEM_HEADS
D_FF = 2816
DEPTH = 4
ALPHA = (2.0 * DEPTH) ** 0.25
LN_EPS = 1e-5
GN_EPS = 64e-5
NORM_EPS = 1e-12

CHUNK = 64
FF_TILE = 256
LANES = 128
VMEM_LIMIT = 56 * 1024 * 1024


def _params(*semantics):
    return pltpu.CompilerParams(dimension_semantics=semantics, vmem_limit_bytes=VMEM_LIMIT)


def _row_tile(n, want):
    return want if n % want == 0 else n


def _layer_norm(z, g, b):
    mu = jnp.mean(z, axis=-1, keepdims=True)
    d = z - mu
    var = jnp.mean(d * d, axis=-1, keepdims=True)
    return d * lax.rsqrt(var + LN_EPS) * g + b


def _sigmoid(x):
    return 1.0 / (1.0 + jnp.exp(-x))


def _split3(x):
    hi = x.astype(BF16)
    r1 = x - hi.astype(F32)
    mid = r1.astype(BF16)
    lo = (r1 - mid.astype(F32)).astype(BF16)
    return jnp.concatenate([hi, mid, lo], axis=1)


def _split2(x):
    hi = x.astype(BF16)
    lo = (x - hi.astype(F32)).astype(BF16)
    return jnp.concatenate([hi, lo], axis=1)


def _mm_kernel(x_ref, *refs, n_w, splits):
    xb = x_ref[...].astype(BF16)
    prods = [jnp.dot(xb, w_ref[...], preferred_element_type=F32) for w_ref in refs[:n_w]]
    if splits is not None:
        edges = [sum(splits[:i]) for i in range(len(splits) + 1)]
        prods = [prods[0][:, lo:hi] for lo, hi in zip(edges[:-1], edges[1:])]
    for p, o_ref in zip(prods, refs[n_w:]):
        o_ref[...] = p


def _layer_spec(w, layer, rows=None, cols=None):
    shape = (rows or w.shape[1], cols or w.shape[2])
    return pl.BlockSpec((None,) + shape, lambda *_: (layer, 0, 0))


def matmul_multi(x, ws, layer, widths=None, splits=None, tm=512):
    n, k = x.shape
    tm = _row_tile(n, tm)
    widths = widths or [w.shape[2] for w in ws]
    outs = splits or widths
    return pl.pallas_call(
        functools.partial(_mm_kernel, n_w=len(ws), splits=splits),
        grid=(n // tm,),
        in_specs=[pl.BlockSpec((tm, k), lambda i: (i, 0))]
        + [_layer_spec(w, layer, cols=c) for w, c in zip(ws, widths)],
        out_specs=[pl.BlockSpec((tm, c), lambda i: (i, 0)) for c in outs],
        out_shape=[jax.ShapeDtypeStruct((n, c), F32) for c in outs],
        compiler_params=_params("parallel"),
        name="matmul_multi",
    )(x, *ws)


def _ffn_kernel(x_ref, w1_ref, w3_ref, w2_ref, g_ref, b_ref, o_ref, xb_ref, acc_ref, *, n_ff):
    xb_ref[...] = x_ref[...].astype(BF16)

    def part(c):
        cols = slice(c * FF_TILE, (c + 1) * FF_TILE)
        xb = xb_ref[...]
        h1 = jnp.dot(xb, w1_ref[:, cols], preferred_element_type=F32)
        h3 = jnp.dot(xb, w3_ref[:, cols], preferred_element_type=F32)
        h = (h1 * _sigmoid(h1)) * h3
        return jnp.dot(h.astype(BF16), w2_ref[cols, :], preferred_element_type=F32)

    acc_ref[...] = part(0)
    for c in range(1, n_ff):
        acc_ref[...] += part(c)
    z = ALPHA * x_ref[...] + 0.5 * acc_ref[...]
    o_ref[...] = _layer_norm(z, g_ref[...], b_ref[...])


def ffn_ln(x, w1, w3, w2, lead, g, b, tm=1024):
    n, d = x.shape
    tm = _row_tile(n, tm)
    n_ff = D_FF // FF_TILE
    resident = lambda a: pl.BlockSpec((None,) * len(lead) + a.shape[len(lead):],
                                      lambda i: lead + (0, 0), pipeline_mode=pl.Buffered(1))
    return pl.pallas_call(
        functools.partial(_ffn_kernel, n_ff=n_ff),
        grid=(n // tm,),
        in_specs=[
            pl.BlockSpec((tm, d), lambda i: (i, 0)),
            resident(w1), resident(w3), resident(w2),
            pl.BlockSpec((1, d), lambda i: (0, 0)),
            pl.BlockSpec((1, d), lambda i: (0, 0)),
        ],
        out_specs=pl.BlockSpec((tm, d), lambda i: (i, 0)),
        out_shape=jax.ShapeDtypeStruct((n, d), F32),
        scratch_shapes=[pltpu.VMEM((tm, d), BF16), pltpu.VMEM((tm, d), F32)],
        compiler_params=_params("parallel"),
        name="ffn_ln",
    )(x, w1, w3, w2, g, b)


def _proj_ln_kernel(x_ref, *refs, n_in):
    a_refs = refs[:n_in]
    w_refs = refs[n_in:2 * n_in]
    g_ref, b_ref, o_ref = refs[2 * n_in:]
    f = None
    for a_ref, w_ref in zip(a_refs, w_refs):
        t = jnp.dot(a_ref[...].astype(BF16), w_ref[...], preferred_element_type=F32)
        f = t if f is None else f + t
    o_ref[...] = _layer_norm(ALPHA * x_ref[...] + f, g_ref[...], b_ref[...])


def proj_ln(x, acts, ws, layer, g, b, tm=512):
    n, d = x.shape
    tm = _row_tile(n, tm)
    n_in = len(acts)
    return pl.pallas_call(
        functools.partial(_proj_ln_kernel, n_in=n_in),
        grid=(n // tm,),
        in_specs=[pl.BlockSpec((tm, d), lambda i: (i, 0))]
        + [pl.BlockSpec((tm, a.shape[1]), lambda i: (i, 0)) for a in acts]
        + [_layer_spec(w, layer) for w in ws]
        + [pl.BlockSpec((1, d), lambda i: (0, 0))] * 2,
        out_specs=pl.BlockSpec((tm, d), lambda i: (i, 0)),
        out_shape=jax.ShapeDtypeStruct((n, d), F32),
        compiler_params=_params("parallel"),
        name="proj_ln",
    )(x, *acts, *ws, g, b)


def _rwkv_prep_kernel(p_ref, prev_ref, mu_ref, w0_ref, wup_ref, a0_ref, aup_ref, gup_ref,
                      kk_ref, ka_ref, rk_ref, seg_ref, *refs, chunked, tiles_per_seq):
    if chunked:
        tril_ref, ones_ref = refs[:2]
        refs = refs[2:]
    g_out, bonus_out = refs[-2:]
    p = p_ref[...]
    if chunked:
        first = (pl.program_id(0) % tiles_per_seq) == 0
        above = jnp.where(first, 0.0, prev_ref[7:8, :])
        top = lax.broadcasted_iota(jnp.int32, p.shape, 0) == 0
        prev = jnp.where(top, above, pltpu.roll(p, 1, 0))
    else:
        prev = prev_ref[...]
    xm = p + (prev - p) * mu_ref[...]
    r = xm[:, 0:RW_WIDTH]
    k = xm[:, RW_WIDTH:2 * RW_WIDTH]
    v = xm[:, 2 * RW_WIDTH:3 * RW_WIDTH]
    wa = xm[:, LORA_WA_START:LORA_G_START]
    gl = xm[:, LORA_G_START:RW_COLS]

    def seg_sum(x):
        return jnp.dot(_split2(x), seg_ref[...], preferred_element_type=F32)

    lw = jnp.dot(jnp.tanh(wa).astype(BF16), wup_ref[...], preferred_element_type=F32)
    la = jnp.dot(wa.astype(BF16), aup_ref[...], preferred_element_type=F32)
    z = -(w0_ref[...] + lw)
    softplus = jnp.maximum(z, 0.0) + jnp.log(1.0 + jnp.exp(-jnp.abs(z)))
    w_log = -softplus - 0.5
    log_decay = -jnp.exp(w_log)
    a = _sigmoid(a0_ref[...] + la)
    g = jnp.dot(_sigmoid(gl).astype(BF16), gup_ref[...], preferred_element_type=F32)
    kk = k * kk_ref[...]
    nrm = jnp.sqrt(seg_sum(kk * kk))
    kk = kk / jnp.maximum(nrm, NORM_EPS)
    k_mod = k * (1.0 + (a - 1.0) * ka_ref[...])
    kka = kk * a
    if chunked:
        parts = _split3(log_decay)

        def time_sum(m_ref):
            s3 = jnp.dot(m_ref[...], parts, preferred_element_type=F32)
            return s3[:, :RW_WIDTH] + s3[:, RW_WIDTH:2 * RW_WIDTH] + s3[:, 2 * RW_WIDTH:]

        cum = time_sum(tril_ref)
        tot = time_sum(ones_ref)
        grow = jnp.exp(-cum)
        rest = jnp.exp(tot - cum)
        outs = (kk * jnp.exp(cum - log_decay), r * jnp.exp(cum), -kka * grow, k_mod * grow,
                -kka * rest, k_mod * rest, v, jnp.exp(tot))
    else:
        outs = (r, jnp.exp(log_decay), k_mod, v, kk, kka)
    for o_ref, val in zip(refs, outs):
        for pr in range(RW_PAIRS):
            o_ref[pr] = val[:, pr * LANES:(pr + 1) * LANES].astype(o_ref.dtype)
    g_out[...] = g
    bonus_out[...] = seg_sum(r * k_mod * rk_ref[...]) * v


def rwkv_prep(p, prev, mu, w0, wup, a0, aup, gup, k_k, k_a, r_k, seg, seq_len, tm=256):
    n = p.shape[0]
    chunked = seq_len > 1
    tm = _row_tile(n, tm)
    row = lambda c: pl.BlockSpec((tm, c), lambda i: (i, 0))
    prev_spec = row(RW_COLS)
    if chunked:
        assert seq_len % tm == 0
        prev_spec = pl.BlockSpec((8, RW_COLS), lambda i: (jnp.maximum(i * (tm // 8) - 1, 0), 0))
    full = lambda a: pl.BlockSpec(a.shape, lambda i: (0, 0))
    consts = (mu, w0, wup, a0, aup, gup, k_k, k_a, r_k, seg)
    dtypes = [F32] * 6
    if chunked:
        assert tm % CHUNK == 0
        t_idx = jnp.arange(tm)
        same = (t_idx[:, None] // CHUNK) == (t_idx[None, :] // CHUNK)
        consts += ((same & (t_idx[None, :] <= t_idx[:, None])).astype(BF16), same.astype(BF16))
        dtypes = [BF16] * 7 + [F32]
    pair = pl.BlockSpec((RW_PAIRS, tm, LANES), lambda i: (0, i, 0))
    return pl.pallas_call(
        functools.partial(_rwkv_prep_kernel, chunked=chunked, tiles_per_seq=max(seq_len // tm, 1)),
        grid=(n // tm,),
        in_specs=[row(RW_COLS), prev_spec] + [full(c) for c in consts],
        out_specs=[pair] * len(dtypes) + [row(RW_WIDTH)] * 2,
        out_shape=[jax.ShapeDtypeStruct((RW_PAIRS, n, LANES), dt) for dt in dtypes]
        + [jax.ShapeDtypeStruct((n, RW_WIDTH), F32)] * 2,
        compiler_params=_params("parallel"),
        name="rwkv_prep",
    )(p, prev, *consts)


def _wkv_step_kernel(r_ref, w_ref, k_ref, v_ref, kk_ref, kka_ref, s0_ref, y_ref, s_ref, *, bb):
    sub = lax.broadcasted_iota(jnp.int32, (HEAD_DIM, LANES), 0)
    lane = lax.broadcasted_iota(jnp.int32, (HEAD_DIM, LANES), 1)
    diag = (lane % HEAD_DIM) == sub
    rj = lax.broadcasted_iota(jnp.int32, (2 * LANES, LANES), 0)
    cl = lax.broadcasted_iota(jnp.int32, (2 * LANES, LANES), 1)
    ones2 = jnp.where(((rj % LANES) // HEAD_DIM) == (cl // HEAD_DIM), 1.0, 0.0).astype(BF16)

    def head_sums(xs):
        out = jnp.dot(_split2(jnp.concatenate(xs, axis=0)), ones2, preferred_element_type=F32)
        return [out[i * HEAD_DIM:(i + 1) * HEAD_DIM] for i in range(len(xs))]

    pairs = [(b, p) for b in range(bb) for p in range(RW_PAIRS)]
    n = len(pairs)
    row = lambda ref, i: ref[pairs[i][1], pairs[i][0]]
    tiles = lambda x: x.reshape(HEAD_DIM // 8, 8, LANES)
    flat = lambda x: x.reshape(HEAD_DIM, LANES)
    s = [tiles(s0_ref[b, p]) for b, p in pairs]
    red = head_sums([flat(s[i] * row(kk_ref, i)) for i in range(n)]
                    + [flat(jnp.where(tiles(diag), row(v_ref, i), 0.0)) for i in range(n)])
    s = [s[i] * row(w_ref, i) - tiles(red[i]) * row(kka_ref, i) + tiles(red[n + i]) * row(k_ref, i)
         for i in range(n)]
    yb = head_sums([flat(s[i] * row(r_ref, i)) for i in range(n)])
    for i, (b, p) in enumerate(pairs):
        s_ref[b, p] = flat(s[i])
        y_ref[p, b] = jnp.sum(jnp.where(diag, yb[i], 0.0), axis=0, keepdims=True)


def wkv_step(r, w, k, v, kk, kka, s0, bb=8):
    _, nb, t, _ = r.shape
    assert t == 1 and nb % bb == 0
    seq = pl.BlockSpec((RW_PAIRS, bb, 1, LANES), lambda i: (0, i, 0, 0))
    st = pl.BlockSpec((bb, RW_PAIRS, HEAD_DIM, LANES), lambda i: (i, 0, 0, 0))
    return pl.pallas_call(
        functools.partial(_wkv_step_kernel, bb=bb),
        grid=(nb // bb,),
        in_specs=[seq] * 6 + [st],
        out_specs=[seq, st],
        out_shape=[jax.ShapeDtypeStruct((RW_PAIRS, nb, 1, LANES), F32),
                   jax.ShapeDtypeStruct(s0.shape, F32)],
        compiler_params=_params("parallel"),
        name="wkv_step",
    )(r, w, k, v, kk, kka, s0)


def _dot_nt(a, b):
    return lax.dot_general(a, b, (((1,), (1,)), ((), ())), preferred_element_type=F32)


def _dot_tn(a, b):
    return lax.dot_general(a, b, (((0,), (0,)), ((), ())), preferred_element_type=F32)


def _dot(a, b):
    return jnp.dot(a, b, preferred_element_type=F32)


def _wkv_chunk_kernel(qt_ref, rt_ref, bh_ref, kh_ref, bb_ref, kb_ref, v_ref, gc_ref, s0_ref,
                      y_ref, s_ref, *, nb, tc):
    @pl.when(pl.program_id(1) == 0)
    def _():
        s_ref[...] = s0_ref[...]

    c = CHUNK
    assert c == HEAD_DIM and 2 * c == LANES
    row = lax.broadcasted_iota(jnp.int32, (c, LANES), 0)
    lane = lax.broadcasted_iota(jnp.int32, (c, LANES), 1)
    head_a = lane < HEAD_DIM
    head_a2 = lax.broadcasted_iota(jnp.int32, (2 * c, LANES), 1) < HEAD_DIM
    strict = (lane % c) < row
    eye = jnp.where((lane % c) == row, 1.0, 0.0)
    row4 = lax.broadcasted_iota(jnp.int32, (c, 2 * LANES), 0)
    col4 = lax.broadcasted_iota(jnp.int32, (c, 2 * LANES), 1)
    kh_cols = (col4 >= c) & (col4 < 3 * c)
    m_mask = kh_cols & ((col4 % c) < row4)
    incl = (col4 % c) <= row4
    pairs = [(b, p) for b in range(nb) for p in range(RW_PAIRS)]
    bf = lambda x: x.astype(BF16)
    stack = lambda *xs: jnp.concatenate(xs, axis=0)
    zero = jnp.zeros((), BF16)
    only_a = lambda x: jnp.where(head_a if x.shape[0] == c else head_a2, x, zero)
    only_b = lambda x: jnp.where(head_a if x.shape[0] == c else head_a2, zero, x)
    per_head = lambda x: stack(only_a(x), only_b(x))
    zeros_c = jnp.zeros((c, LANES), BF16)

    def inverse_stages(ci, ready):
        ld = lambda ref, b, p: ref[p, b, pl.ds(ci * c, c), :]
        qr = [stack(ld(qt_ref, b, p), ld(rt_ref, b, p)) for b, p in pairs]
        bh = [ld(bh_ref, b, p) for b, p in pairs]
        kh = [ld(kh_ref, b, p) for b, p in pairs]
        e = [_dot_nt(q, stack(only_a(stack(x, y)), only_b(stack(y, x)))) for q, x, y in zip(qr, bh, kh)]
        yield
        pw = [jnp.where(strict, jnp.where(head_a, x[:c, :LANES], x[:c, LANES:]), 0.0) for x in e]
        inv = [eye + n for n in pw]
        for _ in range(5):
            pw = [_dot(x, per_head(x)) for x in [bf(x) for x in pw]]
            yield
            inv = [t + _dot(bf(x), per_head(bf(t))) for x, t in zip(pw, inv)]
            yield
        ready[ci] = (qr, e, inv)

    def state_stages(ci, ready):
        qr, e, inv = ready.pop(ci)
        ld = lambda ref, b, p: ref[p, b, pl.ds(ci * c, c), :]
        s = [s_ref[b, p] for b, p in pairs]
        ff = [_dot_nt(q, per_head(bf(x))) for q, x in zip(qr, s)]
        v = [ld(v_ref, b, p) for b, p in pairs]
        yield
        g = [f[:c] + _dot(bf(jnp.where(m_mask, x[:c], 0.0)), stack(zeros_c, only_a(w), only_b(w), zeros_c))
             for f, x, w in zip(ff, e, v)]
        yield
        u = [bf(_dot(bf(t), per_head(bf(x)))) for t, x in zip(inv, g)]
        yield
        y = [f[c:] + _dot(bf(jnp.where(incl, x[c:], 0.0)), stack(only_a(uu), only_a(w), only_b(w), only_b(uu)))
             for f, x, uu, w in zip(ff, e, u, v)]
        add = [_dot_tn(stack(uu, w), stack(ld(bb_ref, b, p), ld(kb_ref, b, p)))
               for uu, w, (b, p) in zip(u, v, pairs)]
        yield
        for i, (b, p) in enumerate(pairs):
            y_ref[p, b, pl.ds(ci * c, c), :] = y[i]
            gc = gc_ref[p, b, pl.ds(ci * c, 8), :]
            kept = (s[i].reshape(c // 8, 8, LANES) * gc).reshape(c, LANES)
            s_ref[b, p] = kept + jnp.where(head_a, add[i][:c], add[i][c:])
        yield

    n_chunks = tc // c
    state_len, inv_len = 5, 11
    ready, inverses, state = {}, {}, None
    for tick in range(-inv_len, state_len * n_chunks):
        for ci in range(n_chunks):
            if ci not in inverses and state_len * ci - inv_len <= tick:
                inverses[ci] = inverse_stages(ci, ready)
        for gen in inverses.values():
            next(gen, None)
        if tick >= 0:
            if tick % state_len == 0:
                state = state_stages(tick // state_len, ready)
            next(state)


def wkv_chunked(qt, rt, bh, kh, bb, kb, v, gc, s0, tc=512):
    _, nb, t, _ = qt.shape
    seq = pl.BlockSpec((RW_PAIRS, nb, tc, LANES), lambda i, j: (0, 0, j, 0))
    st = pl.BlockSpec((nb, RW_PAIRS, HEAD_DIM, LANES), lambda i, j: (0, 0, 0, 0))
    return pl.pallas_call(
        functools.partial(_wkv_chunk_kernel, nb=nb, tc=tc),
        grid=(1, t // tc),
        in_specs=[seq] * 8 + [st],
        out_specs=[seq, st],
        out_shape=[jax.ShapeDtypeStruct((RW_PAIRS, nb, t, LANES), F32),
                   jax.ShapeDtypeStruct(s0.shape, F32)],
        compiler_params=_params("arbitrary", "arbitrary"),
        name="wkv_chunked",
    )(qt, rt, bh, kh, bb, kb, v, gc, s0)


def _mix_out_kernel(x_ref, y_ref, bonus_ref, g_ref, ysw_ref, gng_ref, gnb_ref, seg_ref, lng_ref, lnb_ref,
                    wout_ref, o_ref):
    def seg_mean(x):
        return jnp.dot(_split2(x), seg_ref[...], preferred_element_type=F32) * (1.0 / HEAD_DIM)

    y = jnp.concatenate([y_ref[pr] for pr in range(RW_PAIRS)], axis=1)
    d = y - seg_mean(y)
    var = seg_mean(d * d)
    yn = d * lax.rsqrt(var + GN_EPS) * gng_ref[...] + gnb_ref[...]
    y_rw = (yn + bonus_ref[...]) * g_ref[...]
    f = _dot(y_rw.astype(BF16), wout_ref[:RW_WIDTH, :]) + _dot(ysw_ref[...].astype(BF16), wout_ref[RW_WIDTH:, :])
    o_ref[...] = _layer_norm(ALPHA * x_ref[...] + f, lng_ref[...], lnb_ref[...])


def mix_out_ln(x, y, bonus, g, y_sw, gn_g, gn_b, seg, ln_g, ln_b, w_out, layer, tm=512):
    n, d = x.shape
    tm = _row_tile(n, tm)
    row = lambda c: pl.BlockSpec((tm, c), lambda i: (i, 0))
    full = lambda a: pl.BlockSpec(a.shape, lambda i: (0, 0))
    consts = (gn_g, gn_b, seg, ln_g, ln_b)
    return pl.pallas_call(
        _mix_out_kernel,
        grid=(n // tm,),
        in_specs=[row(d), pl.BlockSpec((RW_PAIRS, tm, LANES), lambda i: (0, i, 0)), row(RW_WIDTH),
                  row(RW_WIDTH), row(SW_WIDTH)] + [full(c) for c in consts] + [_layer_spec(w_out, layer)],
        out_specs=row(d),
        out_shape=jax.ShapeDtypeStruct((n, d), F32),
        compiler_params=_params("parallel"),
        name="mix_out_ln",
    )(x, y, bonus, g, y_sw, *consts, w_out)


def _alibi_slope(h):
    return 2.0 ** (-8.0 * (h + 1) / SW_HEADS)


def _swa_prompt_kernel(sink_ref, bias0_ref, bias1_ref, q_ref, kp_ref, kc_ref, vp_ref, vc_ref, o_ref, *, qb_per_step):
    qb = (q_ref[0] * HEAD_DIM ** -0.5).astype(BF16)
    kw = jnp.concatenate([kp_ref[0], kc_ref[0]], axis=0).astype(BF16)
    vw = jnp.concatenate([vp_ref[0], vc_ref[0]], axis=0).astype(BF16)
    kx = pltpu.roll(kw, HEAD_DIM, 1)
    vx = pltpu.roll(vw, HEAD_DIM, 1)
    low_kv = lax.broadcasted_iota(jnp.int32, kw.shape, 1) < HEAD_DIM
    low_q = lax.broadcasted_iota(jnp.int32, (WINDOW, LANES), 1) < HEAD_DIM
    zero = jnp.zeros((), BF16)
    kc = [jnp.where(low_kv, kw, kx), jnp.where(low_kv, kx, kw)]
    vc = [jnp.where(low_kv, vw, vx), jnp.where(low_kv, vx, vw)]
    items = [(j, h) for j in range(qb_per_step) for h in range(SW_HEADS)]
    rows = lambda j: slice(j * WINDOW, (j + 1) * WINDOW)
    keys = lambda j: slice(j * WINDOW, (j + 2) * WINDOW)
    bias = lambda j, h: bias0_ref[h] if j == 0 else bias1_ref[h]
    qh = []
    for j, h in items:
        q2 = qb[rows(j), (h // 2) * LANES:(h // 2 + 1) * LANES]
        qh.append(jnp.where(low_q, q2, zero) if h % 2 == 0 else jnp.where(low_q, zero, q2))
    logits = [_dot_nt(x, kc[h // SW_GROUP][keys(j)]) + bias(j, h) for x, (j, h) in zip(qh, items)]
    m = [jnp.maximum(jnp.max(x, axis=-1, keepdims=True), sink_ref[h]) for x, (j, h) in zip(logits, items)]
    e = [jnp.exp(x - mx) for x, mx in zip(logits, m)]
    den = [jnp.sum(x, axis=-1, keepdims=True) + jnp.exp(sink_ref[h] - mx) for x, mx, (j, h) in zip(e, m, items)]
    o = [_dot(x.astype(BF16), vc[h // SW_GROUP][keys(j)]) / dn for x, dn, (j, h) in zip(e, den, items)]
    for i in range(0, len(items), 2):
        j, h = items[i]
        o_ref[0, rows(j), (h // 2) * LANES:(h // 2 + 1) * LANES] = jnp.where(low_q, o[i], o[i + 1])


def swa_prompt(q, k, v, sinks, qb_per_step=8):
    nb, t, _ = q.shape
    rows = qb_per_step * WINDOW
    assert t % rows == 0
    i = jnp.arange(WINDOW)[:, None]
    j = jnp.arange(2 * WINDOW)[None, :]
    dist = WINDOW + i - j
    valid = (dist >= 0) & (dist < WINDOW)
    slopes = jnp.asarray([_alibi_slope(h) for h in range(SW_HEADS)], F32)[:, None, None]
    table = lambda ok: jnp.where(ok[None], -slopes * dist.astype(F32)[None], -jnp.inf)
    bias = jnp.stack([table(valid & (j >= WINDOW)), table(valid)])
    table_spec = lambda pick: pl.BlockSpec((None, SW_HEADS, WINDOW, 2 * WINDOW), lambda b, n: (pick(n), 0, 0, 0))
    cur = lambda w: pl.BlockSpec((1, rows, w), lambda b, n: (b, n, 0))
    prv = lambda w: pl.BlockSpec((1, WINDOW, w), lambda b, n: (b, jnp.maximum(n * qb_per_step - 1, 0), 0))
    return pl.pallas_call(
        functools.partial(_swa_prompt_kernel, qb_per_step=qb_per_step),
        grid=(nb, t // rows),
        in_specs=[pl.BlockSpec(memory_space=pltpu.SMEM),
                  table_spec(lambda n: jnp.minimum(n, 1)), table_spec(lambda n: 1),
                  cur(SW_WIDTH), prv(KV_WIDTH), cur(KV_WIDTH), prv(KV_WIDTH), cur(KV_WIDTH)],
        out_specs=cur(SW_WIDTH),
        out_shape=jax.ShapeDtypeStruct((nb, t, SW_WIDTH), F32),
        compiler_params=_params("parallel", "parallel"),
        name="swa_prompt",
    )(sinks, bias, bias, q, k, k, v, v)


def _swa_sample_kernel(sink_ref, slope_ref, q_ref, kn_ref, vn_ref, ck_ref, cv_ref,
                       o_ref, nk_ref, nv_ref, *, bb):
    last = lax.broadcasted_iota(jnp.int32, (WINDOW, KV_WIDTH), 0) == WINDOW - 1
    j = lax.broadcasted_iota(jnp.int32, (SW_HEADS, WINDOW), 1)
    bias = slope_ref[...] * (WINDOW - 1 - j).astype(F32)
    sink = sink_ref[...]
    seqs = range(bb)
    kw = [jnp.where(last, kn_ref[b], pltpu.roll(ck_ref[b], WINDOW - 1, 0)) for b in seqs]
    vw = [jnp.where(last, vn_ref[b], pltpu.roll(cv_ref[b], WINDOW - 1, 0)) for b in seqs]
    s = [_dot_nt(q_ref[b].astype(BF16), kw[b].astype(BF16)) * HEAD_DIM ** -0.5 - bias for b in seqs]
    m = [jnp.maximum(jnp.max(x, axis=-1, keepdims=True), sink) for x in s]
    e = [jnp.exp(x - mx) for x, mx in zip(s, m)]
    prob = [x / (jnp.sum(x, axis=-1, keepdims=True) + jnp.exp(sink - mx)) for x, mx in zip(e, m)]
    o = [_dot(prob[b].astype(BF16), vw[b].astype(BF16)) for b in seqs]
    for b in seqs:
        nk_ref[b] = kw[b]
        nv_ref[b] = vw[b]
        o_ref[b] = o[b]


def swa_sample(q, kn, vn, ck, cv, sinks, slopes, bb=8):
    nb = q.shape[0]
    assert nb % bb == 0
    own = (jnp.arange(SW_HEADS) // SW_GROUP)[:, None] == jnp.arange(SW_KV_HEADS)[None, :]
    q2 = jnp.where(own[None, :, :, None], q[:, :, None, :], 0.0).reshape(nb, SW_HEADS, KV_WIDTH)
    per_b = lambda s: pl.BlockSpec((bb,) + s, lambda b: (b, 0, 0))
    full = lambda a: pl.BlockSpec(a.shape, lambda b: (0, 0))
    o2, nk, nv = pl.pallas_call(
        functools.partial(_swa_sample_kernel, bb=bb),
        grid=(nb // bb,),
        in_specs=[full(sinks), full(slopes), per_b((SW_HEADS, KV_WIDTH)), per_b((1, KV_WIDTH)),
                  per_b((1, KV_WIDTH)), per_b((WINDOW, KV_WIDTH)), per_b((WINDOW, KV_WIDTH))],
        out_specs=[per_b((SW_HEADS, KV_WIDTH)), per_b((WINDOW, KV_WIDTH)), per_b((WINDOW, KV_WIDTH))],
        out_shape=[jax.ShapeDtypeStruct((nb, SW_HEADS, KV_WIDTH), F32),
                   jax.ShapeDtypeStruct((nb, WINDOW, KV_WIDTH), F32),
                   jax.ShapeDtypeStruct((nb, WINDOW, KV_WIDTH), F32)],
        compiler_params=_params("parallel"),
        name="swa_sample",
    )(sinks, slopes, q2, kn, vn, ck, cv)
    o = jnp.sum(jnp.where(own[None, :, :, None], o2.reshape(nb, SW_HEADS, SW_KV_HEADS, HEAD_DIM), 0.0), axis=2)
    return o, nk, nv


def _mem_block_kernel(x_ref, wq_ref, mk_ref, mv_ref, wo_ref, g_ref, b_ref, o_ref):
    x = x_ref[0]
    qb = _dot(x.astype(BF16), wq_ref[...]).astype(BF16)
    cols = [slice(h * MEM_HEAD_DIM, (h + 1) * MEM_HEAD_DIM) for h in range(MEM_HEADS)]
    s = [_dot_nt(qb[:, c], mk_ref[0, :, c].astype(BF16)) * MEM_HEAD_DIM ** -0.5 for c in cols]
    e = [jnp.exp(x - jnp.max(x, axis=-1, keepdims=True)) for x in s]
    den = [jnp.sum(x, axis=-1, keepdims=True) for x in e]
    outs = [_dot(x.astype(BF16), mv_ref[0, :, c].astype(BF16)) / d for x, c, d in zip(e, cols, den)]
    o = jnp.concatenate(outs, axis=1).astype(BF16)
    o_ref[0] = _layer_norm(ALPHA * x + _dot(o, wo_ref[...]), g_ref[...], b_ref[...])


def mem_block(x, wq, mk, mv, wo, layer, g, b, tm=512):
    ng, t, d = x.shape
    tm = _row_tile(t, tm)
    row = pl.BlockSpec((1, tm, d), lambda gi, i: (gi, i, 0))
    mem = pl.BlockSpec((1, MEM_TOKENS, d), lambda gi, i: (gi, 0, 0))
    full = lambda a: pl.BlockSpec(a.shape, lambda gi, i: (0, 0))
    return pl.pallas_call(
        _mem_block_kernel,
        grid=(ng, t // tm),
        in_specs=[row, _layer_spec(wq, layer), mem, mem, _layer_spec(wo, layer), full(g), full(b)],
        out_specs=row,
        out_shape=jax.ShapeDtypeStruct((ng, t, d), F32),
        compiler_params=_params("parallel", "parallel"),
        name="mem_block",
    )(x, wq, mk, mv, wo, g, b)


def _mem_attn_token_kernel(q_ref, mk_ref, mv_ref, o_ref, *, bb):
    rows = MEM_TOKENS * MEM_HEADS
    lane = lax.broadcasted_iota(jnp.int32, (MEM_HEADS, rows), 1)
    head = lax.broadcasted_iota(jnp.int32, (MEM_HEADS, rows), 0)
    own = (lane % MEM_HEADS) == head
    for b in range(bb):
        q = q_ref[b]
        q4 = jnp.concatenate([q[:, h * MEM_HEAD_DIM:(h + 1) * MEM_HEAD_DIM] for h in range(MEM_HEADS)],
                             axis=0).astype(BF16)
        k2 = mk_ref[b].reshape(rows, MEM_HEAD_DIM).astype(BF16)
        v2 = mv_ref[b].reshape(rows, MEM_HEAD_DIM).astype(BF16)
        s = jnp.where(own, _dot_nt(q4, k2) * MEM_HEAD_DIM ** -0.5, -jnp.inf)
        m = jnp.max(s, axis=-1, keepdims=True)
        e = jnp.exp(s - m)
        prob = e / jnp.sum(e, axis=-1, keepdims=True)
        o = _dot(prob.astype(BF16), v2)
        for h in range(MEM_HEADS):
            o_ref[b, :, h * MEM_HEAD_DIM:(h + 1) * MEM_HEAD_DIM] = o[h:h + 1]


def mem_attn_token(q, mk, mv, layer, bb=4):
    ng, _, d = q.shape
    assert ng % bb == 0
    row = pl.BlockSpec((bb, 1, d), lambda g: (g, 0, 0))
    mem = pl.BlockSpec((None, bb, MEM_TOKENS, MEM_HEADS, MEM_HEAD_DIM), lambda g: (layer, g, 0, 0, 0))
    return pl.pallas_call(
        functools.partial(_mem_attn_token_kernel, bb=bb),
        grid=(ng // bb,),
        in_specs=[row, mem, mem],
        out_specs=row,
        out_shape=jax.ShapeDtypeStruct((ng, 1, d), F32),
        compiler_params=_params("parallel"),
        name="mem_attn_token",
    )(q, mk, mv)


def _pair_state(s):
    nb = s.shape[0]
    s = s.reshape(nb, RW_PAIRS, 2, HEAD_DIM, HEAD_DIM)
    return jnp.swapaxes(s, 2, 3).reshape(nb, RW_PAIRS, HEAD_DIM, LANES)


def _unpair_state(s):
    nb = s.shape[0]
    s = s.reshape(nb, RW_PAIRS, HEAD_DIM, 2, HEAD_DIM)
    return jnp.swapaxes(s, 2, 3).reshape(nb, RW_HEADS, HEAD_DIM, HEAD_DIM)


def kernel(x_prompt, x_sample, mem_prompt, state_wkv, state_shift, cache_win_k, cache_win_v,
           cache_mem_k, cache_mem_v, ln_g, ln_b, ffn_w1, ffn_w3, ffn_w2, w_in, rw_mu, rw_w0,
           rw_w_up, rw_a0, rw_a_up, rw_g_up, rw_k_k, rw_k_a, rw_r_k, rw_gn_g, rw_gn_b, sw_sinks,
           w_out, mem_wq, mem_wk, mem_wv, mem_wo):
    depth = ln_g.shape[0]
    bp, tp, d = x_prompt.shape
    bs, ts, _ = x_sample.shape
    assert ts == 1 and cache_win_k.shape[2] == WINDOW and tp % WINDOW == 0

    w1b, w3b, w2b = (w.astype(BF16) for w in (ffn_w1, ffn_w3, ffn_w2))
    w_in_b = w_in.astype(BF16)
    w_out_b = w_out.astype(BF16)
    wqb, wkb, wvb, wob = (w.astype(BF16) for w in (mem_wq, mem_wk, mem_wv, mem_wo))
    zpad = jnp.zeros((depth, D_W_LORA, RW_WIDTH), BF16)
    wup_b = jnp.concatenate([rw_w_up.astype(BF16), zpad], axis=1)
    aup_b = jnp.concatenate([zpad, rw_a_up.astype(BF16)], axis=1)
    gup_b = rw_g_up.astype(BF16)
    hid = jnp.arange(2 * RW_WIDTH) % RW_WIDTH // HEAD_DIM
    seg2 = (hid[:, None] == (jnp.arange(RW_WIDTH) // HEAD_DIM)[None, :]).astype(BF16)
    slopes = jnp.asarray([[_alibi_slope(h)] for h in range(SW_HEADS)], F32)
    row = lambda a: a.reshape(1, -1)

    def layer(l, x, nb, t, prev_fn, s0, swa_fn, mem_fn):
        x = ffn_ln(x, w1b, w3b, w2b, (l, 0), row(ln_g[l, 0]), row(ln_b[l, 0]))
        p_rw, q, k, v = matmul_multi(x, [w_in_b], l, splits=[RW_COLS, SW_WIDTH, KV_WIDTH, KV_WIDTH])
        *ops, g, bonus = rwkv_prep(
            p_rw, prev_fn(p_rw), row(rw_mu[l]), row(rw_w0[l]), wup_b[l], row(rw_a0[l]), aup_b[l],
            gup_b[l], row(rw_k_k[l]), row(rw_k_a[l]), row(rw_r_k[l]), seg2, seq_len=t)
        ops = [a.reshape(RW_PAIRS, nb, t, LANES) for a in ops]
        if t > 1:
            y, s_fin = wkv_chunked(*ops, s0)
        else:
            y, s_fin = wkv_step(*ops, s0)
        y_sw, win_k, win_v = swa_fn(q, k, v)
        shift = x.reshape(nb, t, d)[:, -1]
        x = mix_out_ln(x, y.reshape(RW_PAIRS, nb * t, LANES), bonus, g, y_sw, row(rw_gn_g[l]),
                       row(rw_gn_b[l]), seg2, row(ln_g[l, 1]), row(ln_b[l, 1]), w_out_b, l)
        x = mem_fn(x)
        x = ffn_ln(x, w1b, w3b, w2b, (l, 1), row(ln_g[l, 3]), row(ln_b[l, 3]))
        return x, _unpair_state(s_fin), shift, win_k, win_v

    xp = x_prompt.reshape(bp * tp, d)
    p_wkv, p_shift, p_wk, p_wv, p_mk, p_mv = [], [], [], [], [], []
    for l in range(depth):
        prev_prompt = lambda p_rw: p_rw

        def swa_p(q, k, v, l=l):
            k3 = k.reshape(bp, tp, KV_WIDTH)
            v3 = v.reshape(bp, tp, KV_WIDTH)
            y = swa_prompt(q.reshape(bp, tp, SW_WIDTH), k3, v3, sw_sinks[l])
            tail = lambda a: a[:, -WINDOW:].reshape(bp, WINDOW, SW_KV_HEADS, HEAD_DIM)
            return y.reshape(bp * tp, SW_WIDTH), tail(k3), tail(v3)

        mk, mv = matmul_multi(mem_prompt.reshape(bp * MEM_TOKENS, d), [wkb, wvb], l)
        mk = mk.reshape(bp, MEM_TOKENS, d)
        mv = mv.reshape(bp, MEM_TOKENS, d)

        def mem_p(x, l=l, mk=mk, mv=mv):
            return mem_block(x.reshape(bp, tp, d), wqb, mk, mv, wob, l, row(ln_g[l, 2]),
                             row(ln_b[l, 2])).reshape(bp * tp, d)

        s0 = jnp.zeros((bp, RW_PAIRS, HEAD_DIM, LANES), F32)
        xp, s_fin, shift, wk_, wv_ = layer(l, xp, bp, tp, prev_prompt, s0, swa_p, mem_p)
        p_wkv.append(s_fin)
        p_shift.append(shift)
        p_wk.append(wk_)
        p_wv.append(wv_)
        p_mk.append(mk.reshape(bp, MEM_TOKENS, MEM_HEADS, MEM_HEAD_DIM))
        p_mv.append(mv.reshape(bp, MEM_TOKENS, MEM_HEADS, MEM_HEAD_DIM))

    xs = x_sample.reshape(bs, d)
    s_wkv, s_shift, s_wk, s_wv = [], [], [], []
    for l in range(depth):
        def prev_sample(p_rw, l=l):
            (prev,) = matmul_multi(state_shift[l], [w_in_b], l, widths=[RW_COLS])
            return prev

        def swa_s(q, k, v, l=l):
            o, nk, nv = swa_sample(
                q.reshape(bs, SW_HEADS, HEAD_DIM), k.reshape(bs, 1, KV_WIDTH), v.reshape(bs, 1, KV_WIDTH),
                cache_win_k[l].reshape(bs, WINDOW, KV_WIDTH), cache_win_v[l].reshape(bs, WINDOW, KV_WIDTH),
                sw_sinks[l].reshape(SW_HEADS, 1), slopes)
            unflat = lambda a: a.reshape(bs, WINDOW, SW_KV_HEADS, HEAD_DIM)
            return o.reshape(bs, SW_WIDTH), unflat(nk), unflat(nv)

        def mem_s(x, l=l):
            (qm,) = matmul_multi(x, [wqb], l)
            o = mem_attn_token(qm.reshape(bs, 1, d), cache_mem_k, cache_mem_v, l).reshape(bs, d)
            return proj_ln(x, [o], [wob], l, row(ln_g[l, 2]), row(ln_b[l, 2]))

        xs, s_fin, shift, wk_, wv_ = layer(l, xs, bs, 1, prev_sample, _pair_state(state_wkv[l]),
                                           swa_s, mem_s)
        s_wkv.append(s_fin)
        s_shift.append(shift)
        s_wk.append(wk_)
        s_wv.append(wv_)

    return (xp.reshape(bp, tp, d), xs.reshape(bs, 1, d),
            jnp.stack(p_wkv), jnp.stack(p_shift), jnp.stack(p_wk), jnp.stack(p_wv),
            jnp.stack(p_mk), jnp.stack(p_mv),
            jnp.stack(s_wkv), jnp.stack(s_shift), jnp.stack(s_wk), jnp.stack(s_wv))
```

```python
import functools

import jax
import jax.numpy as jnp
from jax import lax
from jax.experimental import pallas as pl
from jax.experimental.pallas import tpu as pltpu

F32 = jnp.float32
BF16 = jnp.bfloat16

D_MODEL = 1024
HEAD_DIM = 64
RW_WIDTH = 512
RW_HEADS = 8
RW_PAIRS = RW_HEADS // 2
SW_WIDTH = 512
SW_HEADS = 8
SW_KV_HEADS = 2
SW_GROUP = SW_HEADS // SW_KV_HEADS
KV_WIDTH = SW_KV_HEADS * HEAD_DIM
WINDOW = 128
D_W_LORA = 64
D_A_LORA = 64
D_G_LORA = 128
RW_COLS = 3 * RW_WIDTH + D_W_LORA + D_A_LORA + D_G_LORA
LORA_WA_START = 3 * RW_WIDTH
LORA_G_START = LORA_WA_START + D_W_LORA + D_A_LORA
MEM_TOKENS = 256
MEM_HEADS = 4
MEM_HEAD_DIM = D_MODEL // MEM_HEADS
D_FF = 2816
DEPTH = 4
ALPHA = (2.0 * DEPTH) ** 0.25
LN_EPS = 1e-5
GN_EPS = 64e-5
NORM_EPS = 1e-12

CHUNK = 64
FF_TILE = 256
LANES = 128
VMEM_LIMIT = 56 * 1024 * 1024


def _params(*semantics):
    return pltpu.CompilerParams(dimension_semantics=semantics, vmem_limit_bytes=VMEM_LIMIT)


def _row_tile(n, want):
    return want if n % want == 0 else n


def _layer_norm(z, g, b):
    mu = jnp.mean(z, axis=-1, keepdims=True)
    d = z - mu
    var = jnp.mean(d * d, axis=-1, keepdims=True)
    return d * lax.rsqrt(var + LN_EPS) * g + b


def _sigmoid(x):
    return 1.0 / (1.0 + jnp.exp(-x))


def _split3(x):
    hi = x.astype(BF16)
    r1 = x - hi.astype(F32)
    mid = r1.astype(BF16)
    lo = (r1 - mid.astype(F32)).astype(BF16)
    return jnp.concatenate([hi, mid, lo], axis=1)


def _split2(x):
    hi = x.astype(BF16)
    lo = (x - hi.astype(F32)).astype(BF16)
    return jnp.concatenate([hi, lo], axis=1)


def _mm_kernel(x_ref, *refs, n_w, splits):
    xb = x_ref[...].astype(BF16)
    prods = [jnp.dot(xb, w_ref[...], preferred_element_type=F32) for w_ref in refs[:n_w]]
    if splits is not None:
        edges = [sum(splits[:i]) for i in range(len(splits) + 1)]
        prods = [prods[0][:, lo:hi] for lo, hi in zip(edges[:-1], edges[1:])]
    for p, o_ref in zip(prods, refs[n_w:]):
        o_ref[...] = p


def _layer_spec(w, layer, rows=None, cols=None):
    shape = (rows or w.shape[1], cols or w.shape[2])
    return pl.BlockSpec((None,) + shape, lambda *_: (layer, 0, 0))


def matmul_multi(x, ws, layer, widths=None, splits=None, tm=512):
    n, k = x.shape
    tm = _row_tile(n, tm)
    widths = widths or [w.shape[2] for w in ws]
    outs = splits or widths
    return pl.pallas_call(
        functools.partial(_mm_kernel, n_w=len(ws), splits=splits),
        grid=(n // tm,),
        in_specs=[pl.BlockSpec((tm, k), lambda i: (i, 0))]
        + [_layer_spec(w, layer, cols=c) for w, c in zip(ws, widths)],
        out_specs=[pl.BlockSpec((tm, c), lambda i: (i, 0)) for c in outs],
        out_shape=[jax.ShapeDtypeStruct((n, c), F32) for c in outs],
        compiler_params=_params("parallel"),
        name="matmul_multi",
    )(x, *ws)


def _ffn_kernel(x_ref, w1_ref, w3_ref, w2_ref, g_ref, b_ref, o_ref, xb_ref, acc_ref, *, n_ff):
    xb_ref[...] = x_ref[...].astype(BF16)

    def part(c):
        cols = slice(c * FF_TILE, (c + 1) * FF_TILE)
        xb = xb_ref[...]
        h1 = jnp.dot(xb, w1_ref[:, cols], preferred_element_type=F32)
        h3 = jnp.dot(xb, w3_ref[:, cols], preferred_element_type=F32)
        h = (h1 * _sigmoid(h1)) * h3
        return jnp.dot(h.astype(BF16), w2_ref[cols, :], preferred_element_type=F32)

    acc_ref[...] = part(0)
    for c in range(1, n_ff):
        acc_ref[...] += part(c)
    z = ALPHA * x_ref[...] + 0.5 * acc_ref[...]
    o_ref[...] = _layer_norm(z, g_ref[...], b_ref[...])


def ffn_ln(x, w1, w3, w2, lead, g, b, tm=1024):
    n, d = x.shape
    tm = _row_tile(n, tm)
    n_ff = D_FF // FF_TILE
    resident = lambda a: pl.BlockSpec((None,) * len(lead) + a.shape[len(lead):],
                                      lambda i: lead + (0, 0), pipeline_mode=pl.Buffered(1))
    return pl.pallas_call(
        functools.partial(_ffn_kernel, n_ff=n_ff),
        grid=(n // tm,),
        in_specs=[
            pl.BlockSpec((tm, d), lambda i: (i, 0)),
            resident(w1), resident(w3), resident(w2),
            pl.BlockSpec((1, d), lambda i: (0, 0)),
            pl.BlockSpec((1, d), lambda i: (0, 0)),
        ],
        out_specs=pl.BlockSpec((tm, d), lambda i: (i, 0)),
        out_shape=jax.ShapeDtypeStruct((n, d), F32),
        scratch_shapes=[pltpu.VMEM((tm, d), BF16), pltpu.VMEM((tm, d), F32)],
        compiler_params=_params("parallel"),
        name="ffn_ln",
    )(x, w1, w3, w2, g, b)


def _proj_ln_kernel(x_ref, *refs, n_in):
    a_refs = refs[:n_in]
    w_refs = refs[n_in:2 * n_in]
    g_ref, b_ref, o_ref = refs[2 * n_in:]
    f = None
    for a_ref, w_ref in zip(a_refs, w_refs):
        t = jnp.dot(a_ref[...].astype(BF16), w_ref[...], preferred_element_type=F32)
        f = t if f is None else f + t
    o_ref[...] = _layer_norm(ALPHA * x_ref[...] + f, g_ref[...], b_ref[...])


def proj_ln(x, acts, ws, layer, g, b, tm=512):
    n, d = x.shape
    tm = _row_tile(n, tm)
    n_in = len(acts)
    return pl.pallas_call(
        functools.partial(_proj_ln_kernel, n_in=n_in),
        grid=(n // tm,),
        in_specs=[pl.BlockSpec((tm, d), lambda i: (i, 0))]
        + [pl.BlockSpec((tm, a.shape[1]), lambda i: (i, 0)) for a in acts]
        + [_layer_spec(w, layer) for w in ws]
        + [pl.BlockSpec((1, d), lambda i: (0, 0))] * 2,
        out_specs=pl.BlockSpec((tm, d), lambda i: (i, 0)),
        out_shape=jax.ShapeDtypeStruct((n, d), F32),
        compiler_params=_params("parallel"),
        name="proj_ln",
    )(x, *acts, *ws, g, b)


def _rwkv_prep_kernel(p_ref, prev_ref, mu_ref, w0_ref, wup_ref, a0_ref, aup_ref, gup_ref,
                      kk_ref, ka_ref, rk_ref, seg_ref, *refs, chunked, tiles_per_seq):
    if chunked:
        tril_ref, ones_ref = refs[:2]
        refs = refs[2:]
    g_out, bonus_out = refs[-2:]
    p = p_ref[...]
    if chunked:
        first = (pl.program_id(0) % tiles_per_seq) == 0
        above = jnp.where(first, 0.0, prev_ref[7:8, :])
        top = lax.broadcasted_iota(jnp.int32, p.shape, 0) == 0
        prev = jnp.where(top, above, pltpu.roll(p, 1, 0))
    else:
        prev = prev_ref[...]
    xm = p + (prev - p) * mu_ref[...]
    r = xm[:, 0:RW_WIDTH]
    k = xm[:, RW_WIDTH:2 * RW_WIDTH]
    v = xm[:, 2 * RW_WIDTH:3 * RW_WIDTH]
    wa = xm[:, LORA_WA_START:LORA_G_START]
    gl = xm[:, LORA_G_START:RW_COLS]

    def seg_sum(x):
        return jnp.dot(_split2(x), seg_ref[...], preferred_element_type=F32)

    lw = jnp.dot(jnp.tanh(wa).astype(BF16), wup_ref[...], preferred_element_type=F32)
    la = jnp.dot(wa.astype(BF16), aup_ref[...], preferred_element_type=F32)
    z = -(w0_ref[...] + lw)
    softplus = jnp.maximum(z, 0.0) + jnp.log(1.0 + jnp.exp(-jnp.abs(z)))
    w_log = -softplus - 0.5
    log_decay = -jnp.exp(w_log)
    a = _sigmoid(a0_ref[...] + la)
    g = jnp.dot(_sigmoid(gl).astype(BF16), gup_ref[...], preferred_element_type=F32)
    kk = k * kk_ref[...]
    nrm = jnp.sqrt(seg_sum(kk * kk))
    kk = kk / jnp.maximum(nrm, NORM_EPS)
    k_mod = k * (1.0 + (a - 1.0) * ka_ref[...])
    kka = kk * a
    if chunked:
        parts = _split3(log_decay)

        def time_sum(m_ref):
            s3 = jnp.dot(m_ref[...], parts, preferred_element_type=F32)
            return s3[:, :RW_WIDTH] + s3[:, RW_WIDTH:2 * RW_WIDTH] + s3[:, 2 * RW_WIDTH:]

        cum = time_sum(tril_ref)
        tot = time_sum(ones_ref)
        grow = jnp.exp(-cum)
        rest = jnp.exp(tot - cum)
        outs = (kk * jnp.exp(cum - log_decay), r * jnp.exp(cum), -kka * grow, k_mod * grow,
                -kka * rest, k_mod * rest, v, jnp.exp(tot))
    else:
        outs = (r, jnp.exp(log_decay), k_mod, v, kk, kka)
    for o_ref, val in zip(refs, outs):
        if chunked:
            for pr in range(RW_PAIRS):
                o_ref[pr] = val[:, pr * LANES:(pr + 1) * LANES].astype(o_ref.dtype)
        else:
            o_ref[...] = val.T
    g_out[...] = g
    bonus_out[...] = seg_sum(r * k_mod * rk_ref[...]) * v


def rwkv_prep(p, prev, mu, w0, wup, a0, aup, gup, k_k, k_a, r_k, seg, seq_len, tm=256):
    n = p.shape[0]
    chunked = seq_len > 1
    tm = _row_tile(n, tm)
    row = lambda c: pl.BlockSpec((tm, c), lambda i: (i, 0))
    prev_spec = row(RW_COLS)
    if chunked:
        assert seq_len % tm == 0
        prev_spec = pl.BlockSpec((8, RW_COLS), lambda i: (jnp.maximum(i * (tm // 8) - 1, 0), 0))
    full = lambda a: pl.BlockSpec(a.shape, lambda i: (0, 0))
    consts = (mu, w0, wup, a0, aup, gup, k_k, k_a, r_k, seg)
    dtypes = [F32] * 6
    if chunked:
        assert tm % CHUNK == 0
        t_idx = jnp.arange(tm)
        same = (t_idx[:, None] // CHUNK) == (t_idx[None, :] // CHUNK)
        consts += ((same & (t_idx[None, :] <= t_idx[:, None])).astype(BF16), same.astype(BF16))
        dtypes = [BF16] * 7 + [F32]
    if chunked:
        op_spec = pl.BlockSpec((RW_PAIRS, tm, LANES), lambda i: (0, i, 0))
        op_shape = (RW_PAIRS, n, LANES)
    else:
        op_spec = pl.BlockSpec((RW_WIDTH, tm), lambda i: (0, i))
        op_shape = (RW_WIDTH, n)
    return pl.pallas_call(
        functools.partial(_rwkv_prep_kernel, chunked=chunked, tiles_per_seq=max(seq_len // tm, 1)),
        grid=(n // tm,),
        in_specs=[row(RW_COLS), prev_spec] + [full(c) for c in consts],
        out_specs=[op_spec] * len(dtypes) + [row(RW_WIDTH)] * 2,
        out_shape=[jax.ShapeDtypeStruct(op_shape, dt) for dt in dtypes]
        + [jax.ShapeDtypeStruct((n, RW_WIDTH), F32)] * 2,
        compiler_params=_params("parallel"),
        name="rwkv_prep",
    )(p, prev, *consts)


def _wkv_step_kernel(r_ref, w_ref, k_ref, v_ref, kk_ref, kka_ref, s0_ref, y_ref, s_ref):
    r, w, k, kk, kka = (ref[...][None] for ref in (r_ref, w_ref, k_ref, kk_ref, kka_ref))
    rows = 8
    for c in range(HEAD_DIM // rows):
        vals = slice(c * rows, (c + 1) * rows)
        s = s0_ref[0, vals]
        sa = jnp.sum(s * kk, axis=1, keepdims=True)
        s = s * w - sa * kka + v_ref[vals, :][:, None, :] * k
        s_ref[0, vals] = s
        y_ref[vals, :] = jnp.sum(s * r, axis=1)


def wkv_step(r, w, k, v, kk, kka, s0, layer):
    _, nb = r.shape
    vec = pl.BlockSpec((HEAD_DIM, nb), lambda h: (h, 0))
    return pl.pallas_call(
        _wkv_step_kernel,
        grid=(RW_HEADS,),
        in_specs=[vec] * 6 + [pl.BlockSpec((None, 1, HEAD_DIM, HEAD_DIM, nb), lambda h: (layer, h, 0, 0, 0))],
        out_specs=[vec, pl.BlockSpec((1, HEAD_DIM, HEAD_DIM, nb), lambda h: (h, 0, 0, 0))],
        out_shape=[jax.ShapeDtypeStruct((RW_WIDTH, nb), F32),
                   jax.ShapeDtypeStruct((RW_HEADS, HEAD_DIM, HEAD_DIM, nb), F32)],
        compiler_params=_params("parallel"),
        name="wkv_step",
    )(r, w, k, v, kk, kka, s0)


def _dot_nt(a, b):
    return lax.dot_general(a, b, (((1,), (1,)), ((), ())), preferred_element_type=F32)


def _dot_tn(a, b):
    return lax.dot_general(a, b, (((0,), (0,)), ((), ())), preferred_element_type=F32)


def _dot(a, b):
    return jnp.dot(a, b, preferred_element_type=F32)


def _wkv_chunk_kernel(qt_ref, rt_ref, bh_ref, kh_ref, bb_ref, kb_ref, v_ref, gc_ref, s0_ref,
                      y_ref, s_ref, *, nb, tc):
    @pl.when(pl.program_id(1) == 0)
    def _():
        s_ref[...] = s0_ref[...]

    c = CHUNK
    assert c == HEAD_DIM and 2 * c == LANES
    row = lax.broadcasted_iota(jnp.int32, (c, LANES), 0)
    lane = lax.broadcasted_iota(jnp.int32, (c, LANES), 1)
    head_a = lane < HEAD_DIM
    head_a2 = lax.broadcasted_iota(jnp.int32, (2 * c, LANES), 1) < HEAD_DIM
    strict = (lane % c) < row
    eye = jnp.where((lane % c) == row, 1.0, 0.0)
    row4 = lax.broadcasted_iota(jnp.int32, (c, 2 * LANES), 0)
    col4 = lax.broadcasted_iota(jnp.int32, (c, 2 * LANES), 1)
    kh_cols = (col4 >= c) & (col4 < 3 * c)
    m_mask = kh_cols & ((col4 % c) < row4)
    incl = (col4 % c) <= row4
    pairs = [(b, p) for b in range(nb) for p in range(RW_PAIRS)]
    bf = lambda x: x.astype(BF16)
    stack = lambda *xs: jnp.concatenate(xs, axis=0)
    zero = jnp.zeros((), BF16)
    only_a = lambda x: jnp.where(head_a if x.shape[0] == c else head_a2, x, zero)
    only_b = lambda x: jnp.where(head_a if x.shape[0] == c else head_a2, zero, x)
    per_head = lambda x: stack(only_a(x), only_b(x))
    zeros_c = jnp.zeros((c, LANES), BF16)

    def inverse_stages(ci, ready):
        ld = lambda ref, b, p: ref[p, b, pl.ds(ci * c, c), :]
        qr = [stack(ld(qt_ref, b, p), ld(rt_ref, b, p)) for b, p in pairs]
        bh = [ld(bh_ref, b, p) for b, p in pairs]
        kh = [ld(kh_ref, b, p) for b, p in pairs]
        e = [_dot_nt(q, stack(only_a(stack(x, y)), only_b(stack(y, x)))) for q, x, y in zip(qr, bh, kh)]
        yield
        pw = [jnp.where(strict, jnp.where(head_a, x[:c, :LANES], x[:c, LANES:]), 0.0) for x in e]
        inv = [eye + n for n in pw]
        for _ in range(5):
            pw = [_dot(x, per_head(x)) for x in [bf(x) for x in pw]]
            yield
            inv = [t + _dot(bf(x), per_head(bf(t))) for x, t in zip(pw, inv)]
            yield
        ready[ci] = (qr, e, inv)

    def state_stages(ci, ready):
        qr, e, inv = ready.pop(ci)
        ld = lambda ref, b, p: ref[p, b, pl.ds(ci * c, c), :]
        s = [s_ref[b, p] for b, p in pairs]
        ff = [_dot_nt(q, per_head(bf(x))) for q, x in zip(qr, s)]
        v = [ld(v_ref, b, p) for b, p in pairs]
        yield
        g = [f[:c] + _dot(bf(jnp.where(m_mask, x[:c], 0.0)), stack(zeros_c, only_a(w), only_b(w), zeros_c))
             for f, x, w in zip(ff, e, v)]
        yield
        u = [bf(_dot(bf(t), per_head(bf(x)))) for t, x in zip(inv, g)]
        yield
        y = [f[c:] + _dot(bf(jnp.where(incl, x[c:], 0.0)), stack(only_a(uu), only_a(w), only_b(w), only_b(uu)))
             for f, x, uu, w in zip(ff, e, u, v)]
        add = [_dot_tn(stack(uu, w), stack(ld(bb_ref, b, p), ld(kb_ref, b, p)))
               for uu, w, (b, p) in zip(u, v, pairs)]
        yield
        for i, (b, p) in enumerate(pairs):
            y_ref[p, b, pl.ds(ci * c, c), :] = y[i]
            gc = gc_ref[p, b, pl.ds(ci * c, 8), :]
            kept = (s[i].reshape(c // 8, 8, LANES) * gc).reshape(c, LANES)
            s_ref[b, p] = kept + jnp.where(head_a, add[i][:c], add[i][c:])
        yield

    n_chunks = tc // c
    state_len, inv_len = 5, 11
    ready, inverses, state = {}, {}, None
    for tick in range(-inv_len, state_len * n_chunks):
        for ci in range(n_chunks):
            if ci not in inverses and state_len * ci - inv_len <= tick:
                inverses[ci] = inverse_stages(ci, ready)
        for gen in inverses.values():
            next(gen, None)
        if tick >= 0:
            if tick % state_len == 0:
                state = state_stages(tick // state_len, ready)
            next(state)


def wkv_chunked(qt, rt, bh, kh, bb, kb, v, gc, s0, tc=512):
    _, nb, t, _ = qt.shape
    seq = pl.BlockSpec((RW_PAIRS, nb, tc, LANES), lambda i, j: (0, 0, j, 0))
    st = pl.BlockSpec((nb, RW_PAIRS, HEAD_DIM, LANES), lambda i, j: (0, 0, 0, 0))
    return pl.pallas_call(
        functools.partial(_wkv_chunk_kernel, nb=nb, tc=tc),
        grid=(1, t // tc),
        in_specs=[seq] * 8 + [st],
        out_specs=[seq, st],
        out_shape=[jax.ShapeDtypeStruct((RW_PAIRS, nb, t, LANES), F32),
                   jax.ShapeDtypeStruct(s0.shape, F32)],
        compiler_params=_params("arbitrary", "arbitrary"),
        name="wkv_chunked",
    )(qt, rt, bh, kh, bb, kb, v, gc, s0)


def _mix_out_kernel(x_ref, y_ref, bonus_ref, g_ref, ysw_ref, gng_ref, gnb_ref, seg_ref, lng_ref, lnb_ref,
                    wout_ref, o_ref):
    def seg_mean(x):
        return jnp.dot(_split2(x), seg_ref[...], preferred_element_type=F32) * (1.0 / HEAD_DIM)

    y = jnp.concatenate([y_ref[pr] for pr in range(RW_PAIRS)], axis=1)
    d = y - seg_mean(y)
    var = seg_mean(d * d)
    yn = d * lax.rsqrt(var + GN_EPS) * gng_ref[...] + gnb_ref[...]
    y_rw = (yn + bonus_ref[...]) * g_ref[...]
    f = _dot(y_rw.astype(BF16), wout_ref[:RW_WIDTH, :]) + _dot(ysw_ref[...].astype(BF16), wout_ref[RW_WIDTH:, :])
    o_ref[...] = _layer_norm(ALPHA * x_ref[...] + f, lng_ref[...], lnb_ref[...])


def mix_out_ln(x, y, bonus, g, y_sw, gn_g, gn_b, seg, ln_g, ln_b, w_out, layer, tm=512):
    n, d = x.shape
    tm = _row_tile(n, tm)
    row = lambda c: pl.BlockSpec((tm, c), lambda i: (i, 0))
    full = lambda a: pl.BlockSpec(a.shape, lambda i: (0, 0))
    consts = (gn_g, gn_b, seg, ln_g, ln_b)
    return pl.pallas_call(
        _mix_out_kernel,
        grid=(n // tm,),
        in_specs=[row(d), pl.BlockSpec((RW_PAIRS, tm, LANES), lambda i: (0, i, 0)), row(RW_WIDTH),
                  row(RW_WIDTH), row(SW_WIDTH)] + [full(c) for c in consts] + [_layer_spec(w_out, layer)],
        out_specs=row(d),
        out_shape=jax.ShapeDtypeStruct((n, d), F32),
        compiler_params=_params("parallel"),
        name="mix_out_ln",
    )(x, y, bonus, g, y_sw, *consts, w_out)


def _alibi_slope(h):
    return 2.0 ** (-8.0 * (h + 1) / SW_HEADS)


def _swa_prompt_kernel(sink_ref, bias0_ref, bias1_ref, q_ref, kp_ref, kc_ref, vp_ref, vc_ref, o_ref, *, qb_per_step):
    qb = (q_ref[0] * HEAD_DIM ** -0.5).astype(BF16)
    kw = jnp.concatenate([kp_ref[0], kc_ref[0]], axis=0).astype(BF16)
    vw = jnp.concatenate([vp_ref[0], vc_ref[0]], axis=0).astype(BF16)
    kx = pltpu.roll(kw, HEAD_DIM, 1)
    vx = pltpu.roll(vw, HEAD_DIM, 1)
    low_kv = lax.broadcasted_iota(jnp.int32, kw.shape, 1) < HEAD_DIM
    low_q = lax.broadcasted_iota(jnp.int32, (WINDOW, LANES), 1) < HEAD_DIM
    zero = jnp.zeros((), BF16)
    kc = [jnp.where(low_kv, kw, kx), jnp.where(low_kv, kx, kw)]
    vc = [jnp.where(low_kv, vw, vx), jnp.where(low_kv, vx, vw)]
    items = [(j, h) for j in range(qb_per_step) for h in range(SW_HEADS)]
    rows = lambda j: slice(j * WINDOW, (j + 1) * WINDOW)
    keys = lambda j: slice(j * WINDOW, (j + 2) * WINDOW)
    bias = lambda j, h: bias0_ref[h] if j == 0 else bias1_ref[h]
    qh = []
    for j, h in items:
        q2 = qb[rows(j), (h // 2) * LANES:(h // 2 + 1) * LANES]
        qh.append(jnp.where(low_q, q2, zero) if h % 2 == 0 else jnp.where(low_q, zero, q2))
    logits = [_dot_nt(x, kc[h // SW_GROUP][keys(j)]) + bias(j, h) for x, (j, h) in zip(qh, items)]
    m = [jnp.maximum(jnp.max(x, axis=-1, keepdims=True), sink_ref[h]) for x, (j, h) in zip(logits, items)]
    e = [jnp.exp(x - mx) for x, mx in zip(logits, m)]
    den = [jnp.sum(x, axis=-1, keepdims=True) + jnp.exp(sink_ref[h] - mx) for x, mx, (j, h) in zip(e, m, items)]
    o = [_dot(x.astype(BF16), vc[h // SW_GROUP][keys(j)]) / dn for x, dn, (j, h) in zip(e, den, items)]
    for i in range(0, len(items), 2):
        j, h = items[i]
        o_ref[0, rows(j), (h // 2) * LANES:(h // 2 + 1) * LANES] = jnp.where(low_q, o[i], o[i + 1])


def swa_prompt(q, k, v, sinks, qb_per_step=8):
    nb, t, _ = q.shape
    rows = qb_per_step * WINDOW
    assert t % rows == 0
    i = jnp.arange(WINDOW)[:, None]
    j = jnp.arange(2 * WINDOW)[None, :]
    dist = WINDOW + i - j
    valid = (dist >= 0) & (dist < WINDOW)
    slopes = jnp.asarray([_alibi_slope(h) for h in range(SW_HEADS)], F32)[:, None, None]
    table = lambda ok: jnp.where(ok[None], -slopes * dist.astype(F32)[None], -jnp.inf)
    bias = jnp.stack([table(valid & (j >= WINDOW)), table(valid)])
    table_spec = lambda pick: pl.BlockSpec((None, SW_HEADS, WINDOW, 2 * WINDOW), lambda b, n: (pick(n), 0, 0, 0))
    cur = lambda w: pl.BlockSpec((1, rows, w), lambda b, n: (b, n, 0))
    prv = lambda w: pl.BlockSpec((1, WINDOW, w), lambda b, n: (b, jnp.maximum(n * qb_per_step - 1, 0), 0))
    return pl.pallas_call(
        functools.partial(_swa_prompt_kernel, qb_per_step=qb_per_step),
        grid=(nb, t // rows),
        in_specs=[pl.BlockSpec(memory_space=pltpu.SMEM),
                  table_spec(lambda n: jnp.minimum(n, 1)), table_spec(lambda n: 1),
                  cur(SW_WIDTH), prv(KV_WIDTH), cur(KV_WIDTH), prv(KV_WIDTH), cur(KV_WIDTH)],
        out_specs=cur(SW_WIDTH),
        out_shape=jax.ShapeDtypeStruct((nb, t, SW_WIDTH), F32),
        compiler_params=_params("parallel", "parallel"),
        name="swa_prompt",
    )(sinks, bias, bias, q, k, k, v, v)


def _swa_sample_kernel(sink_ref, slope_ref, q_ref, kn_ref, vn_ref, ck_ref, cv_ref,
                       o_ref, nk_ref, nv_ref, *, bb):
    last = lax.broadcasted_iota(jnp.int32, (WINDOW, KV_WIDTH), 0) == WINDOW - 1
    j = lax.broadcasted_iota(jnp.int32, (SW_HEADS, WINDOW), 1)
    bias = slope_ref[...] * (WINDOW - 1 - j).astype(F32)
    sink = sink_ref[...]
    seqs = range(bb)
    kw = [jnp.where(last, kn_ref[b], pltpu.roll(ck_ref[b], WINDOW - 1, 0)) for b in seqs]
    vw = [jnp.where(last, vn_ref[b], pltpu.roll(cv_ref[b], WINDOW - 1, 0)) for b in seqs]
    s = [_dot_nt(q_ref[b].astype(BF16), kw[b].astype(BF16)) * HEAD_DIM ** -0.5 - bias for b in seqs]
    m = [jnp.maximum(jnp.max(x, axis=-1, keepdims=True), sink) for x in s]
    e = [jnp.exp(x - mx) for x, mx in zip(s, m)]
    prob = [x / (jnp.sum(x, axis=-1, keepdims=True) + jnp.exp(sink - mx)) for x, mx in zip(e, m)]
    o = [_dot(prob[b].astype(BF16), vw[b].astype(BF16)) for b in seqs]
    for b in seqs:
        nk_ref[b] = kw[b]
        nv_ref[b] = vw[b]
        o_ref[b] = o[b]


def swa_sample(q, kn, vn, ck, cv, sinks, slopes, bb=8):
    nb = q.shape[0]
    assert nb % bb == 0
    own = (jnp.arange(SW_HEADS) // SW_GROUP)[:, None] == jnp.arange(SW_KV_HEADS)[None, :]
    q2 = jnp.where(own[None, :, :, None], q[:, :, None, :], 0.0).reshape(nb, SW_HEADS, KV_WIDTH)
    per_b = lambda s: pl.BlockSpec((bb,) + s, lambda b: (b, 0, 0))
    full = lambda a: pl.BlockSpec(a.shape, lambda b: (0, 0))
    o2, nk, nv = pl.pallas_call(
        functools.partial(_swa_sample_kernel, bb=bb),
        grid=(nb // bb,),
        in_specs=[full(sinks), full(slopes), per_b((SW_HEADS, KV_WIDTH)), per_b((1, KV_WIDTH)),
                  per_b((1, KV_WIDTH)), per_b((WINDOW, KV_WIDTH)), per_b((WINDOW, KV_WIDTH))],
        out_specs=[per_b((SW_HEADS, KV_WIDTH)), per_b((WINDOW, KV_WIDTH)), per_b((WINDOW, KV_WIDTH))],
        out_shape=[jax.ShapeDtypeStruct((nb, SW_HEADS, KV_WIDTH), F32),
                   jax.ShapeDtypeStruct((nb, WINDOW, KV_WIDTH), F32),
                   jax.ShapeDtypeStruct((nb, WINDOW, KV_WIDTH), F32)],
        compiler_params=_params("parallel"),
        name="swa_sample",
    )(sinks, slopes, q2, kn, vn, ck, cv)
    o = jnp.sum(jnp.where(own[None, :, :, None], o2.reshape(nb, SW_HEADS, SW_KV_HEADS, HEAD_DIM), 0.0), axis=2)
    return o, nk, nv


def _mem_block_kernel(x_ref, wq_ref, mk_ref, mv_ref, wo_ref, g_ref, b_ref, o_ref):
    x = x_ref[0]
    qb = _dot(x.astype(BF16), wq_ref[...]).astype(BF16)
    cols = [slice(h * MEM_HEAD_DIM, (h + 1) * MEM_HEAD_DIM) for h in range(MEM_HEADS)]
    s = [_dot_nt(qb[:, c], mk_ref[0, :, c].astype(BF16)) * MEM_HEAD_DIM ** -0.5 for c in cols]
    e = [jnp.exp(x - jnp.max(x, axis=-1, keepdims=True)) for x in s]
    den = [jnp.sum(x, axis=-1, keepdims=True) for x in e]
    outs = [_dot(x.astype(BF16), mv_ref[0, :, c].astype(BF16)) / d for x, c, d in zip(e, cols, den)]
    o = jnp.concatenate(outs, axis=1).astype(BF16)
    o_ref[0] = _layer_norm(ALPHA * x + _dot(o, wo_ref[...]), g_ref[...], b_ref[...])


def mem_block(x, wq, mk, mv, wo, layer, g, b, tm=512):
    ng, t, d = x.shape
    tm = _row_tile(t, tm)
    row = pl.BlockSpec((1, tm, d), lambda gi, i: (gi, i, 0))
    mem = pl.BlockSpec((1, MEM_TOKENS, d), lambda gi, i: (gi, 0, 0))
    full = lambda a: pl.BlockSpec(a.shape, lambda gi, i: (0, 0))
    return pl.pallas_call(
        _mem_block_kernel,
        grid=(ng, t // tm),
        in_specs=[row, _layer_spec(wq, layer), mem, mem, _layer_spec(wo, layer), full(g), full(b)],
        out_specs=row,
        out_shape=jax.ShapeDtypeStruct((ng, t, d), F32),
        compiler_params=_params("parallel", "parallel"),
        name="mem_block",
    )(x, wq, mk, mv, wo, g, b)


def _mem_attn_token_kernel(q_ref, mk_ref, mv_ref, o_ref, *, bb):
    rows = MEM_TOKENS * MEM_HEADS
    lane = lax.broadcasted_iota(jnp.int32, (MEM_HEADS, rows), 1)
    head = lax.broadcasted_iota(jnp.int32, (MEM_HEADS, rows), 0)
    own = (lane % MEM_HEADS) == head
    for b in range(bb):
        q = q_ref[b]
        q4 = jnp.concatenate([q[:, h * MEM_HEAD_DIM:(h + 1) * MEM_HEAD_DIM] for h in range(MEM_HEADS)],
                             axis=0).astype(BF16)
        k2 = mk_ref[b].reshape(rows, MEM_HEAD_DIM).astype(BF16)
        v2 = mv_ref[b].reshape(rows, MEM_HEAD_DIM).astype(BF16)
        s = jnp.where(own, _dot_nt(q4, k2) * MEM_HEAD_DIM ** -0.5, -jnp.inf)
        m = jnp.max(s, axis=-1, keepdims=True)
        e = jnp.exp(s - m)
        prob = e / jnp.sum(e, axis=-1, keepdims=True)
        o = _dot(prob.astype(BF16), v2)
        for h in range(MEM_HEADS):
            o_ref[b, :, h * MEM_HEAD_DIM:(h + 1) * MEM_HEAD_DIM] = o[h:h + 1]


def mem_attn_token(q, mk, mv, layer, bb=4):
    ng, _, d = q.shape
    assert ng % bb == 0
    row = pl.BlockSpec((bb, 1, d), lambda g: (g, 0, 0))
    mem = pl.BlockSpec((None, bb, MEM_TOKENS, MEM_HEADS, MEM_HEAD_DIM), lambda g: (layer, g, 0, 0, 0))
    return pl.pallas_call(
        functools.partial(_mem_attn_token_kernel, bb=bb),
        grid=(ng // bb,),
        in_specs=[row, mem, mem],
        out_specs=row,
        out_shape=jax.ShapeDtypeStruct((ng, 1, d), F32),
        compiler_params=_params("parallel"),
        name="mem_attn_token",
    )(q, mk, mv)


def _unpair_state(s):
    nb = s.shape[0]
    s = s.reshape(nb, RW_PAIRS, HEAD_DIM, 2, HEAD_DIM)
    return jnp.swapaxes(s, 2, 3).reshape(nb, RW_HEADS, HEAD_DIM, HEAD_DIM)


def kernel(x_prompt, x_sample, mem_prompt, state_wkv, state_shift, cache_win_k, cache_win_v,
           cache_mem_k, cache_mem_v, ln_g, ln_b, ffn_w1, ffn_w3, ffn_w2, w_in, rw_mu, rw_w0,
           rw_w_up, rw_a0, rw_a_up, rw_g_up, rw_k_k, rw_k_a, rw_r_k, rw_gn_g, rw_gn_b, sw_sinks,
           w_out, mem_wq, mem_wk, mem_wv, mem_wo):
    depth = ln_g.shape[0]
    bp, tp, d = x_prompt.shape
    bs, ts, _ = x_sample.shape
    assert ts == 1 and cache_win_k.shape[2] == WINDOW and tp % WINDOW == 0

    w1b, w3b, w2b = (w.astype(BF16) for w in (ffn_w1, ffn_w3, ffn_w2))
    w_in_b = w_in.astype(BF16)
    w_out_b = w_out.astype(BF16)
    wqb, wkb, wvb, wob = (w.astype(BF16) for w in (mem_wq, mem_wk, mem_wv, mem_wo))
    zpad = jnp.zeros((depth, D_W_LORA, RW_WIDTH), BF16)
    wup_b = jnp.concatenate([rw_w_up.astype(BF16), zpad], axis=1)
    aup_b = jnp.concatenate([zpad, rw_a_up.astype(BF16)], axis=1)
    gup_b = rw_g_up.astype(BF16)
    hid = jnp.arange(2 * RW_WIDTH) % RW_WIDTH // HEAD_DIM
    seg2 = (hid[:, None] == (jnp.arange(RW_WIDTH) // HEAD_DIM)[None, :]).astype(BF16)
    slopes = jnp.asarray([[_alibi_slope(h)] for h in range(SW_HEADS)], F32)
    row = lambda a: a.reshape(1, -1)

    def layer(l, x, nb, t, prev_fn, s0, swa_fn, mem_fn):
        x = ffn_ln(x, w1b, w3b, w2b, (l, 0), row(ln_g[l, 0]), row(ln_b[l, 0]))
        p_rw, q, k, v = matmul_multi(x, [w_in_b], l, splits=[RW_COLS, SW_WIDTH, KV_WIDTH, KV_WIDTH])
        *ops, g, bonus = rwkv_prep(
            p_rw, prev_fn(p_rw), row(rw_mu[l]), row(rw_w0[l]), wup_b[l], row(rw_a0[l]), aup_b[l],
            gup_b[l], row(rw_k_k[l]), row(rw_k_a[l]), row(rw_r_k[l]), seg2, seq_len=t)
        if t > 1:
            y, s_fin = wkv_chunked(*[a.reshape(RW_PAIRS, nb, t, LANES) for a in ops], s0)
            y, s_fin = y.reshape(RW_PAIRS, nb * t, LANES), _unpair_state(s_fin)
        else:
            y, s_fin = wkv_step(*ops, s0, l)
            y = jnp.swapaxes(y.T.reshape(nb, RW_PAIRS, LANES), 0, 1)
        y_sw, win_k, win_v = swa_fn(q, k, v)
        shift = x.reshape(nb, t, d)[:, -1]
        x = mix_out_ln(x, y, bonus, g, y_sw, row(rw_gn_g[l]),
                       row(rw_gn_b[l]), seg2, row(ln_g[l, 1]), row(ln_b[l, 1]), w_out_b, l)
        x = mem_fn(x)
        x = ffn_ln(x, w1b, w3b, w2b, (l, 1), row(ln_g[l, 3]), row(ln_b[l, 3]))
        return x, s_fin, shift, win_k, win_v

    xp = x_prompt.reshape(bp * tp, d)
    p_wkv, p_shift, p_wk, p_wv, p_mk, p_mv = [], [], [], [], [], []
    for l in range(depth):
        prev_prompt = lambda p_rw: p_rw

        def swa_p(q, k, v, l=l):
            k3 = k.reshape(bp, tp, KV_WIDTH)
            v3 = v.reshape(bp, tp, KV_WIDTH)
            y = swa_prompt(q.reshape(bp, tp, SW_WIDTH), k3, v3, sw_sinks[l])
            tail = lambda a: a[:, -WINDOW:].reshape(bp, WINDOW, SW_KV_HEADS, HEAD_DIM)
            return y.reshape(bp * tp, SW_WIDTH), tail(k3), tail(v3)

        mk, mv = matmul_multi(mem_prompt.reshape(bp * MEM_TOKENS, d), [wkb, wvb], l)
        mk = mk.reshape(bp, MEM_TOKENS, d)
        mv = mv.reshape(bp, MEM_TOKENS, d)

        def mem_p(x, l=l, mk=mk, mv=mv):
            return mem_block(x.reshape(bp, tp, d), wqb, mk, mv, wob, l, row(ln_g[l, 2]),
                             row(ln_b[l, 2])).reshape(bp * tp, d)

        s0 = jnp.zeros((bp, RW_PAIRS, HEAD_DIM, LANES), F32)
        xp, s_fin, shift, wk_, wv_ = layer(l, xp, bp, tp, prev_prompt, s0, swa_p, mem_p)
        p_wkv.append(s_fin)
        p_shift.append(shift)
        p_wk.append(wk_)
        p_wv.append(wv_)
        p_mk.append(mk.reshape(bp, MEM_TOKENS, MEM_HEADS, MEM_HEAD_DIM))
        p_mv.append(mv.reshape(bp, MEM_TOKENS, MEM_HEADS, MEM_HEAD_DIM))

    xs = x_sample.reshape(bs, d)
    state_seq_minor = jnp.transpose(state_wkv, (0, 2, 3, 4, 1))
    s_wkv, s_shift, s_wk, s_wv = [], [], [], []
    for l in range(depth):
        def prev_sample(p_rw, l=l):
            (prev,) = matmul_multi(state_shift[l], [w_in_b], l, widths=[RW_COLS])
            return prev

        def swa_s(q, k, v, l=l):
            o, nk, nv = swa_sample(
                q.reshape(bs, SW_HEADS, HEAD_DIM), k.reshape(bs, 1, KV_WIDTH), v.reshape(bs, 1, KV_WIDTH),
                cache_win_k[l].reshape(bs, WINDOW, KV_WIDTH), cache_win_v[l].reshape(bs, WINDOW, KV_WIDTH),
                sw_sinks[l].reshape(SW_HEADS, 1), slopes)
            unflat = lambda a: a.reshape(bs, WINDOW, SW_KV_HEADS, HEAD_DIM)
            return o.reshape(bs, SW_WIDTH), unflat(nk), unflat(nv)

        def mem_s(x, l=l):
            (qm,) = matmul_multi(x, [wqb], l)
            o = mem_attn_token(qm.reshape(bs, 1, d), cache_mem_k, cache_mem_v, l).reshape(bs, d)
            return proj_ln(x, [o], [wob], l, row(ln_g[l, 2]), row(ln_b[l, 2]))

        xs, s_fin, shift, wk_, wv_ = layer(l, xs, bs, 1, prev_sample, state_seq_minor, swa_s, mem_s)
        s_wkv.append(s_fin)
        s_shift.append(shift)
        s_wk.append(wk_)
        s_wv.append(wv_)

    return (xp.reshape(bp, tp, d), xs.reshape(bs, 1, d),
            jnp.stack(p_wkv), jnp.stack(p_shift), jnp.stack(p_wk), jnp.stack(p_wv),
            jnp.stack(p_mk), jnp.stack(p_mv),
            jnp.transpose(jnp.stack(s_wkv), (0, 4, 1, 2, 3)), jnp.stack(s_shift), jnp.stack(s_wk),
            jnp.stack(s_wv))
```

```python
import functools

import jax
import jax.numpy as jnp
from jax import lax
from jax.experimental import pallas as pl
from jax.experimental.pallas import tpu as pltpu

F32 = jnp.float32
BF16 = jnp.bfloat16

D_MODEL = 1024
HEAD_DIM = 64
RW_WIDTH = 512
RW_HEADS = 8
RW_PAIRS = RW_HEADS // 2
SW_WIDTH = 512
SW_HEADS = 8
SW_KV_HEADS = 2
SW_GROUP = SW_HEADS // SW_KV_HEADS
KV_WIDTH = SW_KV_HEADS * HEAD_DIM
WINDOW = 128
D_W_LORA = 64
D_A_LORA = 64
D_G_LORA = 128
RW_COLS = 3 * RW_WIDTH + D_W_LORA + D_A_LORA + D_G_LORA
LORA_WA_START = 3 * RW_WIDTH
LORA_G_START = LORA_WA_START + D_W_LORA + D_A_LORA
MEM_TOKENS = 256
MEM_HEADS = 4
MEM_HEAD_DIM = D_MODEL // MEM_HEADS
D_FF = 2816
DEPTH = 4
ALPHA = (2.0 * DEPTH) ** 0.25
LN_EPS = 1e-5
GN_EPS = 64e-5
NORM_EPS = 1e-12

CHUNK = 64
FF_TILE = 256
LANES = 128
VMEM_LIMIT = 56 * 1024 * 1024


def _params(*semantics):
    return pltpu.CompilerParams(dimension_semantics=semantics, vmem_limit_bytes=VMEM_LIMIT)


def _row_tile(n, want):
    return want if n % want == 0 else n


def _layer_norm(z, g, b):
    mu = jnp.mean(z, axis=-1, keepdims=True)
    d = z - mu
    var = jnp.mean(d * d, axis=-1, keepdims=True)
    return d * lax.rsqrt(var + LN_EPS) * g + b


def _sigmoid(x):
    return 1.0 / (1.0 + jnp.exp(-x))


def _split3(x):
    hi = x.astype(BF16)
    r1 = x - hi.astype(F32)
    mid = r1.astype(BF16)
    lo = (r1 - mid.astype(F32)).astype(BF16)
    return jnp.concatenate([hi, mid, lo], axis=1)


def _split2(x):
    hi = x.astype(BF16)
    lo = (x - hi.astype(F32)).astype(BF16)
    return jnp.concatenate([hi, lo], axis=1)


def _mm_kernel(x_ref, *refs, n_w, splits):
    xb = x_ref[...].astype(BF16)
    prods = [jnp.dot(xb, w_ref[...], preferred_element_type=F32) for w_ref in refs[:n_w]]
    if splits is not None:
        edges = [sum(splits[:i]) for i in range(len(splits) + 1)]
        prods = [prods[0][:, lo:hi] for lo, hi in zip(edges[:-1], edges[1:])]
    for p, o_ref in zip(prods, refs[n_w:]):
        o_ref[...] = p


def _layer_spec(w, layer, rows=None, cols=None):
    shape = (rows or w.shape[1], cols or w.shape[2])
    return pl.BlockSpec((None,) + shape, lambda *_: (layer, 0, 0))


def matmul_multi(x, ws, layer, widths=None, splits=None, tm=512):
    n, k = x.shape
    tm = _row_tile(n, tm)
    widths = widths or [w.shape[2] for w in ws]
    outs = splits or widths
    return pl.pallas_call(
        functools.partial(_mm_kernel, n_w=len(ws), splits=splits),
        grid=(n // tm,),
        in_specs=[pl.BlockSpec((tm, k), lambda i: (i, 0))]
        + [_layer_spec(w, layer, cols=c) for w, c in zip(ws, widths)],
        out_specs=[pl.BlockSpec((tm, c), lambda i: (i, 0)) for c in outs],
        out_shape=[jax.ShapeDtypeStruct((n, c), F32) for c in outs],
        compiler_params=_params("parallel"),
        name="matmul_multi",
    )(x, *ws)


def _ffn_kernel(x_ref, w1_ref, w3_ref, w2_ref, g_ref, b_ref, o_ref, xb_ref, acc_ref, *, n_ff):
    xb_ref[...] = x_ref[...].astype(BF16)

    def part(c):
        cols = slice(c * FF_TILE, (c + 1) * FF_TILE)
        xb = xb_ref[...]
        h1 = jnp.dot(xb, w1_ref[:, cols], preferred_element_type=F32)
        h3 = jnp.dot(xb, w3_ref[:, cols], preferred_element_type=F32)
        h = (h1 * _sigmoid(h1)) * h3
        return jnp.dot(h.astype(BF16), w2_ref[cols, :], preferred_element_type=F32)

    acc_ref[...] = part(0)
    for c in range(1, n_ff):
        acc_ref[...] += part(c)
    z = ALPHA * x_ref[...] + 0.5 * acc_ref[...]
    o_ref[...] = _layer_norm(z, g_ref[...], b_ref[...])


def ffn_ln(x, w1, w3, w2, lead, g, b, tm=1024):
    n, d = x.shape
    tm = _row_tile(n, tm)
    n_ff = D_FF // FF_TILE
    resident = lambda a: pl.BlockSpec((None,) * len(lead) + a.shape[len(lead):],
                                      lambda i: lead + (0, 0), pipeline_mode=pl.Buffered(1))
    return pl.pallas_call(
        functools.partial(_ffn_kernel, n_ff=n_ff),
        grid=(n // tm,),
        in_specs=[
            pl.BlockSpec((tm, d), lambda i: (i, 0)),
            resident(w1), resident(w3), resident(w2),
            pl.BlockSpec((1, d), lambda i: (0, 0)),
            pl.BlockSpec((1, d), lambda i: (0, 0)),
        ],
        out_specs=pl.BlockSpec((tm, d), lambda i: (i, 0)),
        out_shape=jax.ShapeDtypeStruct((n, d), F32),
        scratch_shapes=[pltpu.VMEM((tm, d), BF16), pltpu.VMEM((tm, d), F32)],
        compiler_params=_params("parallel"),
        name="ffn_ln",
    )(x, w1, w3, w2, g, b)


def _proj_ln_kernel(x_ref, *refs, n_in):
    a_refs = refs[:n_in]
    w_refs = refs[n_in:2 * n_in]
    g_ref, b_ref, o_ref = refs[2 * n_in:]
    f = None
    for a_ref, w_ref in zip(a_refs, w_refs):
        t = jnp.dot(a_ref[...].astype(BF16), w_ref[...], preferred_element_type=F32)
        f = t if f is None else f + t
    o_ref[...] = _layer_norm(ALPHA * x_ref[...] + f, g_ref[...], b_ref[...])


def proj_ln(x, acts, ws, layer, g, b, tm=512):
    n, d = x.shape
    tm = _row_tile(n, tm)
    n_in = len(acts)
    return pl.pallas_call(
        functools.partial(_proj_ln_kernel, n_in=n_in),
        grid=(n // tm,),
        in_specs=[pl.BlockSpec((tm, d), lambda i: (i, 0))]
        + [pl.BlockSpec((tm, a.shape[1]), lambda i: (i, 0)) for a in acts]
        + [_layer_spec(w, layer) for w in ws]
        + [pl.BlockSpec((1, d), lambda i: (0, 0))] * 2,
        out_specs=pl.BlockSpec((tm, d), lambda i: (i, 0)),
        out_shape=jax.ShapeDtypeStruct((n, d), F32),
        compiler_params=_params("parallel"),
        name="proj_ln",
    )(x, *acts, *ws, g, b)


def _rwkv_prep_kernel(p_ref, prev_ref, mu_ref, w0_ref, wup_ref, a0_ref, aup_ref, gup_ref,
                      kk_ref, ka_ref, rk_ref, seg_ref, *refs, chunked, tiles_per_seq):
    if chunked:
        tril_ref, ones_ref = refs[:2]
        refs = refs[2:]
    g_out, bonus_out = refs[-2:]
    p = p_ref[...]
    if chunked:
        first = (pl.program_id(0) % tiles_per_seq) == 0
        above = jnp.where(first, 0.0, prev_ref[7:8, :])
        top = lax.broadcasted_iota(jnp.int32, p.shape, 0) == 0
        prev = jnp.where(top, above, pltpu.roll(p, 1, 0))
    else:
        prev = prev_ref[...]
    xm = p + (prev - p) * mu_ref[...]
    r = xm[:, 0:RW_WIDTH]
    k = xm[:, RW_WIDTH:2 * RW_WIDTH]
    v = xm[:, 2 * RW_WIDTH:3 * RW_WIDTH]
    wa = xm[:, LORA_WA_START:LORA_G_START]
    gl = xm[:, LORA_G_START:RW_COLS]

    def seg_sum(x):
        return jnp.dot(_split2(x), seg_ref[...], preferred_element_type=F32)

    lw = jnp.dot(jnp.tanh(wa).astype(BF16), wup_ref[...], preferred_element_type=F32)
    la = jnp.dot(wa.astype(BF16), aup_ref[...], preferred_element_type=F32)
    z = -(w0_ref[...] + lw)
    softplus = jnp.maximum(z, 0.0) + jnp.log(1.0 + jnp.exp(-jnp.abs(z)))
    w_log = -softplus - 0.5
    log_decay = -jnp.exp(w_log)
    a = _sigmoid(a0_ref[...] + la)
    g = jnp.dot(_sigmoid(gl).astype(BF16), gup_ref[...], preferred_element_type=F32)
    kk = k * kk_ref[...]
    nrm = jnp.sqrt(seg_sum(kk * kk))
    kk = kk / jnp.maximum(nrm, NORM_EPS)
    k_mod = k * (1.0 + (a - 1.0) * ka_ref[...])
    kka = kk * a
    if chunked:
        parts = _split3(log_decay)

        def time_sum(m_ref):
            s3 = jnp.dot(m_ref[...], parts, preferred_element_type=F32)
            return s3[:, :RW_WIDTH] + s3[:, RW_WIDTH:2 * RW_WIDTH] + s3[:, 2 * RW_WIDTH:]

        cum = time_sum(tril_ref)
        tot = time_sum(ones_ref)
        grow = jnp.exp(-cum)
        rest = jnp.exp(tot - cum)
        outs = (kk * jnp.exp(cum - log_decay), r * jnp.exp(cum), -kka * grow, k_mod * grow,
                -kka * rest, k_mod * rest, v, jnp.exp(tot))
    else:
        outs = (r, jnp.exp(log_decay), k_mod, v, kk, kka)
    for o_ref, val in zip(refs, outs):
        if chunked:
            for pr in range(RW_PAIRS):
                o_ref[pr] = val[:, pr * LANES:(pr + 1) * LANES].astype(o_ref.dtype)
        else:
            o_ref[...] = val.T
    g_out[...] = g
    bonus_out[...] = seg_sum(r * k_mod * rk_ref[...]) * v


def rwkv_prep(p, prev, mu, w0, wup, a0, aup, gup, k_k, k_a, r_k, seg, seq_len, tm=256):
    n = p.shape[0]
    chunked = seq_len > 1
    tm = _row_tile(n, tm)
    row = lambda c: pl.BlockSpec((tm, c), lambda i: (i, 0))
    prev_spec = row(RW_COLS)
    if chunked:
        assert seq_len % tm == 0
        prev_spec = pl.BlockSpec((8, RW_COLS), lambda i: (jnp.maximum(i * (tm // 8) - 1, 0), 0))
    full = lambda a: pl.BlockSpec(a.shape, lambda i: (0, 0))
    consts = (mu, w0, wup, a0, aup, gup, k_k, k_a, r_k, seg)
    dtypes = [F32] * 6
    if chunked:
        assert tm % CHUNK == 0
        t_idx = jnp.arange(tm)
        same = (t_idx[:, None] // CHUNK) == (t_idx[None, :] // CHUNK)
        consts += ((same & (t_idx[None, :] <= t_idx[:, None])).astype(BF16), same.astype(BF16))
        dtypes = [BF16] * 7 + [F32]
    if chunked:
        op_spec = pl.BlockSpec((RW_PAIRS, tm, LANES), lambda i: (0, i, 0))
        op_shape = (RW_PAIRS, n, LANES)
    else:
        op_spec = pl.BlockSpec((RW_WIDTH, tm), lambda i: (0, i))
        op_shape = (RW_WIDTH, n)
    return pl.pallas_call(
        functools.partial(_rwkv_prep_kernel, chunked=chunked, tiles_per_seq=max(seq_len // tm, 1)),
        grid=(n // tm,),
        in_specs=[row(RW_COLS), prev_spec] + [full(c) for c in consts],
        out_specs=[op_spec] * len(dtypes) + [row(RW_WIDTH)] * 2,
        out_shape=[jax.ShapeDtypeStruct(op_shape, dt) for dt in dtypes]
        + [jax.ShapeDtypeStruct((n, RW_WIDTH), F32)] * 2,
        compiler_params=_params("parallel"),
        name="rwkv_prep",
    )(p, prev, *consts)


def _wkv_step_kernel(r_ref, w_ref, k_ref, v_ref, kk_ref, kka_ref, s0_ref, y_ref, s_ref):
    r, w, k, kk, kka = (ref[...][None] for ref in (r_ref, w_ref, k_ref, kk_ref, kka_ref))
    rows = 8
    for c in range(HEAD_DIM // rows):
        vals = slice(c * rows, (c + 1) * rows)
        s = s0_ref[0, vals]
        sa = jnp.sum(s * kk, axis=1, keepdims=True)
        s = s * w - sa * kka + v_ref[vals, :][:, None, :] * k
        s_ref[0, vals] = s
        y_ref[vals, :] = jnp.sum(s * r, axis=1)


def wkv_step(r, w, k, v, kk, kka, s0, layer):
    _, nb = r.shape
    vec = pl.BlockSpec((HEAD_DIM, nb), lambda h: (h, 0))
    return pl.pallas_call(
        _wkv_step_kernel,
        grid=(RW_HEADS,),
        in_specs=[vec] * 6 + [pl.BlockSpec((None, 1, HEAD_DIM, HEAD_DIM, nb), lambda h: (layer, h, 0, 0, 0))],
        out_specs=[vec, pl.BlockSpec((1, HEAD_DIM, HEAD_DIM, nb), lambda h: (h, 0, 0, 0))],
        out_shape=[jax.ShapeDtypeStruct((RW_WIDTH, nb), F32),
                   jax.ShapeDtypeStruct((RW_HEADS, HEAD_DIM, HEAD_DIM, nb), F32)],
        compiler_params=_params("parallel"),
        name="wkv_step",
    )(r, w, k, v, kk, kka, s0)


def _dot_nt(a, b):
    return lax.dot_general(a, b, (((1,), (1,)), ((), ())), preferred_element_type=F32)


def _dot_tn(a, b):
    return lax.dot_general(a, b, (((0,), (0,)), ((), ())), preferred_element_type=F32)


def _dot(a, b):
    return jnp.dot(a, b, preferred_element_type=F32)


def _wkv_chunk_kernel(qt_ref, rt_ref, bh_ref, kh_ref, bb_ref, kb_ref, v_ref, gc_ref, s0_ref,
                      y_ref, s_ref, *, nb, tc):
    @pl.when(pl.program_id(1) == 0)
    def _():
        s_ref[...] = s0_ref[...]

    c = CHUNK
    assert c == HEAD_DIM and 2 * c == LANES
    row = lax.broadcasted_iota(jnp.int32, (c, LANES), 0)
    lane = lax.broadcasted_iota(jnp.int32, (c, LANES), 1)
    head_a = lane < HEAD_DIM
    head_a2 = lax.broadcasted_iota(jnp.int32, (2 * c, LANES), 1) < HEAD_DIM
    strict = (lane % c) < row
    eye = jnp.where((lane % c) == row, 1.0, 0.0)
    row4 = lax.broadcasted_iota(jnp.int32, (c, 2 * LANES), 0)
    col4 = lax.broadcasted_iota(jnp.int32, (c, 2 * LANES), 1)
    kh_cols = (col4 >= c) & (col4 < 3 * c)
    m_mask = kh_cols & ((col4 % c) < row4)
    incl = (col4 % c) <= row4
    pairs = [(b, p) for b in range(nb) for p in range(RW_PAIRS)]
    bf = lambda x: x.astype(BF16)
    stack = lambda *xs: jnp.concatenate(xs, axis=0)
    zero = jnp.zeros((), BF16)
    only_a = lambda x: jnp.where(head_a if x.shape[0] == c else head_a2, x, zero)
    only_b = lambda x: jnp.where(head_a if x.shape[0] == c else head_a2, zero, x)
    per_head = lambda x: stack(only_a(x), only_b(x))
    zeros_c = jnp.zeros((c, LANES), BF16)

    def inverse_stages(ci, ready):
        ld = lambda ref, b, p: ref[p, b, pl.ds(ci * c, c), :]
        qr = [stack(ld(qt_ref, b, p), ld(rt_ref, b, p)) for b, p in pairs]
        bh = [ld(bh_ref, b, p) for b, p in pairs]
        kh = [ld(kh_ref, b, p) for b, p in pairs]
        e = [_dot_nt(q, stack(only_a(stack(x, y)), only_b(stack(y, x)))) for q, x, y in zip(qr, bh, kh)]
        yield
        pw = [jnp.where(strict, jnp.where(head_a, x[:c, :LANES], x[:c, LANES:]), 0.0) for x in e]
        inv = [eye + n for n in pw]
        for _ in range(5):
            pw = [_dot(x, per_head(x)) for x in [bf(x) for x in pw]]
            yield
            inv = [t + _dot(bf(x), per_head(bf(t))) for x, t in zip(pw, inv)]
            yield
        ready[ci] = (qr, e, inv)

    def state_stages(ci, ready):
        qr, e, inv = ready.pop(ci)
        ld = lambda ref, b, p: ref[p, b, pl.ds(ci * c, c), :]
        s = [s_ref[b, p] for b, p in pairs]
        ff = [_dot_nt(q, per_head(bf(x))) for q, x in zip(qr, s)]
        v = [ld(v_ref, b, p) for b, p in pairs]
        yield
        g = [f[:c] + _dot(bf(jnp.where(m_mask, x[:c], 0.0)), stack(zeros_c, only_a(w), only_b(w), zeros_c))
             for f, x, w in zip(ff, e, v)]
        yield
        u = [bf(_dot(bf(t), per_head(bf(x)))) for t, x in zip(inv, g)]
        yield
        y = [f[c:] + _dot(bf(jnp.where(incl, x[c:], 0.0)), stack(only_a(uu), only_a(w), only_b(w), only_b(uu)))
             for f, x, uu, w in zip(ff, e, u, v)]
        add = [_dot_tn(stack(uu, w), stack(ld(bb_ref, b, p), ld(kb_ref, b, p)))
               for uu, w, (b, p) in zip(u, v, pairs)]
        yield
        for i, (b, p) in enumerate(pairs):
            y_ref[p, b, pl.ds(ci * c, c), :] = y[i]
            gc = gc_ref[p, b, pl.ds(ci * c, 8), :]
            kept = (s[i].reshape(c // 8, 8, LANES) * gc).reshape(c, LANES)
            s_ref[b, p] = kept + jnp.where(head_a, add[i][:c], add[i][c:])
        yield

    n_chunks = tc // c
    state_len, inv_len = 5, 11
    ready, inverses, state = {}, {}, None
    for tick in range(-inv_len, state_len * n_chunks):
        for ci in range(n_chunks):
            if ci not in inverses and state_len * ci - inv_len <= tick:
                inverses[ci] = inverse_stages(ci, ready)
        for gen in inverses.values():
            next(gen, None)
        if tick >= 0:
            if tick % state_len == 0:
                state = state_stages(tick // state_len, ready)
            next(state)


def wkv_chunked(qt, rt, bh, kh, bb, kb, v, gc, s0, tc=512):
    _, nb, t, _ = qt.shape
    seq = pl.BlockSpec((RW_PAIRS, nb, tc, LANES), lambda i, j: (0, 0, j, 0))
    st = pl.BlockSpec((nb, RW_PAIRS, HEAD_DIM, LANES), lambda i, j: (0, 0, 0, 0))
    return pl.pallas_call(
        functools.partial(_wkv_chunk_kernel, nb=nb, tc=tc),
        grid=(1, t // tc),
        in_specs=[seq] * 8 + [st],
        out_specs=[seq, st],
        out_shape=[jax.ShapeDtypeStruct((RW_PAIRS, nb, t, LANES), F32),
                   jax.ShapeDtypeStruct(s0.shape, F32)],
        compiler_params=_params("arbitrary", "arbitrary"),
        name="wkv_chunked",
    )(qt, rt, bh, kh, bb, kb, v, gc, s0)


def _mix_out_kernel(x_ref, y_ref, bonus_ref, g_ref, ysw_ref, gng_ref, gnb_ref, seg_ref, lng_ref, lnb_ref,
                    wout_ref, o_ref):
    def seg_mean(x):
        return jnp.dot(_split2(x), seg_ref[...], preferred_element_type=F32) * (1.0 / HEAD_DIM)

    y = jnp.concatenate([y_ref[pr] for pr in range(RW_PAIRS)], axis=1)
    d = y - seg_mean(y)
    var = seg_mean(d * d)
    yn = d * lax.rsqrt(var + GN_EPS) * gng_ref[...] + gnb_ref[...]
    y_rw = (yn + bonus_ref[...]) * g_ref[...]
    f = _dot(y_rw.astype(BF16), wout_ref[:RW_WIDTH, :]) + _dot(ysw_ref[...].astype(BF16), wout_ref[RW_WIDTH:, :])
    o_ref[...] = _layer_norm(ALPHA * x_ref[...] + f, lng_ref[...], lnb_ref[...])


def mix_out_ln(x, y, bonus, g, y_sw, gn_g, gn_b, seg, ln_g, ln_b, w_out, layer, tm=512):
    n, d = x.shape
    tm = _row_tile(n, tm)
    row = lambda c: pl.BlockSpec((tm, c), lambda i: (i, 0))
    full = lambda a: pl.BlockSpec(a.shape, lambda i: (0, 0))
    consts = (gn_g, gn_b, seg, ln_g, ln_b)
    return pl.pallas_call(
        _mix_out_kernel,
        grid=(n // tm,),
        in_specs=[row(d), pl.BlockSpec((RW_PAIRS, tm, LANES), lambda i: (0, i, 0)), row(RW_WIDTH),
                  row(RW_WIDTH), row(SW_WIDTH)] + [full(c) for c in consts] + [_layer_spec(w_out, layer)],
        out_specs=row(d),
        out_shape=jax.ShapeDtypeStruct((n, d), F32),
        compiler_params=_params("parallel"),
        name="mix_out_ln",
    )(x, y, bonus, g, y_sw, *consts, w_out)


def _alibi_slope(h):
    return 2.0 ** (-8.0 * (h + 1) / SW_HEADS)


def _swa_prompt_kernel(sink_ref, bias0_ref, bias1_ref, q_ref, kp_ref, kc_ref, vp_ref, vc_ref, o_ref, *, qb_per_step):
    qb = (q_ref[0] * HEAD_DIM ** -0.5).astype(BF16)
    kw = jnp.concatenate([kp_ref[0], kc_ref[0]], axis=0).astype(BF16)
    vw = jnp.concatenate([vp_ref[0], vc_ref[0]], axis=0).astype(BF16)
    kx = pltpu.roll(kw, HEAD_DIM, 1)
    vx = pltpu.roll(vw, HEAD_DIM, 1)
    low_kv = lax.broadcasted_iota(jnp.int32, kw.shape, 1) < HEAD_DIM
    low_q = lax.broadcasted_iota(jnp.int32, (WINDOW, LANES), 1) < HEAD_DIM
    zero = jnp.zeros((), BF16)
    kc = [jnp.where(low_kv, kw, kx), jnp.where(low_kv, kx, kw)]
    vc = [jnp.where(low_kv, vw, vx), jnp.where(low_kv, vx, vw)]
    items = [(j, h) for j in range(qb_per_step) for h in range(SW_HEADS)]
    rows = lambda j: slice(j * WINDOW, (j + 1) * WINDOW)
    keys = lambda j: slice(j * WINDOW, (j + 2) * WINDOW)
    bias = lambda j, h: bias0_ref[h] if j == 0 else bias1_ref[h]
    qh = []
    for j, h in items:
        q2 = qb[rows(j), (h // 2) * LANES:(h // 2 + 1) * LANES]
        qh.append(jnp.where(low_q, q2, zero) if h % 2 == 0 else jnp.where(low_q, zero, q2))
    logits = [_dot_nt(x, kc[h // SW_GROUP][keys(j)]) + bias(j, h) for x, (j, h) in zip(qh, items)]
    m = [jnp.maximum(jnp.max(x, axis=-1, keepdims=True), sink_ref[h]) for x, (j, h) in zip(logits, items)]
    e = [jnp.exp(x - mx) for x, mx in zip(logits, m)]
    den = [jnp.sum(x, axis=-1, keepdims=True) + jnp.exp(sink_ref[h] - mx) for x, mx, (j, h) in zip(e, m, items)]
    o = [_dot(x.astype(BF16), vc[h // SW_GROUP][keys(j)]) / dn for x, dn, (j, h) in zip(e, den, items)]
    for i in range(0, len(items), 2):
        j, h = items[i]
        o_ref[0, rows(j), (h // 2) * LANES:(h // 2 + 1) * LANES] = jnp.where(low_q, o[i], o[i + 1])


def swa_prompt(q, k, v, sinks, qb_per_step=8):
    nb, t, _ = q.shape
    rows = qb_per_step * WINDOW
    assert t % rows == 0
    i = jnp.arange(WINDOW)[:, None]
    j = jnp.arange(2 * WINDOW)[None, :]
    dist = WINDOW + i - j
    valid = (dist >= 0) & (dist < WINDOW)
    slopes = jnp.asarray([_alibi_slope(h) for h in range(SW_HEADS)], F32)[:, None, None]
    table = lambda ok: jnp.where(ok[None], -slopes * dist.astype(F32)[None], -jnp.inf)
    bias = jnp.stack([table(valid & (j >= WINDOW)), table(valid)])
    table_spec = lambda pick: pl.BlockSpec((None, SW_HEADS, WINDOW, 2 * WINDOW), lambda b, n: (pick(n), 0, 0, 0))
    cur = lambda w: pl.BlockSpec((1, rows, w), lambda b, n: (b, n, 0))
    prv = lambda w: pl.BlockSpec((1, WINDOW, w), lambda b, n: (b, jnp.maximum(n * qb_per_step - 1, 0), 0))
    return pl.pallas_call(
        functools.partial(_swa_prompt_kernel, qb_per_step=qb_per_step),
        grid=(nb, t // rows),
        in_specs=[pl.BlockSpec(memory_space=pltpu.SMEM),
                  table_spec(lambda n: jnp.minimum(n, 1)), table_spec(lambda n: 1),
                  cur(SW_WIDTH), prv(KV_WIDTH), cur(KV_WIDTH), prv(KV_WIDTH), cur(KV_WIDTH)],
        out_specs=cur(SW_WIDTH),
        out_shape=jax.ShapeDtypeStruct((nb, t, SW_WIDTH), F32),
        compiler_params=_params("parallel", "parallel"),
        name="swa_prompt",
    )(sinks, bias, bias, q, k, k, v, v)


def _swa_sample_kernel(sink_ref, slope_ref, q_ref, kn_ref, vn_ref, ck_ref, cv_ref,
                       o_ref, nk_ref, nv_ref, *, bb):
    last = lax.broadcasted_iota(jnp.int32, (KV_WIDTH, WINDOW), 1) == WINDOW - 1
    j = lax.broadcasted_iota(jnp.int32, (SW_HEADS, WINDOW), 1)
    bias = slope_ref[...] * (WINDOW - 1 - j).astype(F32)
    sink = sink_ref[...]
    seqs = range(bb)
    flat = lambda ref, b: ref[b].reshape(KV_WIDTH, ref.shape[-1])
    kw = [jnp.where(last, flat(kn_ref, b), pltpu.roll(flat(ck_ref, b), WINDOW - 1, 1)) for b in seqs]
    vw = [jnp.where(last, flat(vn_ref, b), pltpu.roll(flat(cv_ref, b), WINDOW - 1, 1)) for b in seqs]
    s = [_dot(q_ref[b].astype(BF16), kw[b].astype(BF16)) * HEAD_DIM ** -0.5 - bias for b in seqs]
    m = [jnp.maximum(jnp.max(x, axis=-1, keepdims=True), sink) for x in s]
    e = [jnp.exp(x - mx) for x, mx in zip(s, m)]
    prob = [x / (jnp.sum(x, axis=-1, keepdims=True) + jnp.exp(sink - mx)) for x, mx in zip(e, m)]
    o = [_dot_nt(prob[b].astype(BF16), vw[b].astype(BF16)) for b in seqs]
    for b in seqs:
        nk_ref[b] = kw[b].reshape(SW_KV_HEADS, HEAD_DIM, WINDOW)
        nv_ref[b] = vw[b].reshape(SW_KV_HEADS, HEAD_DIM, WINDOW)
        o_ref[b] = o[b]


def swa_sample(q, kn, vn, ck, cv, layer, sinks, slopes, bb=8):
    nb = q.shape[0]
    assert nb % bb == 0
    own = (jnp.arange(SW_HEADS) // SW_GROUP)[:, None] == jnp.arange(SW_KV_HEADS)[None, :]
    q2 = jnp.where(own[None, :, :, None], q[:, :, None, :], 0.0).reshape(nb, SW_HEADS, KV_WIDTH)
    full = lambda a: pl.BlockSpec(a.shape, lambda b: (0, 0))
    heads = pl.BlockSpec((bb, SW_HEADS, KV_WIDTH), lambda b: (b, 0, 0))
    new = pl.BlockSpec((bb, SW_KV_HEADS, HEAD_DIM, 1), lambda b: (b, 0, 0, 0))
    cache = pl.BlockSpec((None, bb, SW_KV_HEADS, HEAD_DIM, WINDOW), lambda b: (layer, b, 0, 0, 0))
    win = pl.BlockSpec((bb, SW_KV_HEADS, HEAD_DIM, WINDOW), lambda b: (b, 0, 0, 0))
    win_shape = jax.ShapeDtypeStruct((nb, SW_KV_HEADS, HEAD_DIM, WINDOW), F32)
    o2, nk, nv = pl.pallas_call(
        functools.partial(_swa_sample_kernel, bb=bb),
        grid=(nb // bb,),
        in_specs=[full(sinks), full(slopes), heads, new, new, cache, cache],
        out_specs=[heads, win, win],
        out_shape=[jax.ShapeDtypeStruct((nb, SW_HEADS, KV_WIDTH), F32), win_shape, win_shape],
        compiler_params=_params("parallel"),
        name="swa_sample",
    )(sinks, slopes, q2, kn, vn, ck, cv)
    o = jnp.sum(jnp.where(own[None, :, :, None], o2.reshape(nb, SW_HEADS, SW_KV_HEADS, HEAD_DIM), 0.0), axis=2)
    return o, nk, nv


def _mem_block_kernel(x_ref, wq_ref, mk_ref, mv_ref, wo_ref, g_ref, b_ref, o_ref):
    x = x_ref[0]
    qb = _dot(x.astype(BF16), wq_ref[...]).astype(BF16)
    cols = [slice(h * MEM_HEAD_DIM, (h + 1) * MEM_HEAD_DIM) for h in range(MEM_HEADS)]
    s = [_dot_nt(qb[:, c], mk_ref[0, :, c].astype(BF16)) * MEM_HEAD_DIM ** -0.5 for c in cols]
    e = [jnp.exp(x - jnp.max(x, axis=-1, keepdims=True)) for x in s]
    den = [jnp.sum(x, axis=-1, keepdims=True) for x in e]
    outs = [_dot(x.astype(BF16), mv_ref[0, :, c].astype(BF16)) / d for x, c, d in zip(e, cols, den)]
    o = jnp.concatenate(outs, axis=1).astype(BF16)
    o_ref[0] = _layer_norm(ALPHA * x + _dot(o, wo_ref[...]), g_ref[...], b_ref[...])


def mem_block(x, wq, mk, mv, wo, layer, g, b, tm=512):
    ng, t, d = x.shape
    tm = _row_tile(t, tm)
    row = pl.BlockSpec((1, tm, d), lambda gi, i: (gi, i, 0))
    mem = pl.BlockSpec((1, MEM_TOKENS, d), lambda gi, i: (gi, 0, 0))
    full = lambda a: pl.BlockSpec(a.shape, lambda gi, i: (0, 0))
    return pl.pallas_call(
        _mem_block_kernel,
        grid=(ng, t // tm),
        in_specs=[row, _layer_spec(wq, layer), mem, mem, _layer_spec(wo, layer), full(g), full(b)],
        out_specs=row,
        out_shape=jax.ShapeDtypeStruct((ng, t, d), F32),
        compiler_params=_params("parallel", "parallel"),
        name="mem_block",
    )(x, wq, mk, mv, wo, g, b)


def _mem_attn_token_kernel(q_ref, mk_ref, mv_ref, o_ref, *, bb):
    rows = MEM_TOKENS * MEM_HEADS
    lane = lax.broadcasted_iota(jnp.int32, (MEM_HEADS, rows), 1)
    head = lax.broadcasted_iota(jnp.int32, (MEM_HEADS, rows), 0)
    own = (lane % MEM_HEADS) == head
    for b in range(bb):
        q = q_ref[b]
        q4 = jnp.concatenate([q[:, h * MEM_HEAD_DIM:(h + 1) * MEM_HEAD_DIM] for h in range(MEM_HEADS)],
                             axis=0).astype(BF16)
        k2 = mk_ref[b].reshape(rows, MEM_HEAD_DIM).astype(BF16)
        v2 = mv_ref[b].reshape(rows, MEM_HEAD_DIM).astype(BF16)
        s = jnp.where(own, _dot_nt(q4, k2) * MEM_HEAD_DIM ** -0.5, -jnp.inf)
        m = jnp.max(s, axis=-1, keepdims=True)
        e = jnp.exp(s - m)
        prob = e / jnp.sum(e, axis=-1, keepdims=True)
        o = _dot(prob.astype(BF16), v2)
        for h in range(MEM_HEADS):
            o_ref[b, :, h * MEM_HEAD_DIM:(h + 1) * MEM_HEAD_DIM] = o[h:h + 1]


def mem_attn_token(q, mk, mv, layer, bb=4):
    ng, _, d = q.shape
    assert ng % bb == 0
    row = pl.BlockSpec((bb, 1, d), lambda g: (g, 0, 0))
    mem = pl.BlockSpec((None, bb, MEM_TOKENS, MEM_HEADS, MEM_HEAD_DIM), lambda g: (layer, g, 0, 0, 0))
    return pl.pallas_call(
        functools.partial(_mem_attn_token_kernel, bb=bb),
        grid=(ng // bb,),
        in_specs=[row, mem, mem],
        out_specs=row,
        out_shape=jax.ShapeDtypeStruct((ng, 1, d), F32),
        compiler_params=_params("parallel"),
        name="mem_attn_token",
    )(q, mk, mv)


def _unpair_state(s):
    nb = s.shape[0]
    s = s.reshape(nb, RW_PAIRS, HEAD_DIM, 2, HEAD_DIM)
    return jnp.swapaxes(s, 2, 3).reshape(nb, RW_HEADS, HEAD_DIM, HEAD_DIM)


def kernel(x_prompt, x_sample, mem_prompt, state_wkv, state_shift, cache_win_k, cache_win_v,
           cache_mem_k, cache_mem_v, ln_g, ln_b, ffn_w1, ffn_w3, ffn_w2, w_in, rw_mu, rw_w0,
           rw_w_up, rw_a0, rw_a_up, rw_g_up, rw_k_k, rw_k_a, rw_r_k, rw_gn_g, rw_gn_b, sw_sinks,
           w_out, mem_wq, mem_wk, mem_wv, mem_wo):
    depth = ln_g.shape[0]
    bp, tp, d = x_prompt.shape
    bs, ts, _ = x_sample.shape
    assert ts == 1 and cache_win_k.shape[2] == WINDOW and tp % WINDOW == 0

    w1b, w3b, w2b = (w.astype(BF16) for w in (ffn_w1, ffn_w3, ffn_w2))
    w_in_b = w_in.astype(BF16)
    w_out_b = w_out.astype(BF16)
    wqb, wkb, wvb, wob = (w.astype(BF16) for w in (mem_wq, mem_wk, mem_wv, mem_wo))
    zpad = jnp.zeros((depth, D_W_LORA, RW_WIDTH), BF16)
    wup_b = jnp.concatenate([rw_w_up.astype(BF16), zpad], axis=1)
    aup_b = jnp.concatenate([zpad, rw_a_up.astype(BF16)], axis=1)
    gup_b = rw_g_up.astype(BF16)
    hid = jnp.arange(2 * RW_WIDTH) % RW_WIDTH // HEAD_DIM
    seg2 = (hid[:, None] == (jnp.arange(RW_WIDTH) // HEAD_DIM)[None, :]).astype(BF16)
    slopes = jnp.asarray([[_alibi_slope(h)] for h in range(SW_HEADS)], F32)
    row = lambda a: a.reshape(1, -1)

    def layer(l, x, nb, t, prev_fn, s0, swa_fn, mem_fn):
        x = ffn_ln(x, w1b, w3b, w2b, (l, 0), row(ln_g[l, 0]), row(ln_b[l, 0]))
        p_rw, q, k, v = matmul_multi(x, [w_in_b], l, splits=[RW_COLS, SW_WIDTH, KV_WIDTH, KV_WIDTH])
        *ops, g, bonus = rwkv_prep(
            p_rw, prev_fn(p_rw), row(rw_mu[l]), row(rw_w0[l]), wup_b[l], row(rw_a0[l]), aup_b[l],
            gup_b[l], row(rw_k_k[l]), row(rw_k_a[l]), row(rw_r_k[l]), seg2, seq_len=t)
        if t > 1:
            y, s_fin = wkv_chunked(*[a.reshape(RW_PAIRS, nb, t, LANES) for a in ops], s0)
            y, s_fin = y.reshape(RW_PAIRS, nb * t, LANES), _unpair_state(s_fin)
        else:
            y, s_fin = wkv_step(*ops, s0, l)
            y = jnp.swapaxes(y.T.reshape(nb, RW_PAIRS, LANES), 0, 1)
        y_sw, win_k, win_v = swa_fn(q, k, v)
        shift = x.reshape(nb, t, d)[:, -1]
        x = mix_out_ln(x, y, bonus, g, y_sw, row(rw_gn_g[l]),
                       row(rw_gn_b[l]), seg2, row(ln_g[l, 1]), row(ln_b[l, 1]), w_out_b, l)
        x = mem_fn(x)
        x = ffn_ln(x, w1b, w3b, w2b, (l, 1), row(ln_g[l, 3]), row(ln_b[l, 3]))
        return x, s_fin, shift, win_k, win_v

    xp = x_prompt.reshape(bp * tp, d)
    p_wkv, p_shift, p_wk, p_wv, p_mk, p_mv = [], [], [], [], [], []
    for l in range(depth):
        prev_prompt = lambda p_rw: p_rw

        def swa_p(q, k, v, l=l):
            k3 = k.reshape(bp, tp, KV_WIDTH)
            v3 = v.reshape(bp, tp, KV_WIDTH)
            y = swa_prompt(q.reshape(bp, tp, SW_WIDTH), k3, v3, sw_sinks[l])
            tail = lambda a: a[:, -WINDOW:].reshape(bp, WINDOW, SW_KV_HEADS, HEAD_DIM)
            return y.reshape(bp * tp, SW_WIDTH), tail(k3), tail(v3)

        mk, mv = matmul_multi(mem_prompt.reshape(bp * MEM_TOKENS, d), [wkb, wvb], l)
        mk = mk.reshape(bp, MEM_TOKENS, d)
        mv = mv.reshape(bp, MEM_TOKENS, d)

        def mem_p(x, l=l, mk=mk, mv=mv):
            return mem_block(x.reshape(bp, tp, d), wqb, mk, mv, wob, l, row(ln_g[l, 2]),
                             row(ln_b[l, 2])).reshape(bp * tp, d)

        s0 = jnp.zeros((bp, RW_PAIRS, HEAD_DIM, LANES), F32)
        xp, s_fin, shift, wk_, wv_ = layer(l, xp, bp, tp, prev_prompt, s0, swa_p, mem_p)
        p_wkv.append(s_fin)
        p_shift.append(shift)
        p_wk.append(wk_)
        p_wv.append(wv_)
        p_mk.append(mk.reshape(bp, MEM_TOKENS, MEM_HEADS, MEM_HEAD_DIM))
        p_mv.append(mv.reshape(bp, MEM_TOKENS, MEM_HEADS, MEM_HEAD_DIM))

    xs = x_sample.reshape(bs, d)
    state_seq_minor = jnp.transpose(state_wkv, (0, 2, 3, 4, 1))
    win_k_pos_minor = jnp.transpose(cache_win_k, (0, 1, 3, 4, 2))
    win_v_pos_minor = jnp.transpose(cache_win_v, (0, 1, 3, 4, 2))
    s_wkv, s_shift, s_wk, s_wv = [], [], [], []
    for l in range(depth):
        def prev_sample(p_rw, l=l):
            (prev,) = matmul_multi(state_shift[l], [w_in_b], l, widths=[RW_COLS])
            return prev

        def swa_s(q, k, v, l=l):
            col = lambda a: a.reshape(bs, SW_KV_HEADS, HEAD_DIM, 1)
            o, nk, nv = swa_sample(q.reshape(bs, SW_HEADS, HEAD_DIM), col(k), col(v), win_k_pos_minor,
                                   win_v_pos_minor, l, sw_sinks[l].reshape(SW_HEADS, 1), slopes)
            return o.reshape(bs, SW_WIDTH), nk, nv

        def mem_s(x, l=l):
            (qm,) = matmul_multi(x, [wqb], l)
            o = mem_attn_token(qm.reshape(bs, 1, d), cache_mem_k, cache_mem_v, l).reshape(bs, d)
            return proj_ln(x, [o], [wob], l, row(ln_g[l, 2]), row(ln_b[l, 2]))

        xs, s_fin, shift, wk_, wv_ = layer(l, xs, bs, 1, prev_sample, state_seq_minor, swa_s, mem_s)
        s_wkv.append(s_fin)
        s_shift.append(shift)
        s_wk.append(wk_)
        s_wv.append(wv_)

    return (xp.reshape(bp, tp, d), xs.reshape(bs, 1, d),
            jnp.stack(p_wkv), jnp.stack(p_shift), jnp.stack(p_wk), jnp.stack(p_wv),
            jnp.stack(p_mk), jnp.stack(p_mv),
            jnp.transpose(jnp.stack(s_wkv), (0, 4, 1, 2, 3)), jnp.stack(s_shift),
            jnp.transpose(jnp.stack(s_wk), (0, 1, 4, 2, 3)), jnp.transpose(jnp.stack(s_wv), (0, 1, 4, 2, 3)))
```

```python
import functools

import jax
import jax.numpy as jnp
from jax import lax
from jax.experimental import pallas as pl
from jax.experimental.pallas import tpu as pltpu

F32 = jnp.float32
BF16 = jnp.bfloat16

D_MODEL = 1024
HEAD_DIM = 64
RW_WIDTH = 512
RW_HEADS = 8
RW_PAIRS = RW_HEADS // 2
SW_WIDTH = 512
SW_HEADS = 8
SW_KV_HEADS = 2
SW_GROUP = SW_HEADS // SW_KV_HEADS
KV_WIDTH = SW_KV_HEADS * HEAD_DIM
WINDOW = 128
D_W_LORA = 64
D_A_LORA = 64
D_G_LORA = 128
RW_COLS = 3 * RW_WIDTH + D_W_LORA + D_A_LORA + D_G_LORA
LORA_WA_START = 3 * RW_WIDTH
LORA_G_START = LORA_WA_START + D_W_LORA + D_A_LORA
MEM_TOKENS = 256
MEM_HEADS = 4
MEM_HEAD_DIM = D_MODEL // MEM_HEADS
D_FF = 2816
DEPTH = 4
ALPHA = (2.0 * DEPTH) ** 0.25
LN_EPS = 1e-5
GN_EPS = 64e-5
NORM_EPS = 1e-12

CHUNK = 64
FF_TILE = 256
LANES = 128
VMEM_LIMIT = 56 * 1024 * 1024


def _params(*semantics):
    return pltpu.CompilerParams(dimension_semantics=semantics, vmem_limit_bytes=VMEM_LIMIT)


def _row_tile(n, want):
    return want if n % want == 0 else n


def _layer_norm(z, g, b):
    mu = jnp.mean(z, axis=-1, keepdims=True)
    d = z - mu
    var = jnp.mean(d * d, axis=-1, keepdims=True)
    return d * lax.rsqrt(var + LN_EPS) * g + b


def _sigmoid(x):
    return 1.0 / (1.0 + jnp.exp(-x))


def _split3(x):
    hi = x.astype(BF16)
    r1 = x - hi.astype(F32)
    mid = r1.astype(BF16)
    lo = (r1 - mid.astype(F32)).astype(BF16)
    return jnp.concatenate([hi, mid, lo], axis=1)


def _split2(x):
    hi = x.astype(BF16)
    lo = (x - hi.astype(F32)).astype(BF16)
    return jnp.concatenate([hi, lo], axis=1)


def _mm_kernel(x_ref, *refs, n_w, splits):
    xb = x_ref[...].astype(BF16)
    prods = [jnp.dot(xb, w_ref[...], preferred_element_type=F32) for w_ref in refs[:n_w]]
    if splits is not None:
        edges = [sum(splits[:i]) for i in range(len(splits) + 1)]
        prods = [prods[0][:, lo:hi] for lo, hi in zip(edges[:-1], edges[1:])]
    for p, o_ref in zip(prods, refs[n_w:]):
        o_ref[...] = p


def _layer_spec(w, layer, rows=None, cols=None):
    shape = (rows or w.shape[1], cols or w.shape[2])
    return pl.BlockSpec((None,) + shape, lambda *_: (layer, 0, 0))


def matmul_multi(x, ws, layer, widths=None, splits=None, tm=512):
    n, k = x.shape
    tm = _row_tile(n, tm)
    widths = widths or [w.shape[2] for w in ws]
    outs = splits or widths
    return pl.pallas_call(
        functools.partial(_mm_kernel, n_w=len(ws), splits=splits),
        grid=(n // tm,),
        in_specs=[pl.BlockSpec((tm, k), lambda i: (i, 0))]
        + [_layer_spec(w, layer, cols=c) for w, c in zip(ws, widths)],
        out_specs=[pl.BlockSpec((tm, c), lambda i: (i, 0)) for c in outs],
        out_shape=[jax.ShapeDtypeStruct((n, c), F32) for c in outs],
        compiler_params=_params("parallel"),
        name="matmul_multi",
    )(x, *ws)


def _ffn_kernel(x_ref, w1_ref, w3_ref, w2_ref, g_ref, b_ref, o_ref, xb_ref, acc_ref, *, n_ff):
    xb_ref[...] = x_ref[...].astype(BF16)

    def part(c):
        cols = slice(c * FF_TILE, (c + 1) * FF_TILE)
        xb = xb_ref[...]
        h1 = jnp.dot(xb, w1_ref[:, cols], preferred_element_type=F32)
        h3 = jnp.dot(xb, w3_ref[:, cols], preferred_element_type=F32)
        h = (h1 * _sigmoid(h1)) * h3
        return jnp.dot(h.astype(BF16), w2_ref[cols, :], preferred_element_type=F32)

    acc_ref[...] = part(0)
    for c in range(1, n_ff):
        acc_ref[...] += part(c)
    z = ALPHA * x_ref[...] + 0.5 * acc_ref[...]
    o_ref[...] = _layer_norm(z, g_ref[...], b_ref[...])


def ffn_ln(x, w1, w3, w2, lead, g, b, tm=1024):
    n, d = x.shape
    tm = _row_tile(n, tm)
    n_ff = D_FF // FF_TILE
    resident = lambda a: pl.BlockSpec((None,) * len(lead) + a.shape[len(lead):],
                                      lambda i: lead + (0, 0), pipeline_mode=pl.Buffered(1))
    return pl.pallas_call(
        functools.partial(_ffn_kernel, n_ff=n_ff),
        grid=(n // tm,),
        in_specs=[
            pl.BlockSpec((tm, d), lambda i: (i, 0)),
            resident(w1), resident(w3), resident(w2),
            pl.BlockSpec((1, d), lambda i: (0, 0)),
            pl.BlockSpec((1, d), lambda i: (0, 0)),
        ],
        out_specs=pl.BlockSpec((tm, d), lambda i: (i, 0)),
        out_shape=jax.ShapeDtypeStruct((n, d), F32),
        scratch_shapes=[pltpu.VMEM((tm, d), BF16), pltpu.VMEM((tm, d), F32)],
        compiler_params=_params("parallel"),
        name="ffn_ln",
    )(x, w1, w3, w2, g, b)


def _proj_ln_kernel(x_ref, *refs, n_in):
    a_refs = refs[:n_in]
    w_refs = refs[n_in:2 * n_in]
    g_ref, b_ref, o_ref = refs[2 * n_in:]
    f = None
    for a_ref, w_ref in zip(a_refs, w_refs):
        t = jnp.dot(a_ref[...].astype(BF16), w_ref[...], preferred_element_type=F32)
        f = t if f is None else f + t
    o_ref[...] = _layer_norm(ALPHA * x_ref[...] + f, g_ref[...], b_ref[...])


def proj_ln(x, acts, ws, layer, g, b, tm=512):
    n, d = x.shape
    tm = _row_tile(n, tm)
    n_in = len(acts)
    return pl.pallas_call(
        functools.partial(_proj_ln_kernel, n_in=n_in),
        grid=(n // tm,),
        in_specs=[pl.BlockSpec((tm, d), lambda i: (i, 0))]
        + [pl.BlockSpec((tm, a.shape[1]), lambda i: (i, 0)) for a in acts]
        + [_layer_spec(w, layer) for w in ws]
        + [pl.BlockSpec((1, d), lambda i: (0, 0))] * 2,
        out_specs=pl.BlockSpec((tm, d), lambda i: (i, 0)),
        out_shape=jax.ShapeDtypeStruct((n, d), F32),
        compiler_params=_params("parallel"),
        name="proj_ln",
    )(x, *acts, *ws, g, b)


def _rwkv_prep_kernel(p_ref, prev_ref, mu_ref, w0_ref, wup_ref, a0_ref, aup_ref, gup_ref,
                      kk_ref, ka_ref, rk_ref, seg_ref, *refs, chunked, tiles_per_seq):
    if chunked:
        tril_ref, ones_ref = refs[:2]
        refs = refs[2:]
    g_out, bonus_out = refs[-2:]
    p = p_ref[...]
    if chunked:
        first = (pl.program_id(0) % tiles_per_seq) == 0
        above = jnp.where(first, 0.0, prev_ref[7:8, :])
        top = lax.broadcasted_iota(jnp.int32, p.shape, 0) == 0
        prev = jnp.where(top, above, pltpu.roll(p, 1, 0))
    else:
        prev = prev_ref[...]
    xm = p + (prev - p) * mu_ref[...]
    r = xm[:, 0:RW_WIDTH]
    k = xm[:, RW_WIDTH:2 * RW_WIDTH]
    v = xm[:, 2 * RW_WIDTH:3 * RW_WIDTH]
    wa = xm[:, LORA_WA_START:LORA_G_START]
    gl = xm[:, LORA_G_START:RW_COLS]

    def seg_sum(x):
        return jnp.dot(_split2(x), seg_ref[...], preferred_element_type=F32)

    lw = jnp.dot(jnp.tanh(wa).astype(BF16), wup_ref[...], preferred_element_type=F32)
    la = jnp.dot(wa.astype(BF16), aup_ref[...], preferred_element_type=F32)
    z = -(w0_ref[...] + lw)
    softplus = jnp.maximum(z, 0.0) + jnp.log(1.0 + jnp.exp(-jnp.abs(z)))
    w_log = -softplus - 0.5
    log_decay = -jnp.exp(w_log)
    a = _sigmoid(a0_ref[...] + la)
    g = jnp.dot(_sigmoid(gl).astype(BF16), gup_ref[...], preferred_element_type=F32)
    kk = k * kk_ref[...]
    nrm = jnp.sqrt(seg_sum(kk * kk))
    kk = kk / jnp.maximum(nrm, NORM_EPS)
    k_mod = k * (1.0 + (a - 1.0) * ka_ref[...])
    kka = kk * a
    if chunked:
        parts = _split3(log_decay)

        def time_sum(m_ref):
            s3 = jnp.dot(m_ref[...], parts, preferred_element_type=F32)
            return s3[:, :RW_WIDTH] + s3[:, RW_WIDTH:2 * RW_WIDTH] + s3[:, 2 * RW_WIDTH:]

        cum = time_sum(tril_ref)
        tot = time_sum(ones_ref)
        grow = jnp.exp(-cum)
        rest = jnp.exp(tot - cum)
        outs = (kk * jnp.exp(cum - log_decay), r * jnp.exp(cum), -kka * grow, k_mod * grow,
                -kka * rest, k_mod * rest, v, jnp.exp(tot))
    else:
        outs = (r, jnp.exp(log_decay), k_mod, v, kk, kka)
    for o_ref, val in zip(refs, outs):
        if chunked:
            for pr in range(RW_PAIRS):
                o_ref[pr] = val[:, pr * LANES:(pr + 1) * LANES].astype(o_ref.dtype)
        else:
            o_ref[...] = val.T
    g_out[...] = g
    bonus_out[...] = seg_sum(r * k_mod * rk_ref[...]) * v


def rwkv_prep(p, prev, mu, w0, wup, a0, aup, gup, k_k, k_a, r_k, seg, seq_len, tm=256):
    n = p.shape[0]
    chunked = seq_len > 1
    tm = _row_tile(n, tm)
    row = lambda c: pl.BlockSpec((tm, c), lambda i: (i, 0))
    prev_spec = row(RW_COLS)
    if chunked:
        assert seq_len % tm == 0
        prev_spec = pl.BlockSpec((8, RW_COLS), lambda i: (jnp.maximum(i * (tm // 8) - 1, 0), 0))
    full = lambda a: pl.BlockSpec(a.shape, lambda i: (0, 0))
    consts = (mu, w0, wup, a0, aup, gup, k_k, k_a, r_k, seg)
    dtypes = [F32] * 6
    if chunked:
        assert tm % CHUNK == 0
        t_idx = jnp.arange(tm)
        same = (t_idx[:, None] // CHUNK) == (t_idx[None, :] // CHUNK)
        consts += ((same & (t_idx[None, :] <= t_idx[:, None])).astype(BF16), same.astype(BF16))
        dtypes = [BF16] * 7 + [F32]
    if chunked:
        op_spec = pl.BlockSpec((RW_PAIRS, tm, LANES), lambda i: (0, i, 0))
        op_shape = (RW_PAIRS, n, LANES)
    else:
        op_spec = pl.BlockSpec((RW_WIDTH, tm), lambda i: (0, i))
        op_shape = (RW_WIDTH, n)
    return pl.pallas_call(
        functools.partial(_rwkv_prep_kernel, chunked=chunked, tiles_per_seq=max(seq_len // tm, 1)),
        grid=(n // tm,),
        in_specs=[row(RW_COLS), prev_spec] + [full(c) for c in consts],
        out_specs=[op_spec] * len(dtypes) + [row(RW_WIDTH)] * 2,
        out_shape=[jax.ShapeDtypeStruct(op_shape, dt) for dt in dtypes]
        + [jax.ShapeDtypeStruct((n, RW_WIDTH), F32)] * 2,
        compiler_params=_params("parallel"),
        name="rwkv_prep",
    )(p, prev, *consts)


def _wkv_step_kernel(r_ref, w_ref, k_ref, v_ref, kk_ref, kka_ref, s0_ref, y_ref, s_ref):
    r, w, k, kk, kka = (ref[...][None] for ref in (r_ref, w_ref, k_ref, kk_ref, kka_ref))
    rows = 8
    for c in range(HEAD_DIM // rows):
        vals = slice(c * rows, (c + 1) * rows)
        s = s0_ref[0, vals]
        sa = jnp.sum(s * kk, axis=1, keepdims=True)
        s = s * w - sa * kka + v_ref[vals, :][:, None, :] * k
        s_ref[0, vals] = s
        y_ref[vals, :] = jnp.sum(s * r, axis=1)


def wkv_step(r, w, k, v, kk, kka, s0, layer):
    _, nb = r.shape
    vec = pl.BlockSpec((HEAD_DIM, nb), lambda h: (h, 0))
    return pl.pallas_call(
        _wkv_step_kernel,
        grid=(RW_HEADS,),
        in_specs=[vec] * 6 + [pl.BlockSpec((None, 1, HEAD_DIM, HEAD_DIM, nb), lambda h: (layer, h, 0, 0, 0))],
        out_specs=[vec, pl.BlockSpec((1, HEAD_DIM, HEAD_DIM, nb), lambda h: (h, 0, 0, 0))],
        out_shape=[jax.ShapeDtypeStruct((RW_WIDTH, nb), F32),
                   jax.ShapeDtypeStruct((RW_HEADS, HEAD_DIM, HEAD_DIM, nb), F32)],
        compiler_params=_params("parallel"),
        name="wkv_step",
    )(r, w, k, v, kk, kka, s0)


def _dot_nt(a, b):
    return lax.dot_general(a, b, (((1,), (1,)), ((), ())), preferred_element_type=F32)


def _dot_tn(a, b):
    return lax.dot_general(a, b, (((0,), (0,)), ((), ())), preferred_element_type=F32)


def _dot(a, b):
    return jnp.dot(a, b, preferred_element_type=F32)


def _wkv_chunk_kernel(qt_ref, rt_ref, bh_ref, kh_ref, bb_ref, kb_ref, v_ref, gc_ref, s0_ref,
                      y_ref, s_ref, *, nb, tc):
    @pl.when(pl.program_id(1) == 0)
    def _():
        s_ref[...] = s0_ref[...]

    c = CHUNK
    assert c == HEAD_DIM and 2 * c == LANES
    row = lax.broadcasted_iota(jnp.int32, (c, LANES), 0)
    lane = lax.broadcasted_iota(jnp.int32, (c, LANES), 1)
    head_a = lane < HEAD_DIM
    head_a2 = lax.broadcasted_iota(jnp.int32, (2 * c, LANES), 1) < HEAD_DIM
    strict = (lane % c) < row
    eye = jnp.where((lane % c) == row, 1.0, 0.0)
    row4 = lax.broadcasted_iota(jnp.int32, (c, 2 * LANES), 0)
    col4 = lax.broadcasted_iota(jnp.int32, (c, 2 * LANES), 1)
    kh_cols = (col4 >= c) & (col4 < 3 * c)
    m_mask = kh_cols & ((col4 % c) < row4)
    incl = (col4 % c) <= row4
    pairs = [(b, p) for b in range(nb) for p in range(RW_PAIRS)]
    bf = lambda x: x.astype(BF16)
    stack = lambda *xs: jnp.concatenate(xs, axis=0)
    zero = jnp.zeros((), BF16)
    only_a = lambda x: jnp.where(head_a if x.shape[0] == c else head_a2, x, zero)
    only_b = lambda x: jnp.where(head_a if x.shape[0] == c else head_a2, zero, x)
    per_head = lambda x: stack(only_a(x), only_b(x))
    zeros_c = jnp.zeros((c, LANES), BF16)

    def inverse_stages(ci, ready):
        ld = lambda ref, b, p: ref[p, b, pl.ds(ci * c, c), :]
        qr = [stack(ld(qt_ref, b, p), ld(rt_ref, b, p)) for b, p in pairs]
        bh = [ld(bh_ref, b, p) for b, p in pairs]
        kh = [ld(kh_ref, b, p) for b, p in pairs]
        e = [_dot_nt(q, stack(only_a(stack(x, y)), only_b(stack(y, x)))) for q, x, y in zip(qr, bh, kh)]
        yield
        pw = [jnp.where(strict, jnp.where(head_a, x[:c, :LANES], x[:c, LANES:]), 0.0) for x in e]
        inv = [eye + n for n in pw]
        for _ in range(5):
            pw = [_dot(x, per_head(x)) for x in [bf(x) for x in pw]]
            yield
            inv = [t + _dot(bf(x), per_head(bf(t))) for x, t in zip(pw, inv)]
            yield
        ready[ci] = (qr, e, inv)

    def state_stages(ci, ready):
        qr, e, inv = ready.pop(ci)
        ld = lambda ref, b, p: ref[p, b, pl.ds(ci * c, c), :]
        s = [s_ref[b, p] for b, p in pairs]
        ff = [_dot_nt(q, per_head(bf(x))) for q, x in zip(qr, s)]
        v = [ld(v_ref, b, p) for b, p in pairs]
        yield
        g = [f[:c] + _dot(bf(jnp.where(m_mask, x[:c], 0.0)), stack(zeros_c, only_a(w), only_b(w), zeros_c))
             for f, x, w in zip(ff, e, v)]
        yield
        u = [bf(_dot(bf(t), per_head(bf(x)))) for t, x in zip(inv, g)]
        yield
        y = [f[c:] + _dot(bf(jnp.where(incl, x[c:], 0.0)), stack(only_a(uu), only_a(w), only_b(w), only_b(uu)))
             for f, x, uu, w in zip(ff, e, u, v)]
        add = [_dot_tn(stack(uu, w), stack(ld(bb_ref, b, p), ld(kb_ref, b, p)))
               for uu, w, (b, p) in zip(u, v, pairs)]
        yield
        for i, (b, p) in enumerate(pairs):
            y_ref[p, b, pl.ds(ci * c, c), :] = y[i]
            gc = gc_ref[p, b, pl.ds(ci * c, 8), :]
            kept = (s[i].reshape(c // 8, 8, LANES) * gc).reshape(c, LANES)
            s_ref[b, p] = kept + jnp.where(head_a, add[i][:c], add[i][c:])
        yield

    n_chunks = tc // c
    state_len, inv_len = 5, 11
    ready, inverses, state = {}, {}, None
    for tick in range(-inv_len, state_len * n_chunks):
        for ci in range(n_chunks):
            if ci not in inverses and state_len * ci - inv_len <= tick:
                inverses[ci] = inverse_stages(ci, ready)
        for gen in inverses.values():
            next(gen, None)
        if tick >= 0:
            if tick % state_len == 0:
                state = state_stages(tick // state_len, ready)
            next(state)


def wkv_chunked(qt, rt, bh, kh, bb, kb, v, gc, s0, tc=512):
    _, nb, t, _ = qt.shape
    seq = pl.BlockSpec((RW_PAIRS, nb, tc, LANES), lambda i, j: (0, 0, j, 0))
    st = pl.BlockSpec((nb, RW_PAIRS, HEAD_DIM, LANES), lambda i, j: (0, 0, 0, 0))
    return pl.pallas_call(
        functools.partial(_wkv_chunk_kernel, nb=nb, tc=tc),
        grid=(1, t // tc),
        in_specs=[seq] * 8 + [st],
        out_specs=[seq, st],
        out_shape=[jax.ShapeDtypeStruct((RW_PAIRS, nb, t, LANES), F32),
                   jax.ShapeDtypeStruct(s0.shape, F32)],
        compiler_params=_params("arbitrary", "arbitrary"),
        name="wkv_chunked",
    )(qt, rt, bh, kh, bb, kb, v, gc, s0)


def _mix_out_kernel(x_ref, y_ref, bonus_ref, g_ref, ysw_ref, gng_ref, gnb_ref, seg_ref, lng_ref, lnb_ref,
                    wout_ref, o_ref):
    def seg_mean(x):
        sums = [_dot(_split2(x[:, pr * LANES:(pr + 1) * LANES]), seg_ref[...]) for pr in range(RW_PAIRS)]
        return jnp.concatenate(sums, axis=1) * (1.0 / HEAD_DIM)

    y = jnp.concatenate([y_ref[pr] for pr in range(RW_PAIRS)], axis=1)
    d = y - seg_mean(y)
    var = seg_mean(d * d)
    yn = d * lax.rsqrt(var + GN_EPS) * gng_ref[...] + gnb_ref[...]
    y_rw = (yn + bonus_ref[...]) * g_ref[...]
    f = _dot(y_rw.astype(BF16), wout_ref[:RW_WIDTH, :]) + _dot(ysw_ref[...].astype(BF16), wout_ref[RW_WIDTH:, :])
    o_ref[...] = _layer_norm(ALPHA * x_ref[...] + f, lng_ref[...], lnb_ref[...])


def mix_out_ln(x, y, bonus, g, y_sw, gn_g, gn_b, seg, ln_g, ln_b, w_out, layer, tm=512):
    n, d = x.shape
    tm = _row_tile(n, tm)
    row = lambda c: pl.BlockSpec((tm, c), lambda i: (i, 0))
    full = lambda a: pl.BlockSpec(a.shape, lambda i: (0, 0))
    consts = (gn_g, gn_b, seg, ln_g, ln_b)
    return pl.pallas_call(
        _mix_out_kernel,
        grid=(n // tm,),
        in_specs=[row(d), pl.BlockSpec((RW_PAIRS, tm, LANES), lambda i: (0, i, 0)), row(RW_WIDTH),
                  row(RW_WIDTH), row(SW_WIDTH)] + [full(c) for c in consts] + [_layer_spec(w_out, layer)],
        out_specs=row(d),
        out_shape=jax.ShapeDtypeStruct((n, d), F32),
        compiler_params=_params("parallel"),
        name="mix_out_ln",
    )(x, y, bonus, g, y_sw, *consts, w_out)


def _alibi_slope(h):
    return 2.0 ** (-8.0 * (h + 1) / SW_HEADS)


def _swa_prompt_kernel(sink_ref, bias0_ref, bias1_ref, q_ref, kp_ref, kc_ref, vp_ref, vc_ref, o_ref, *, qb_per_step):
    qb = (q_ref[0] * HEAD_DIM ** -0.5).astype(BF16)
    kw = jnp.concatenate([kp_ref[0], kc_ref[0]], axis=0).astype(BF16)
    vw = jnp.concatenate([vp_ref[0], vc_ref[0]], axis=0).astype(BF16)
    kx = pltpu.roll(kw, HEAD_DIM, 1)
    vx = pltpu.roll(vw, HEAD_DIM, 1)
    low_kv = lax.broadcasted_iota(jnp.int32, kw.shape, 1) < HEAD_DIM
    low_q = lax.broadcasted_iota(jnp.int32, (WINDOW, LANES), 1) < HEAD_DIM
    zero = jnp.zeros((), BF16)
    kc = [jnp.where(low_kv, kw, kx), jnp.where(low_kv, kx, kw)]
    vc = [jnp.where(low_kv, vw, vx), jnp.where(low_kv, vx, vw)]
    items = [(j, h) for j in range(qb_per_step) for h in range(SW_HEADS)]
    rows = lambda j: slice(j * WINDOW, (j + 1) * WINDOW)
    keys = lambda j: slice(j * WINDOW, (j + 2) * WINDOW)
    bias = lambda j, h: bias0_ref[h] if j == 0 else bias1_ref[h]
    qh = []
    for j, h in items:
        q2 = qb[rows(j), (h // 2) * LANES:(h // 2 + 1) * LANES]
        qh.append(jnp.where(low_q, q2, zero) if h % 2 == 0 else jnp.where(low_q, zero, q2))
    logits = [_dot_nt(x, kc[h // SW_GROUP][keys(j)]) + bias(j, h) for x, (j, h) in zip(qh, items)]
    m = [jnp.maximum(jnp.max(x, axis=-1, keepdims=True), sink_ref[h]) for x, (j, h) in zip(logits, items)]
    e = [jnp.exp(x - mx) for x, mx in zip(logits, m)]
    den = [jnp.sum(x, axis=-1, keepdims=True) + jnp.exp(sink_ref[h] - mx) for x, mx, (j, h) in zip(e, m, items)]
    o = [_dot(x.astype(BF16), vc[h // SW_GROUP][keys(j)]) / dn for x, dn, (j, h) in zip(e, den, items)]
    for i in range(0, len(items), 2):
        j, h = items[i]
        o_ref[0, rows(j), (h // 2) * LANES:(h // 2 + 1) * LANES] = jnp.where(low_q, o[i], o[i + 1])


def swa_prompt(q, k, v, sinks, qb_per_step=8):
    nb, t, _ = q.shape
    rows = qb_per_step * WINDOW
    assert t % rows == 0
    i = jnp.arange(WINDOW)[:, None]
    j = jnp.arange(2 * WINDOW)[None, :]
    dist = WINDOW + i - j
    valid = (dist >= 0) & (dist < WINDOW)
    slopes = jnp.asarray([_alibi_slope(h) for h in range(SW_HEADS)], F32)[:, None, None]
    table = lambda ok: jnp.where(ok[None], -slopes * dist.astype(F32)[None], -jnp.inf)
    bias = jnp.stack([table(valid & (j >= WINDOW)), table(valid)])
    table_spec = lambda pick: pl.BlockSpec((None, SW_HEADS, WINDOW, 2 * WINDOW), lambda b, n: (pick(n), 0, 0, 0))
    cur = lambda w: pl.BlockSpec((1, rows, w), lambda b, n: (b, n, 0))
    prv = lambda w: pl.BlockSpec((1, WINDOW, w), lambda b, n: (b, jnp.maximum(n * qb_per_step - 1, 0), 0))
    return pl.pallas_call(
        functools.partial(_swa_prompt_kernel, qb_per_step=qb_per_step),
        grid=(nb, t // rows),
        in_specs=[pl.BlockSpec(memory_space=pltpu.SMEM),
                  table_spec(lambda n: jnp.minimum(n, 1)), table_spec(lambda n: 1),
                  cur(SW_WIDTH), prv(KV_WIDTH), cur(KV_WIDTH), prv(KV_WIDTH), cur(KV_WIDTH)],
        out_specs=cur(SW_WIDTH),
        out_shape=jax.ShapeDtypeStruct((nb, t, SW_WIDTH), F32),
        compiler_params=_params("parallel", "parallel"),
        name="swa_prompt",
    )(sinks, bias, bias, q, k, k, v, v)


def _swa_sample_kernel(sink_ref, slope_ref, q_ref, kn_ref, vn_ref, ck_ref, cv_ref,
                       o_ref, nk_ref, nv_ref, *, bb):
    last = lax.broadcasted_iota(jnp.int32, (KV_WIDTH, WINDOW), 1) == WINDOW - 1
    j = lax.broadcasted_iota(jnp.int32, (SW_HEADS, WINDOW), 1)
    bias = slope_ref[...] * (WINDOW - 1 - j).astype(F32)
    sink = sink_ref[...]
    seqs = range(bb)
    flat = lambda ref, b: ref[b].reshape(KV_WIDTH, ref.shape[-1])
    kw = [jnp.where(last, flat(kn_ref, b), pltpu.roll(flat(ck_ref, b), WINDOW - 1, 1)) for b in seqs]
    vw = [jnp.where(last, flat(vn_ref, b), pltpu.roll(flat(cv_ref, b), WINDOW - 1, 1)) for b in seqs]
    s = [_dot(q_ref[b].astype(BF16), kw[b].astype(BF16)) * HEAD_DIM ** -0.5 - bias for b in seqs]
    m = [jnp.maximum(jnp.max(x, axis=-1, keepdims=True), sink) for x in s]
    e = [jnp.exp(x - mx) for x, mx in zip(s, m)]
    prob = [x / (jnp.sum(x, axis=-1, keepdims=True) + jnp.exp(sink - mx)) for x, mx in zip(e, m)]
    o = [_dot_nt(prob[b].astype(BF16), vw[b].astype(BF16)) for b in seqs]
    for b in seqs:
        nk_ref[b] = kw[b].reshape(SW_KV_HEADS, HEAD_DIM, WINDOW)
        nv_ref[b] = vw[b].reshape(SW_KV_HEADS, HEAD_DIM, WINDOW)
        o_ref[b] = o[b]


def swa_sample(q, kn, vn, ck, cv, layer, sinks, slopes, bb=8):
    nb = q.shape[0]
    assert nb % bb == 0
    own = (jnp.arange(SW_HEADS) // SW_GROUP)[:, None] == jnp.arange(SW_KV_HEADS)[None, :]
    q2 = jnp.where(own[None, :, :, None], q[:, :, None, :], 0.0).reshape(nb, SW_HEADS, KV_WIDTH)
    full = lambda a: pl.BlockSpec(a.shape, lambda b: (0, 0))
    heads = pl.BlockSpec((bb, SW_HEADS, KV_WIDTH), lambda b: (b, 0, 0))
    new = pl.BlockSpec((bb, SW_KV_HEADS, HEAD_DIM, 1), lambda b: (b, 0, 0, 0))
    cache = pl.BlockSpec((None, bb, SW_KV_HEADS, HEAD_DIM, WINDOW), lambda b: (layer, b, 0, 0, 0))
    win = pl.BlockSpec((bb, SW_KV_HEADS, HEAD_DIM, WINDOW), lambda b: (b, 0, 0, 0))
    win_shape = jax.ShapeDtypeStruct((nb, SW_KV_HEADS, HEAD_DIM, WINDOW), F32)
    o2, nk, nv = pl.pallas_call(
        functools.partial(_swa_sample_kernel, bb=bb),
        grid=(nb // bb,),
        in_specs=[full(sinks), full(slopes), heads, new, new, cache, cache],
        out_specs=[heads, win, win],
        out_shape=[jax.ShapeDtypeStruct((nb, SW_HEADS, KV_WIDTH), F32), win_shape, win_shape],
        compiler_params=_params("parallel"),
        name="swa_sample",
    )(sinks, slopes, q2, kn, vn, ck, cv)
    o = jnp.sum(jnp.where(own[None, :, :, None], o2.reshape(nb, SW_HEADS, SW_KV_HEADS, HEAD_DIM), 0.0), axis=2)
    return o, nk, nv


def _mem_block_kernel(x_ref, wq_ref, mk_ref, mv_ref, wo_ref, g_ref, b_ref, o_ref):
    x = x_ref[0]
    qb = _dot(x.astype(BF16), wq_ref[...]).astype(BF16)
    cols = [slice(h * MEM_HEAD_DIM, (h + 1) * MEM_HEAD_DIM) for h in range(MEM_HEADS)]
    s = [_dot_nt(qb[:, c], mk_ref[0, :, c].astype(BF16)) * MEM_HEAD_DIM ** -0.5 for c in cols]
    e = [jnp.exp(x - jnp.max(x, axis=-1, keepdims=True)) for x in s]
    den = [jnp.sum(x, axis=-1, keepdims=True) for x in e]
    outs = [_dot(x.astype(BF16), mv_ref[0, :, c].astype(BF16)) / d for x, c, d in zip(e, cols, den)]
    o = jnp.concatenate(outs, axis=1).astype(BF16)
    o_ref[0] = _layer_norm(ALPHA * x + _dot(o, wo_ref[...]), g_ref[...], b_ref[...])


def mem_block(x, wq, mk, mv, wo, layer, g, b, tm=512):
    ng, t, d = x.shape
    tm = _row_tile(t, tm)
    row = pl.BlockSpec((1, tm, d), lambda gi, i: (gi, i, 0))
    mem = pl.BlockSpec((1, MEM_TOKENS, d), lambda gi, i: (gi, 0, 0))
    full = lambda a: pl.BlockSpec(a.shape, lambda gi, i: (0, 0))
    return pl.pallas_call(
        _mem_block_kernel,
        grid=(ng, t // tm),
        in_specs=[row, _layer_spec(wq, layer), mem, mem, _layer_spec(wo, layer), full(g), full(b)],
        out_specs=row,
        out_shape=jax.ShapeDtypeStruct((ng, t, d), F32),
        compiler_params=_params("parallel", "parallel"),
        name="mem_block",
    )(x, wq, mk, mv, wo, g, b)


def _mem_attn_token_kernel(q_ref, mk_ref, mv_ref, o_ref, *, bb):
    rows = MEM_TOKENS * MEM_HEADS
    lane = lax.broadcasted_iota(jnp.int32, (MEM_HEADS, rows), 1)
    head = lax.broadcasted_iota(jnp.int32, (MEM_HEADS, rows), 0)
    own = (lane % MEM_HEADS) == head
    for b in range(bb):
        q = q_ref[b]
        q4 = jnp.concatenate([q[:, h * MEM_HEAD_DIM:(h + 1) * MEM_HEAD_DIM] for h in range(MEM_HEADS)],
                             axis=0).astype(BF16)
        k2 = mk_ref[b].reshape(rows, MEM_HEAD_DIM).astype(BF16)
        v2 = mv_ref[b].reshape(rows, MEM_HEAD_DIM).astype(BF16)
        s = jnp.where(own, _dot_nt(q4, k2) * MEM_HEAD_DIM ** -0.5, -jnp.inf)
        m = jnp.max(s, axis=-1, keepdims=True)
        e = jnp.exp(s - m)
        prob = e / jnp.sum(e, axis=-1, keepdims=True)
        o = _dot(prob.astype(BF16), v2)
        for h in range(MEM_HEADS):
            o_ref[b, :, h * MEM_HEAD_DIM:(h + 1) * MEM_HEAD_DIM] = o[h:h + 1]


def mem_attn_token(q, mk, mv, layer, bb=8):
    ng, _, d = q.shape
    assert ng % bb == 0
    row = pl.BlockSpec((bb, 1, d), lambda g: (g, 0, 0))
    mem = pl.BlockSpec((None, bb, MEM_TOKENS, MEM_HEADS, MEM_HEAD_DIM), lambda g: (layer, g, 0, 0, 0))
    return pl.pallas_call(
        functools.partial(_mem_attn_token_kernel, bb=bb),
        grid=(ng // bb,),
        in_specs=[row, mem, mem],
        out_specs=row,
        out_shape=jax.ShapeDtypeStruct((ng, 1, d), F32),
        compiler_params=_params("parallel"),
        name="mem_attn_token",
    )(q, mk, mv)


def _unpair_state(s):
    nb = s.shape[0]
    s = s.reshape(nb, RW_PAIRS, HEAD_DIM, 2, HEAD_DIM)
    return jnp.swapaxes(s, 2, 3).reshape(nb, RW_HEADS, HEAD_DIM, HEAD_DIM)


def kernel(x_prompt, x_sample, mem_prompt, state_wkv, state_shift, cache_win_k, cache_win_v,
           cache_mem_k, cache_mem_v, ln_g, ln_b, ffn_w1, ffn_w3, ffn_w2, w_in, rw_mu, rw_w0,
           rw_w_up, rw_a0, rw_a_up, rw_g_up, rw_k_k, rw_k_a, rw_r_k, rw_gn_g, rw_gn_b, sw_sinks,
           w_out, mem_wq, mem_wk, mem_wv, mem_wo):
    depth = ln_g.shape[0]
    bp, tp, d = x_prompt.shape
    bs, ts, _ = x_sample.shape
    assert ts == 1 and cache_win_k.shape[2] == WINDOW and tp % WINDOW == 0

    w1b, w3b, w2b = (w.astype(BF16) for w in (ffn_w1, ffn_w3, ffn_w2))
    w_in_b = w_in.astype(BF16)
    w_out_b = w_out.astype(BF16)
    wqb, wkb, wvb, wob = (w.astype(BF16) for w in (mem_wq, mem_wk, mem_wv, mem_wo))
    zpad = jnp.zeros((depth, D_W_LORA, RW_WIDTH), BF16)
    wup_b = jnp.concatenate([rw_w_up.astype(BF16), zpad], axis=1)
    aup_b = jnp.concatenate([zpad, rw_a_up.astype(BF16)], axis=1)
    gup_b = rw_g_up.astype(BF16)
    hid = jnp.arange(2 * RW_WIDTH) % RW_WIDTH // HEAD_DIM
    seg2 = (hid[:, None] == (jnp.arange(RW_WIDTH) // HEAD_DIM)[None, :]).astype(BF16)
    seg_pair = seg2[jnp.arange(2 * LANES) % LANES + (jnp.arange(2 * LANES) // LANES) * RW_WIDTH, :LANES]
    slopes = jnp.asarray([[_alibi_slope(h)] for h in range(SW_HEADS)], F32)
    row = lambda a: a.reshape(1, -1)

    def layer(l, x, nb, t, prev_fn, s0, swa_fn, mem_fn):
        x = ffn_ln(x, w1b, w3b, w2b, (l, 0), row(ln_g[l, 0]), row(ln_b[l, 0]))
        p_rw, q, k, v = matmul_multi(x, [w_in_b], l, splits=[RW_COLS, SW_WIDTH, KV_WIDTH, KV_WIDTH])
        *ops, g, bonus = rwkv_prep(
            p_rw, prev_fn(p_rw), row(rw_mu[l]), row(rw_w0[l]), wup_b[l], row(rw_a0[l]), aup_b[l],
            gup_b[l], row(rw_k_k[l]), row(rw_k_a[l]), row(rw_r_k[l]), seg2, seq_len=t)
        if t > 1:
            y, s_fin = wkv_chunked(*[a.reshape(RW_PAIRS, nb, t, LANES) for a in ops], s0)
            y, s_fin = y.reshape(RW_PAIRS, nb * t, LANES), _unpair_state(s_fin)
        else:
            y, s_fin = wkv_step(*ops, s0, l)
            y = jnp.swapaxes(y.T.reshape(nb, RW_PAIRS, LANES), 0, 1)
        y_sw, win_k, win_v = swa_fn(q, k, v)
        shift = x.reshape(nb, t, d)[:, -1]
        x = mix_out_ln(x, y, bonus, g, y_sw, row(rw_gn_g[l]),
                       row(rw_gn_b[l]), seg_pair, row(ln_g[l, 1]), row(ln_b[l, 1]), w_out_b, l)
        x = mem_fn(x)
        x = ffn_ln(x, w1b, w3b, w2b, (l, 1), row(ln_g[l, 3]), row(ln_b[l, 3]))
        return x, s_fin, shift, win_k, win_v

    xp = x_prompt.reshape(bp * tp, d)
    p_wkv, p_shift, p_wk, p_wv, p_mk, p_mv = [], [], [], [], [], []
    for l in range(depth):
        prev_prompt = lambda p_rw: p_rw

        def swa_p(q, k, v, l=l):
            k3 = k.reshape(bp, tp, KV_WIDTH)
            v3 = v.reshape(bp, tp, KV_WIDTH)
            y = swa_prompt(q.reshape(bp, tp, SW_WIDTH), k3, v3, sw_sinks[l])
            tail = lambda a: a[:, -WINDOW:].reshape(bp, WINDOW, SW_KV_HEADS, HEAD_DIM)
            return y.reshape(bp * tp, SW_WIDTH), tail(k3), tail(v3)

        mk, mv = matmul_multi(mem_prompt.reshape(bp * MEM_TOKENS, d), [wkb, wvb], l)
        mk = mk.reshape(bp, MEM_TOKENS, d)
        mv = mv.reshape(bp, MEM_TOKENS, d)

        def mem_p(x, l=l, mk=mk, mv=mv):
            return mem_block(x.reshape(bp, tp, d), wqb, mk, mv, wob, l, row(ln_g[l, 2]),
                             row(ln_b[l, 2])).reshape(bp * tp, d)

        s0 = jnp.zeros((bp, RW_PAIRS, HEAD_DIM, LANES), F32)
        xp, s_fin, shift, wk_, wv_ = layer(l, xp, bp, tp, prev_prompt, s0, swa_p, mem_p)
        p_wkv.append(s_fin)
        p_shift.append(shift)
        p_wk.append(wk_)
        p_wv.append(wv_)
        p_mk.append(mk.reshape(bp, MEM_TOKENS, MEM_HEADS, MEM_HEAD_DIM))
        p_mv.append(mv.reshape(bp, MEM_TOKENS, MEM_HEADS, MEM_HEAD_DIM))

    xs = x_sample.reshape(bs, d)
    state_seq_minor = jnp.transpose(state_wkv, (0, 2, 3, 4, 1))
    win_k_pos_minor = jnp.transpose(cache_win_k, (0, 1, 3, 4, 2))
    win_v_pos_minor = jnp.transpose(cache_win_v, (0, 1, 3, 4, 2))
    s_wkv, s_shift, s_wk, s_wv = [], [], [], []
    for l in range(depth):
        def prev_sample(p_rw, l=l):
            (prev,) = matmul_multi(state_shift[l], [w_in_b], l, widths=[RW_COLS])
            return prev

        def swa_s(q, k, v, l=l):
            col = lambda a: a.reshape(bs, SW_KV_HEADS, HEAD_DIM, 1)
            o, nk, nv = swa_sample(q.reshape(bs, SW_HEADS, HEAD_DIM), col(k), col(v), win_k_pos_minor,
                                   win_v_pos_minor, l, sw_sinks[l].reshape(SW_HEADS, 1), slopes)
            return o.reshape(bs, SW_WIDTH), nk, nv

        def mem_s(x, l=l):
            (qm,) = matmul_multi(x, [wqb], l)
            o = mem_attn_token(qm.reshape(bs, 1, d), cache_mem_k, cache_mem_v, l).reshape(bs, d)
            return proj_ln(x, [o], [wob], l, row(ln_g[l, 2]), row(ln_b[l, 2]))

        xs, s_fin, shift, wk_, wv_ = layer(l, xs, bs, 1, prev_sample, state_seq_minor, swa_s, mem_s)
        s_wkv.append(s_fin)
        s_shift.append(shift)
        s_wk.append(wk_)
        s_wv.append(wv_)

    return (xp.reshape(bp, tp, d), xs.reshape(bs, 1, d),
            jnp.stack(p_wkv), jnp.stack(p_shift), jnp.stack(p_wk), jnp.stack(p_wv),
            jnp.stack(p_mk), jnp.stack(p_mv),
            jnp.transpose(jnp.stack(s_wkv), (0, 4, 1, 2, 3)), jnp.stack(s_shift),
            jnp.transpose(jnp.stack(s_wk), (0, 1, 4, 2, 3)), jnp.transpose(jnp.stack(s_wv), (0, 1, 4, 2, 3)))
```

```python
import functools

import jax
import jax.numpy as jnp
from jax import lax
from jax.experimental import pallas as pl
from jax.experimental.pallas import tpu as pltpu

F32 = jnp.float32
BF16 = jnp.bfloat16

D_MODEL = 1024
HEAD_DIM = 64
RW_WIDTH = 512
RW_HEADS = 8
RW_PAIRS = RW_HEADS // 2
SW_WIDTH = 512
SW_HEADS = 8
SW_KV_HEADS = 2
SW_GROUP = SW_HEADS // SW_KV_HEADS
KV_WIDTH = SW_KV_HEADS * HEAD_DIM
WINDOW = 128
D_W_LORA = 64
D_A_LORA = 64
D_G_LORA = 128
RW_COLS = 3 * RW_WIDTH + D_W_LORA + D_A_LORA + D_G_LORA
LORA_WA_START = 3 * RW_WIDTH
LORA_G_START = LORA_WA_START + D_W_LORA + D_A_LORA
MEM_TOKENS = 256
MEM_HEADS = 4
MEM_HEAD_DIM = D_MODEL // MEM_HEADS
D_FF = 2816
DEPTH = 4
ALPHA = (2.0 * DEPTH) ** 0.25
LN_EPS = 1e-5
GN_EPS = 64e-5
NORM_EPS = 1e-12

CHUNK = 64
FF_TILE = 256
LANES = 128
VMEM_LIMIT = 56 * 1024 * 1024


def _params(*semantics):
    return pltpu.CompilerParams(dimension_semantics=semantics, vmem_limit_bytes=VMEM_LIMIT)


def _row_tile(n, want):
    return want if n % want == 0 else n


def _layer_norm(z, g, b):
    mu = jnp.mean(z, axis=-1, keepdims=True)
    d = z - mu
    var = jnp.mean(d * d, axis=-1, keepdims=True)
    return d * lax.rsqrt(var + LN_EPS) * g + b


def _sigmoid(x):
    return 1.0 / (1.0 + jnp.exp(-x))


def _split3(x):
    hi = x.astype(BF16)
    r1 = x - hi.astype(F32)
    mid = r1.astype(BF16)
    lo = (r1 - mid.astype(F32)).astype(BF16)
    return jnp.concatenate([hi, mid, lo], axis=1)


def _split2(x):
    hi = x.astype(BF16)
    lo = (x - hi.astype(F32)).astype(BF16)
    return jnp.concatenate([hi, lo], axis=1)


def _mm_kernel(x_ref, *refs, n_w, splits):
    xb = x_ref[...].astype(BF16)
    prods = [jnp.dot(xb, w_ref[...], preferred_element_type=F32) for w_ref in refs[:n_w]]
    if splits is not None:
        edges = [sum(splits[:i]) for i in range(len(splits) + 1)]
        prods = [prods[0][:, lo:hi] for lo, hi in zip(edges[:-1], edges[1:])]
    for p, o_ref in zip(prods, refs[n_w:]):
        o_ref[...] = p


def _layer_spec(w, layer, rows=None, cols=None):
    shape = (rows or w.shape[1], cols or w.shape[2])
    return pl.BlockSpec((None,) + shape, lambda *_: (layer, 0, 0))


def matmul_multi(x, ws, layer, widths=None, splits=None, tm=512):
    n, k = x.shape
    tm = _row_tile(n, tm)
    widths = widths or [w.shape[2] for w in ws]
    outs = splits or widths
    return pl.pallas_call(
        functools.partial(_mm_kernel, n_w=len(ws), splits=splits),
        grid=(n // tm,),
        in_specs=[pl.BlockSpec((tm, k), lambda i: (i, 0))]
        + [_layer_spec(w, layer, cols=c) for w, c in zip(ws, widths)],
        out_specs=[pl.BlockSpec((tm, c), lambda i: (i, 0)) for c in outs],
        out_shape=[jax.ShapeDtypeStruct((n, c), F32) for c in outs],
        compiler_params=_params("parallel"),
        name="matmul_multi",
    )(x, *ws)


def _ffn_kernel(x_ref, w1_ref, w3_ref, w2_ref, g_ref, b_ref, o_ref, xb_ref, acc_ref, *, n_ff):
    xb_ref[...] = x_ref[...].astype(BF16)

    def part(c):
        cols = slice(c * FF_TILE, (c + 1) * FF_TILE)
        xb = xb_ref[...]
        h1 = jnp.dot(xb, w1_ref[:, cols], preferred_element_type=F32)
        h3 = jnp.dot(xb, w3_ref[:, cols], preferred_element_type=F32)
        h = (h1 * _sigmoid(h1)) * h3
        return jnp.dot(h.astype(BF16), w2_ref[cols, :], preferred_element_type=F32)

    acc_ref[...] = part(0)
    for c in range(1, n_ff):
        acc_ref[...] += part(c)
    z = ALPHA * x_ref[...] + 0.5 * acc_ref[...]
    o_ref[...] = _layer_norm(z, g_ref[...], b_ref[...])


def ffn_ln(x, w1, w3, w2, lead, g, b, tm=1024):
    n, d = x.shape
    tm = _row_tile(n, tm)
    n_ff = D_FF // FF_TILE
    resident = lambda a: pl.BlockSpec((None,) * len(lead) + a.shape[len(lead):],
                                      lambda i: lead + (0, 0), pipeline_mode=pl.Buffered(1))
    return pl.pallas_call(
        functools.partial(_ffn_kernel, n_ff=n_ff),
        grid=(n // tm,),
        in_specs=[
            pl.BlockSpec((tm, d), lambda i: (i, 0)),
            resident(w1), resident(w3), resident(w2),
            pl.BlockSpec((1, d), lambda i: (0, 0)),
            pl.BlockSpec((1, d), lambda i: (0, 0)),
        ],
        out_specs=pl.BlockSpec((tm, d), lambda i: (i, 0)),
        out_shape=jax.ShapeDtypeStruct((n, d), F32),
        scratch_shapes=[pltpu.VMEM((tm, d), BF16), pltpu.VMEM((tm, d), F32)],
        compiler_params=_params("parallel"),
        name="ffn_ln",
    )(x, w1, w3, w2, g, b)


def _proj_ln_kernel(x_ref, *refs, n_in):
    a_refs = refs[:n_in]
    w_refs = refs[n_in:2 * n_in]
    g_ref, b_ref, o_ref = refs[2 * n_in:]
    f = None
    for a_ref, w_ref in zip(a_refs, w_refs):
        t = jnp.dot(a_ref[...].astype(BF16), w_ref[...], preferred_element_type=F32)
        f = t if f is None else f + t
    o_ref[...] = _layer_norm(ALPHA * x_ref[...] + f, g_ref[...], b_ref[...])


def proj_ln(x, acts, ws, layer, g, b, tm=512):
    n, d = x.shape
    tm = _row_tile(n, tm)
    n_in = len(acts)
    return pl.pallas_call(
        functools.partial(_proj_ln_kernel, n_in=n_in),
        grid=(n // tm,),
        in_specs=[pl.BlockSpec((tm, d), lambda i: (i, 0))]
        + [pl.BlockSpec((tm, a.shape[1]), lambda i: (i, 0)) for a in acts]
        + [_layer_spec(w, layer) for w in ws]
        + [pl.BlockSpec((1, d), lambda i: (0, 0))] * 2,
        out_specs=pl.BlockSpec((tm, d), lambda i: (i, 0)),
        out_shape=jax.ShapeDtypeStruct((n, d), F32),
        compiler_params=_params("parallel"),
        name="proj_ln",
    )(x, *acts, *ws, g, b)


def _rwkv_prep_kernel(p_ref, prev_ref, mu_ref, w0_ref, wup_ref, a0_ref, aup_ref, gup_ref,
                      kk_ref, ka_ref, rk_ref, seg_ref, *refs, chunked, tiles_per_seq):
    if chunked:
        tril_ref, ones_ref = refs[:2]
        refs = refs[2:]
    g_out, bonus_out = refs[-2:]
    p = p_ref[...]
    if chunked:
        first = (pl.program_id(0) % tiles_per_seq) == 0
        above = jnp.where(first, 0.0, prev_ref[7:8, :])
        top = lax.broadcasted_iota(jnp.int32, p.shape, 0) == 0
        prev = jnp.where(top, above, pltpu.roll(p, 1, 0))
    else:
        prev = prev_ref[...]
    xm = p + (prev - p) * mu_ref[...]
    r = xm[:, 0:RW_WIDTH]
    k = xm[:, RW_WIDTH:2 * RW_WIDTH]
    v = xm[:, 2 * RW_WIDTH:3 * RW_WIDTH]
    wa = xm[:, LORA_WA_START:LORA_G_START]
    gl = xm[:, LORA_G_START:RW_COLS]

    def seg_sum(x):
        return jnp.dot(_split2(x), seg_ref[...], preferred_element_type=F32)

    lw = jnp.dot(jnp.tanh(wa).astype(BF16), wup_ref[...], preferred_element_type=F32)
    la = jnp.dot(wa.astype(BF16), aup_ref[...], preferred_element_type=F32)
    z = -(w0_ref[...] + lw)
    softplus = jnp.maximum(z, 0.0) + jnp.log(1.0 + jnp.exp(-jnp.abs(z)))
    w_log = -softplus - 0.5
    log_decay = -jnp.exp(w_log)
    a = _sigmoid(a0_ref[...] + la)
    g = jnp.dot(_sigmoid(gl).astype(BF16), gup_ref[...], preferred_element_type=F32)
    kk = k * kk_ref[...]
    nrm = jnp.sqrt(seg_sum(kk * kk))
    kk = kk / jnp.maximum(nrm, NORM_EPS)
    k_mod = k * (1.0 + (a - 1.0) * ka_ref[...])
    kka = kk * a
    if chunked:
        parts = _split3(log_decay)

        def time_sum(m_ref):
            s3 = jnp.dot(m_ref[...], parts, preferred_element_type=F32)
            return s3[:, :RW_WIDTH] + s3[:, RW_WIDTH:2 * RW_WIDTH] + s3[:, 2 * RW_WIDTH:]

        cum = time_sum(tril_ref)
        tot = time_sum(ones_ref)
        grow = jnp.exp(-cum)
        rest = jnp.exp(tot - cum)
        outs = (kk * jnp.exp(cum - log_decay), r * jnp.exp(cum), -kka * grow, k_mod * grow,
                -kka * rest, k_mod * rest, v, jnp.exp(tot))
    else:
        outs = (r, jnp.exp(log_decay), k_mod, v, kk, kka)
    for o_ref, val in zip(refs, outs):
        if chunked:
            for pr in range(RW_PAIRS):
                o_ref[pr] = val[:, pr * LANES:(pr + 1) * LANES].astype(o_ref.dtype)
        else:
            o_ref[...] = val.T
    g_out[...] = g
    bonus_out[...] = seg_sum(r * k_mod * rk_ref[...]) * v


def rwkv_prep(p, prev, mu, w0, wup, a0, aup, gup, k_k, k_a, r_k, seg, seq_len, tm=256):
    n = p.shape[0]
    chunked = seq_len > 1
    tm = _row_tile(n, tm)
    row = lambda c: pl.BlockSpec((tm, c), lambda i: (i, 0))
    prev_spec = row(RW_COLS)
    if chunked:
        assert seq_len % tm == 0
        prev_spec = pl.BlockSpec((8, RW_COLS), lambda i: (jnp.maximum(i * (tm // 8) - 1, 0), 0))
    full = lambda a: pl.BlockSpec(a.shape, lambda i: (0, 0))
    consts = (mu, w0, wup, a0, aup, gup, k_k, k_a, r_k, seg)
    dtypes = [F32] * 6
    if chunked:
        assert tm % CHUNK == 0
        t_idx = jnp.arange(tm)
        same = (t_idx[:, None] // CHUNK) == (t_idx[None, :] // CHUNK)
        consts += ((same & (t_idx[None, :] <= t_idx[:, None])).astype(BF16), same.astype(BF16))
        dtypes = [BF16] * 7 + [F32]
    if chunked:
        op_spec = pl.BlockSpec((RW_PAIRS, tm, LANES), lambda i: (0, i, 0))
        op_shape = (RW_PAIRS, n, LANES)
    else:
        op_spec = pl.BlockSpec((RW_WIDTH, tm), lambda i: (0, i))
        op_shape = (RW_WIDTH, n)
    return pl.pallas_call(
        functools.partial(_rwkv_prep_kernel, chunked=chunked, tiles_per_seq=max(seq_len // tm, 1)),
        grid=(n // tm,),
        in_specs=[row(RW_COLS), prev_spec] + [full(c) for c in consts],
        out_specs=[op_spec] * len(dtypes) + [row(RW_WIDTH)] * 2,
        out_shape=[jax.ShapeDtypeStruct(op_shape, dt) for dt in dtypes]
        + [jax.ShapeDtypeStruct((n, RW_WIDTH), F32)] * 2,
        compiler_params=_params("parallel"),
        name="rwkv_prep",
    )(p, prev, *consts)


def _wkv_step_kernel(r_ref, w_ref, k_ref, v_ref, kk_ref, kka_ref, s0_ref, y_ref, s_ref):
    r, w, k, kk, kka = (ref[...][None] for ref in (r_ref, w_ref, k_ref, kk_ref, kka_ref))
    rows = 8
    for c in range(HEAD_DIM // rows):
        vals = slice(c * rows, (c + 1) * rows)
        s = s0_ref[0, vals]
        sa = jnp.sum(s * kk, axis=1, keepdims=True)
        s = s * w - sa * kka + v_ref[vals, :][:, None, :] * k
        s_ref[0, vals] = s
        y_ref[vals, :] = jnp.sum(s * r, axis=1)


def wkv_step(r, w, k, v, kk, kka, s0, layer):
    _, nb = r.shape
    vec = pl.BlockSpec((HEAD_DIM, nb), lambda h: (h, 0))
    return pl.pallas_call(
        _wkv_step_kernel,
        grid=(RW_HEADS,),
        in_specs=[vec] * 6 + [pl.BlockSpec((None, 1, HEAD_DIM, HEAD_DIM, nb), lambda h: (layer, h, 0, 0, 0))],
        out_specs=[vec, pl.BlockSpec((1, HEAD_DIM, HEAD_DIM, nb), lambda h: (h, 0, 0, 0))],
        out_shape=[jax.ShapeDtypeStruct((RW_WIDTH, nb), F32),
                   jax.ShapeDtypeStruct((RW_HEADS, HEAD_DIM, HEAD_DIM, nb), F32)],
        compiler_params=_params("parallel"),
        name="wkv_step",
    )(r, w, k, v, kk, kka, s0)


def _dot_nt(a, b):
    return lax.dot_general(a, b, (((1,), (1,)), ((), ())), preferred_element_type=F32)


def _dot_tn(a, b):
    return lax.dot_general(a, b, (((0,), (0,)), ((), ())), preferred_element_type=F32)


def _dot(a, b):
    return jnp.dot(a, b, preferred_element_type=F32)


def _wkv_chunk_kernel(qt_ref, rt_ref, bh_ref, kh_ref, bb_ref, kb_ref, v_ref, gc_ref, s0_ref,
                      y_ref, s_ref, *, nb, tc):
    @pl.when(pl.program_id(1) == 0)
    def _():
        s_ref[...] = s0_ref[...]

    c = CHUNK
    assert c == HEAD_DIM and 2 * c == LANES
    row = lax.broadcasted_iota(jnp.int32, (c, LANES), 0)
    lane = lax.broadcasted_iota(jnp.int32, (c, LANES), 1)
    head_a = lane < HEAD_DIM
    head_a2 = lax.broadcasted_iota(jnp.int32, (2 * c, LANES), 1) < HEAD_DIM
    strict = (lane % c) < row
    eye = jnp.where((lane % c) == row, 1.0, 0.0)
    row4 = lax.broadcasted_iota(jnp.int32, (c, 2 * LANES), 0)
    col4 = lax.broadcasted_iota(jnp.int32, (c, 2 * LANES), 1)
    kh_cols = (col4 >= c) & (col4 < 3 * c)
    m_mask = kh_cols & ((col4 % c) < row4)
    incl = (col4 % c) <= row4
    pairs = [(b, p) for b in range(nb) for p in range(RW_PAIRS)]
    bf = lambda x: x.astype(BF16)
    stack = lambda *xs: jnp.concatenate(xs, axis=0)
    zero = jnp.zeros((), BF16)
    only_a = lambda x: jnp.where(head_a if x.shape[0] == c else head_a2, x, zero)
    only_b = lambda x: jnp.where(head_a if x.shape[0] == c else head_a2, zero, x)
    per_head = lambda x: stack(only_a(x), only_b(x))
    zeros_c = jnp.zeros((c, LANES), BF16)

    def inverse_stages(ci, ready):
        ld = lambda ref, b, p: ref[p, b, pl.ds(ci * c, c), :]
        qr = [stack(ld(qt_ref, b, p), ld(rt_ref, b, p)) for b, p in pairs]
        bh = [ld(bh_ref, b, p) for b, p in pairs]
        kh = [ld(kh_ref, b, p) for b, p in pairs]
        e = [_dot_nt(q, stack(only_a(stack(x, y)), only_b(stack(y, x)))) for q, x, y in zip(qr, bh, kh)]
        yield
        pw = [jnp.where(strict, jnp.where(head_a, x[:c, :LANES], x[:c, LANES:]), 0.0) for x in e]
        inv = [eye + n for n in pw]
        for _ in range(5):
            pw = [_dot(x, per_head(x)) for x in [bf(x) for x in pw]]
            yield
            inv = [t + _dot(bf(x), per_head(bf(t))) for x, t in zip(pw, inv)]
            yield
        ready[ci] = (qr, e, inv)

    def state_stages(ci, ready):
        qr, e, inv = ready.pop(ci)
        ld = lambda ref, b, p: ref[p, b, pl.ds(ci * c, c), :]
        s = [s_ref[b, p] for b, p in pairs]
        ff = [_dot_nt(q, per_head(bf(x))) for q, x in zip(qr, s)]
        v = [ld(v_ref, b, p) for b, p in pairs]
        yield
        g = [f[:c] + _dot(bf(jnp.where(m_mask, x[:c], 0.0)), stack(zeros_c, only_a(w), only_b(w), zeros_c))
             for f, x, w in zip(ff, e, v)]
        yield
        u = [bf(_dot(bf(t), per_head(bf(x)))) for t, x in zip(inv, g)]
        yield
        y = [f[c:] + _dot(bf(jnp.where(incl, x[c:], 0.0)), stack(only_a(uu), only_a(w), only_b(w), only_b(uu)))
             for f, x, uu, w in zip(ff, e, u, v)]
        add = [_dot_tn(stack(uu, w), stack(ld(bb_ref, b, p), ld(kb_ref, b, p)))
               for uu, w, (b, p) in zip(u, v, pairs)]
        yield
        for i, (b, p) in enumerate(pairs):
            y_ref[p, b, pl.ds(ci * c, c), :] = y[i]
            gc = gc_ref[p, b, pl.ds(ci * c, 8), :]
            kept = (s[i].reshape(c // 8, 8, LANES) * gc).reshape(c, LANES)
            s_ref[b, p] = kept + jnp.where(head_a, add[i][:c], add[i][c:])
        yield

    n_chunks = tc // c
    state_len, inv_len = 5, 11
    ready, inverses, state = {}, {}, None
    for tick in range(-inv_len, state_len * n_chunks):
        for ci in range(n_chunks):
            if ci not in inverses and state_len * ci - inv_len <= tick:
                inverses[ci] = inverse_stages(ci, ready)
        for gen in inverses.values():
            next(gen, None)
        if tick >= 0:
            if tick % state_len == 0:
                state = state_stages(tick // state_len, ready)
            next(state)


def wkv_chunked(qt, rt, bh, kh, bb, kb, v, gc, s0, tc=512):
    _, nb, t, _ = qt.shape
    seq = pl.BlockSpec((RW_PAIRS, nb, tc, LANES), lambda i, j: (0, 0, j, 0))
    st = pl.BlockSpec((nb, RW_PAIRS, HEAD_DIM, LANES), lambda i, j: (0, 0, 0, 0))
    return pl.pallas_call(
        functools.partial(_wkv_chunk_kernel, nb=nb, tc=tc),
        grid=(1, t // tc),
        in_specs=[seq] * 8 + [st],
        out_specs=[seq, st],
        out_shape=[jax.ShapeDtypeStruct((RW_PAIRS, nb, t, LANES), F32),
                   jax.ShapeDtypeStruct(s0.shape, F32)],
        compiler_params=_params("arbitrary", "arbitrary"),
        name="wkv_chunked",
    )(qt, rt, bh, kh, bb, kb, v, gc, s0)


def _mix_out_kernel(x_ref, y_ref, bonus_ref, g_ref, ysw_ref, gng_ref, gnb_ref, seg_ref, lng_ref, lnb_ref,
                    wout_ref, o_ref):
    def seg_mean(x):
        sums = [_dot(_split2(x[:, pr * LANES:(pr + 1) * LANES]), seg_ref[...]) for pr in range(RW_PAIRS)]
        return jnp.concatenate(sums, axis=1) * (1.0 / HEAD_DIM)

    y = jnp.concatenate([y_ref[pr] for pr in range(RW_PAIRS)], axis=1)
    d = y - seg_mean(y)
    var = seg_mean(d * d)
    yn = d * lax.rsqrt(var + GN_EPS) * gng_ref[...] + gnb_ref[...]
    y_rw = (yn + bonus_ref[...]) * g_ref[...]
    f = _dot(y_rw.astype(BF16), wout_ref[:RW_WIDTH, :]) + _dot(ysw_ref[...].astype(BF16), wout_ref[RW_WIDTH:, :])
    o_ref[...] = _layer_norm(ALPHA * x_ref[...] + f, lng_ref[...], lnb_ref[...])


def mix_out_ln(x, y, bonus, g, y_sw, gn_g, gn_b, seg, ln_g, ln_b, w_out, layer, tm=512):
    n, d = x.shape
    tm = _row_tile(n, tm)
    row = lambda c: pl.BlockSpec((tm, c), lambda i: (i, 0))
    full = lambda a: pl.BlockSpec(a.shape, lambda i: (0, 0))
    consts = (gn_g, gn_b, seg, ln_g, ln_b)
    return pl.pallas_call(
        _mix_out_kernel,
        grid=(n // tm,),
        in_specs=[row(d), pl.BlockSpec((RW_PAIRS, tm, LANES), lambda i: (0, i, 0)), row(RW_WIDTH),
                  row(RW_WIDTH), row(SW_WIDTH)] + [full(c) for c in consts] + [_layer_spec(w_out, layer)],
        out_specs=row(d),
        out_shape=jax.ShapeDtypeStruct((n, d), F32),
        compiler_params=_params("parallel"),
        name="mix_out_ln",
    )(x, y, bonus, g, y_sw, *consts, w_out)


def _alibi_slope(h):
    return 2.0 ** (-8.0 * (h + 1) / SW_HEADS)


def _swa_prompt_kernel(sink_ref, bias0_ref, bias1_ref, q_ref, kp_ref, kc_ref, vp_ref, vc_ref, o_ref, *, qb_per_step):
    qb = (q_ref[0] * HEAD_DIM ** -0.5).astype(BF16)
    kw = jnp.concatenate([kp_ref[0], kc_ref[0]], axis=0).astype(BF16)
    vw = jnp.concatenate([vp_ref[0], vc_ref[0]], axis=0).astype(BF16)
    kx = pltpu.roll(kw, HEAD_DIM, 1)
    vx = pltpu.roll(vw, HEAD_DIM, 1)
    low_kv = lax.broadcasted_iota(jnp.int32, kw.shape, 1) < HEAD_DIM
    low_q = lax.broadcasted_iota(jnp.int32, (WINDOW, LANES), 1) < HEAD_DIM
    zero = jnp.zeros((), BF16)
    kc = [jnp.where(low_kv, kw, kx), jnp.where(low_kv, kx, kw)]
    vc = [jnp.where(low_kv, vw, vx), jnp.where(low_kv, vx, vw)]
    items = [(j, h) for j in range(qb_per_step) for h in range(SW_HEADS)]
    rows = lambda j: slice(j * WINDOW, (j + 1) * WINDOW)
    keys = lambda j: slice(j * WINDOW, (j + 2) * WINDOW)
    bias = lambda j, h: bias0_ref[h] if j == 0 else bias1_ref[h]
    qh = []
    for j, h in items:
        q2 = qb[rows(j), (h // 2) * LANES:(h // 2 + 1) * LANES]
        qh.append(jnp.where(low_q, q2, zero) if h % 2 == 0 else jnp.where(low_q, zero, q2))
    logits = [_dot_nt(x, kc[h // SW_GROUP][keys(j)]) + bias(j, h) for x, (j, h) in zip(qh, items)]
    m = [jnp.maximum(jnp.max(x, axis=-1, keepdims=True), sink_ref[h]) for x, (j, h) in zip(logits, items)]
    e = [jnp.exp(x - mx) for x, mx in zip(logits, m)]
    den = [jnp.sum(x, axis=-1, keepdims=True) + jnp.exp(sink_ref[h] - mx) for x, mx, (j, h) in zip(e, m, items)]
    o = [_dot(x.astype(BF16), vc[h // SW_GROUP][keys(j)]) / dn for x, dn, (j, h) in zip(e, den, items)]
    for i in range(0, len(items), 2):
        j, h = items[i]
        o_ref[0, rows(j), (h // 2) * LANES:(h // 2 + 1) * LANES] = jnp.where(low_q, o[i], o[i + 1])


def swa_prompt(q, k, v, sinks, qb_per_step=8):
    nb, t, _ = q.shape
    rows = qb_per_step * WINDOW
    assert t % rows == 0
    i = jnp.arange(WINDOW)[:, None]
    j = jnp.arange(2 * WINDOW)[None, :]
    dist = WINDOW + i - j
    valid = (dist >= 0) & (dist < WINDOW)
    slopes = jnp.asarray([_alibi_slope(h) for h in range(SW_HEADS)], F32)[:, None, None]
    table = lambda ok: jnp.where(ok[None], -slopes * dist.astype(F32)[None], -jnp.inf)
    bias = jnp.stack([table(valid & (j >= WINDOW)), table(valid)])
    table_spec = lambda pick: pl.BlockSpec((None, SW_HEADS, WINDOW, 2 * WINDOW), lambda b, n: (pick(n), 0, 0, 0))
    cur = lambda w: pl.BlockSpec((1, rows, w), lambda b, n: (b, n, 0))
    prv = lambda w: pl.BlockSpec((1, WINDOW, w), lambda b, n: (b, jnp.maximum(n * qb_per_step - 1, 0), 0))
    return pl.pallas_call(
        functools.partial(_swa_prompt_kernel, qb_per_step=qb_per_step),
        grid=(nb, t // rows),
        in_specs=[pl.BlockSpec(memory_space=pltpu.SMEM),
                  table_spec(lambda n: jnp.minimum(n, 1)), table_spec(lambda n: 1),
                  cur(SW_WIDTH), prv(KV_WIDTH), cur(KV_WIDTH), prv(KV_WIDTH), cur(KV_WIDTH)],
        out_specs=cur(SW_WIDTH),
        out_shape=jax.ShapeDtypeStruct((nb, t, SW_WIDTH), F32),
        compiler_params=_params("parallel", "parallel"),
        name="swa_prompt",
    )(sinks, bias, bias, q, k, k, v, v)


def _swa_sample_kernel(sink_ref, slope_ref, q_ref, kn_ref, vn_ref, ck_ref, cv_ref,
                       o_ref, nk_ref, nv_ref, *, bb):
    last = lax.broadcasted_iota(jnp.int32, (KV_WIDTH, WINDOW), 1) == WINDOW - 1
    j = lax.broadcasted_iota(jnp.int32, (SW_HEADS, WINDOW), 1)
    bias = slope_ref[...] * (WINDOW - 1 - j).astype(F32)
    sink = sink_ref[...]
    seqs = range(bb)
    flat = lambda ref, b: ref[b].reshape(KV_WIDTH, ref.shape[-1])
    kw = [jnp.where(last, flat(kn_ref, b), pltpu.roll(flat(ck_ref, b), WINDOW - 1, 1)) for b in seqs]
    vw = [jnp.where(last, flat(vn_ref, b), pltpu.roll(flat(cv_ref, b), WINDOW - 1, 1)) for b in seqs]
    s = [_dot(q_ref[b].astype(BF16), kw[b].astype(BF16)) * HEAD_DIM ** -0.5 - bias for b in seqs]
    m = [jnp.maximum(jnp.max(x, axis=-1, keepdims=True), sink) for x in s]
    e = [jnp.exp(x - mx) for x, mx in zip(s, m)]
    prob = [x / (jnp.sum(x, axis=-1, keepdims=True) + jnp.exp(sink - mx)) for x, mx in zip(e, m)]
    o = [_dot_nt(prob[b].astype(BF16), vw[b].astype(BF16)) for b in seqs]
    for b in seqs:
        nk_ref[b] = kw[b].reshape(SW_KV_HEADS, HEAD_DIM, WINDOW)
        nv_ref[b] = vw[b].reshape(SW_KV_HEADS, HEAD_DIM, WINDOW)
        o_ref[b] = o[b]


def swa_sample(q, kn, vn, ck, cv, layer, sinks, slopes, bb=8):
    nb = q.shape[0]
    assert nb % bb == 0
    own = (jnp.arange(SW_HEADS) // SW_GROUP)[:, None] == jnp.arange(SW_KV_HEADS)[None, :]
    q2 = jnp.where(own[None, :, :, None], q[:, :, None, :], 0.0).reshape(nb, SW_HEADS, KV_WIDTH)
    full = lambda a: pl.BlockSpec(a.shape, lambda b: (0, 0))
    heads = pl.BlockSpec((bb, SW_HEADS, KV_WIDTH), lambda b: (b, 0, 0))
    new = pl.BlockSpec((bb, SW_KV_HEADS, HEAD_DIM, 1), lambda b: (b, 0, 0, 0))
    cache = pl.BlockSpec((None, bb, SW_KV_HEADS, HEAD_DIM, WINDOW), lambda b: (layer, b, 0, 0, 0))
    win = pl.BlockSpec((bb, SW_KV_HEADS, HEAD_DIM, WINDOW), lambda b: (b, 0, 0, 0))
    win_shape = jax.ShapeDtypeStruct((nb, SW_KV_HEADS, HEAD_DIM, WINDOW), F32)
    o2, nk, nv = pl.pallas_call(
        functools.partial(_swa_sample_kernel, bb=bb),
        grid=(nb // bb,),
        in_specs=[full(sinks), full(slopes), heads, new, new, cache, cache],
        out_specs=[heads, win, win],
        out_shape=[jax.ShapeDtypeStruct((nb, SW_HEADS, KV_WIDTH), F32), win_shape, win_shape],
        compiler_params=_params("parallel"),
        name="swa_sample",
    )(sinks, slopes, q2, kn, vn, ck, cv)
    o = jnp.sum(jnp.where(own[None, :, :, None], o2.reshape(nb, SW_HEADS, SW_KV_HEADS, HEAD_DIM), 0.0), axis=2)
    return o, nk, nv


def _mem_block_kernel(x_ref, wq_ref, mk_ref, mv_ref, wo_ref, g_ref, b_ref, o_ref):
    x = x_ref[0]
    qb = _dot(x.astype(BF16), wq_ref[...]).astype(BF16)
    cols = [slice(h * MEM_HEAD_DIM, (h + 1) * MEM_HEAD_DIM) for h in range(MEM_HEADS)]
    s = [_dot_nt(qb[:, c], mk_ref[0, :, c].astype(BF16)) * MEM_HEAD_DIM ** -0.5 for c in cols]
    e = [jnp.exp(x - jnp.max(x, axis=-1, keepdims=True)) for x in s]
    den = [jnp.sum(x, axis=-1, keepdims=True) for x in e]
    outs = [_dot(x.astype(BF16), mv_ref[0, :, c].astype(BF16)) / d for x, c, d in zip(e, cols, den)]
    o = jnp.concatenate(outs, axis=1).astype(BF16)
    o_ref[0] = _layer_norm(ALPHA * x + _dot(o, wo_ref[...]), g_ref[...], b_ref[...])


def mem_block(x, wq, mk, mv, wo, layer, g, b, tm=512):
    ng, t, d = x.shape
    tm = _row_tile(t, tm)
    row = pl.BlockSpec((1, tm, d), lambda gi, i: (gi, i, 0))
    mem = pl.BlockSpec((1, MEM_TOKENS, d), lambda gi, i: (gi, 0, 0))
    full = lambda a: pl.BlockSpec(a.shape, lambda gi, i: (0, 0))
    return pl.pallas_call(
        _mem_block_kernel,
        grid=(ng, t // tm),
        in_specs=[row, _layer_spec(wq, layer), mem, mem, _layer_spec(wo, layer), full(g), full(b)],
        out_specs=row,
        out_shape=jax.ShapeDtypeStruct((ng, t, d), F32),
        compiler_params=_params("parallel", "parallel"),
        name="mem_block",
    )(x, wq, mk, mv, wo, g, b)


def _mem_attn_token_kernel(q_ref, mk_ref, mv_ref, o_ref, *, bb):
    rows = MEM_TOKENS * MEM_HEADS
    lane = lax.broadcasted_iota(jnp.int32, (MEM_HEADS, rows), 1)
    head = lax.broadcasted_iota(jnp.int32, (MEM_HEADS, rows), 0)
    own = (lane % MEM_HEADS) == head
    for b in range(bb):
        q = q_ref[b]
        q4 = jnp.concatenate([q[:, h * MEM_HEAD_DIM:(h + 1) * MEM_HEAD_DIM] for h in range(MEM_HEADS)],
                             axis=0).astype(BF16)
        k2 = mk_ref[b].reshape(rows, MEM_HEAD_DIM).astype(BF16)
        v2 = mv_ref[b].reshape(rows, MEM_HEAD_DIM).astype(BF16)
        s = jnp.where(own, _dot_nt(q4, k2) * MEM_HEAD_DIM ** -0.5, -jnp.inf)
        m = jnp.max(s, axis=-1, keepdims=True)
        e = jnp.exp(s - m)
        prob = e / jnp.sum(e, axis=-1, keepdims=True)
        o = _dot(prob.astype(BF16), v2)
        for h in range(MEM_HEADS):
            o_ref[b, :, h * MEM_HEAD_DIM:(h + 1) * MEM_HEAD_DIM] = o[h:h + 1]


def mem_attn_token(q, mk, mv, layer, bb=8):
    ng, _, d = q.shape
    assert ng % bb == 0
    row = pl.BlockSpec((bb, 1, d), lambda g: (g, 0, 0))
    mem = pl.BlockSpec((None, bb, MEM_TOKENS, MEM_HEADS, MEM_HEAD_DIM), lambda g: (layer, g, 0, 0, 0))
    return pl.pallas_call(
        functools.partial(_mem_attn_token_kernel, bb=bb),
        grid=(ng // bb,),
        in_specs=[row, mem, mem],
        out_specs=row,
        out_shape=jax.ShapeDtypeStruct((ng, 1, d), F32),
        compiler_params=_params("parallel"),
        name="mem_attn_token",
    )(q, mk, mv)


def _unpair_state(s):
    nb = s.shape[0]
    s = s.reshape(nb, RW_PAIRS, HEAD_DIM, 2, HEAD_DIM)
    return jnp.swapaxes(s, 2, 3).reshape(nb, RW_HEADS, HEAD_DIM, HEAD_DIM)


def kernel(x_prompt, x_sample, mem_prompt, state_wkv, state_shift, cache_win_k, cache_win_v,
           cache_mem_k, cache_mem_v, ln_g, ln_b, ffn_w1, ffn_w3, ffn_w2, w_in, rw_mu, rw_w0,
           rw_w_up, rw_a0, rw_a_up, rw_g_up, rw_k_k, rw_k_a, rw_r_k, rw_gn_g, rw_gn_b, sw_sinks,
           w_out, mem_wq, mem_wk, mem_wv, mem_wo):
    depth = ln_g.shape[0]
    bp, tp, d = x_prompt.shape
    bs, ts, _ = x_sample.shape
    assert ts == 1 and cache_win_k.shape[2] == WINDOW and tp % WINDOW == 0

    w1b, w3b, w2b = (w.astype(BF16) for w in (ffn_w1, ffn_w3, ffn_w2))
    w_in_b = w_in.astype(BF16)
    w_out_b = w_out.astype(BF16)
    wqb, wkb, wvb, wob = (w.astype(BF16) for w in (mem_wq, mem_wk, mem_wv, mem_wo))
    zpad = jnp.zeros((depth, D_W_LORA, RW_WIDTH), BF16)
    wup_b = jnp.concatenate([rw_w_up.astype(BF16), zpad], axis=1)
    aup_b = jnp.concatenate([zpad, rw_a_up.astype(BF16)], axis=1)
    gup_b = rw_g_up.astype(BF16)
    hid = jnp.arange(2 * RW_WIDTH) % RW_WIDTH // HEAD_DIM
    seg2 = (hid[:, None] == (jnp.arange(RW_WIDTH) // HEAD_DIM)[None, :]).astype(BF16)
    seg_pair = ((jnp.arange(2 * LANES) % LANES // HEAD_DIM)[:, None]
                == (jnp.arange(LANES) // HEAD_DIM)[None, :]).astype(BF16)
    slopes = jnp.asarray([[_alibi_slope(h)] for h in range(SW_HEADS)], F32)
    row = lambda a: a.reshape(1, -1)

    def layer(l, x, nb, t, prev_fn, s0, swa_fn, mem_fn):
        x = ffn_ln(x, w1b, w3b, w2b, (l, 0), row(ln_g[l, 0]), row(ln_b[l, 0]))
        p_rw, q, k, v = matmul_multi(x, [w_in_b], l, splits=[RW_COLS, SW_WIDTH, KV_WIDTH, KV_WIDTH])
        *ops, g, bonus = rwkv_prep(
            p_rw, prev_fn(p_rw), row(rw_mu[l]), row(rw_w0[l]), wup_b[l], row(rw_a0[l]), aup_b[l],
            gup_b[l], row(rw_k_k[l]), row(rw_k_a[l]), row(rw_r_k[l]), seg2, seq_len=t)
        if t > 1:
            y, s_fin = wkv_chunked(*[a.reshape(RW_PAIRS, nb, t, LANES) for a in ops], s0)
            y, s_fin = y.reshape(RW_PAIRS, nb * t, LANES), _unpair_state(s_fin)
        else:
            y, s_fin = wkv_step(*ops, s0, l)
            y = jnp.swapaxes(y.T.reshape(nb, RW_PAIRS, LANES), 0, 1)
        y_sw, win_k, win_v = swa_fn(q, k, v)
        shift = x.reshape(nb, t, d)[:, -1]
        x = mix_out_ln(x, y, bonus, g, y_sw, row(rw_gn_g[l]),
                       row(rw_gn_b[l]), seg_pair, row(ln_g[l, 1]), row(ln_b[l, 1]), w_out_b, l)
        x = mem_fn(x)
        x = ffn_ln(x, w1b, w3b, w2b, (l, 1), row(ln_g[l, 3]), row(ln_b[l, 3]))
        return x, s_fin, shift, win_k, win_v

    xp = x_prompt.reshape(bp * tp, d)
    p_wkv, p_shift, p_wk, p_wv, p_mk, p_mv = [], [], [], [], [], []
    for l in range(depth):
        prev_prompt = lambda p_rw: p_rw

        def swa_p(q, k, v, l=l):
            k3 = k.reshape(bp, tp, KV_WIDTH)
            v3 = v.reshape(bp, tp, KV_WIDTH)
            y = swa_prompt(q.reshape(bp, tp, SW_WIDTH), k3, v3, sw_sinks[l])
            tail = lambda a: a[:, -WINDOW:].reshape(bp, WINDOW, SW_KV_HEADS, HEAD_DIM)
            return y.reshape(bp * tp, SW_WIDTH), tail(k3), tail(v3)

        mk, mv = matmul_multi(mem_prompt.reshape(bp * MEM_TOKENS, d), [wkb, wvb], l)
        mk = mk.reshape(bp, MEM_TOKENS, d)
        mv = mv.reshape(bp, MEM_TOKENS, d)

        def mem_p(x, l=l, mk=mk, mv=mv):
            return mem_block(x.reshape(bp, tp, d), wqb, mk, mv, wob, l, row(ln_g[l, 2]),
                             row(ln_b[l, 2])).reshape(bp * tp, d)

        s0 = jnp.zeros((bp, RW_PAIRS, HEAD_DIM, LANES), F32)
        xp, s_fin, shift, wk_, wv_ = layer(l, xp, bp, tp, prev_prompt, s0, swa_p, mem_p)
        p_wkv.append(s_fin)
        p_shift.append(shift)
        p_wk.append(wk_)
        p_wv.append(wv_)
        p_mk.append(mk.reshape(bp, MEM_TOKENS, MEM_HEADS, MEM_HEAD_DIM))
        p_mv.append(mv.reshape(bp, MEM_TOKENS, MEM_HEADS, MEM_HEAD_DIM))

    xs = x_sample.reshape(bs, d)
    state_seq_minor = jnp.transpose(state_wkv, (0, 2, 3, 4, 1))
    win_k_pos_minor = jnp.transpose(cache_win_k, (0, 1, 3, 4, 2))
    win_v_pos_minor = jnp.transpose(cache_win_v, (0, 1, 3, 4, 2))
    s_wkv, s_shift, s_wk, s_wv = [], [], [], []
    for l in range(depth):
        def prev_sample(p_rw, l=l):
            (prev,) = matmul_multi(state_shift[l], [w_in_b], l, widths=[RW_COLS])
            return prev

        def swa_s(q, k, v, l=l):
            col = lambda a: a.reshape(bs, SW_KV_HEADS, HEAD_DIM, 1)
            o, nk, nv = swa_sample(q.reshape(bs, SW_HEADS, HEAD_DIM), col(k), col(v), win_k_pos_minor,
                                   win_v_pos_minor, l, sw_sinks[l].reshape(SW_HEADS, 1), slopes)
            return o.reshape(bs, SW_WIDTH), nk, nv

        def mem_s(x, l=l):
            (qm,) = matmul_multi(x, [wqb], l)
            o = mem_attn_token(qm.reshape(bs, 1, d), cache_mem_k, cache_mem_v, l).reshape(bs, d)
            return proj_ln(x, [o], [wob], l, row(ln_g[l, 2]), row(ln_b[l, 2]))

        xs, s_fin, shift, wk_, wv_ = layer(l, xs, bs, 1, prev_sample, state_seq_minor, swa_s, mem_s)
        s_wkv.append(s_fin)
        s_shift.append(shift)
        s_wk.append(wk_)
        s_wv.append(wv_)

    return (xp.reshape(bp, tp, d), xs.reshape(bs, 1, d),
            jnp.stack(p_wkv), jnp.stack(p_shift), jnp.stack(p_wk), jnp.stack(p_wv),
            jnp.stack(p_mk), jnp.stack(p_mv),
            jnp.transpose(jnp.stack(s_wkv), (0, 4, 1, 2, 3)), jnp.stack(s_shift),
            jnp.transpose(jnp.stack(s_wk), (0, 1, 4, 2, 3)), jnp.transpose(jnp.stack(s_wv), (0, 1, 4, 2, 3)))
```
